```python
import math
import jax
import jax.numpy as jnp
from jax import lax
import numpy as np

D_MODEL = 2048
BATCH = 8
SEQ = 4096
DEPTH = 2
DEC_BATCH = 8
DEC_SEQ = 64
PAST_LEN = 1024

CHUNK = 64
N_EVEN = (DEPTH + 1) // 2
N_ODD = DEPTH // 2

A_HEADS = 8
A_HEAD_DIM = 128
A_WIDTH = A_HEADS * A_HEAD_DIM
IDX_HEADS = 16
IDX_DIM = 64
IDX_W_SCALE = (IDX_HEADS * IDX_DIM) ** -0.5
TOPK_MAX = 256
A_QBLOCK = 32
N_BUCKETS = 32
MAX_DISTANCE = 128

B_HEAD_DIM = 64
B_WIDTH = D_MODEL
B_HEADS = B_WIDTH // B_HEAD_DIM
B_GROUPS = 4
B_HPG = B_HEADS // B_GROUPS
B_STATE = 128
CONV_W = 4
B_CONV_DIM = B_WIDTH + 2 * B_GROUPS * B_STATE
DT_MIN = 1e-3
DT_MAX = 1e-1

C_HEADS = 16
Q_LORA = 512
KV_LORA = 512
QK_NOPE = 128
QK_ROPE = 64
V_DIM = 128
C_WIDTH = C_HEADS * V_DIM
ROPE_THETA = 10000.0
MLA_SCALE = (QK_NOPE + QK_ROPE) ** -0.5
C_QBLOCK = 128

ALPHA = (2 * DEPTH) ** 0.25
BETA = (8 * DEPTH) ** -0.25
EPS = 1e-5

SPLIT0 = (A_WIDTH, A_WIDTH, A_WIDTH, A_WIDTH, IDX_HEADS * IDX_DIM, IDX_DIM, IDX_HEADS, B_WIDTH, B_CONV_DIM, B_HEADS)
SPLIT1 = (Q_LORA, KV_LORA, QK_ROPE, C_WIDTH)
W_IN0 = sum(SPLIT0)
W_IN1 = sum(SPLIT1)
F32 = jnp.float32

kernel_name = 'hybrid_dsa_ssd_mla_stream_step'


def split_cols(h, sizes):
    return jnp.split(h, np.cumsum(sizes)[:-1].tolist(), axis=-1)


def layer_norm(x, g, b):
    xf = x.astype(F32)
    mu = jnp.mean(xf, axis=-1, keepdims=True)
    var = jnp.mean(jnp.square(xf - mu), axis=-1, keepdims=True)
    return ((xf - mu) * lax.rsqrt(var + EPS) * g.astype(F32) + b.astype(F32)).astype(x.dtype)


def rms_norm(x, g):
    xf = x.astype(F32)
    return (xf * lax.rsqrt(jnp.mean(jnp.square(xf), axis=-1, keepdims=True) + EPS) * g.astype(F32)).astype(x.dtype)


def chunk_visible(q_pos, k_pos):
    return (k_pos[None, :] // CHUNK) <= (q_pos[:, None] // CHUNK)


def to_blocks(a, size):
    b, t = a.shape[:2]
    return jnp.moveaxis(a.reshape((b, t // size, size) + a.shape[2:]), 1, 0)


def from_blocks(a):
    nb, b, size = a.shape[:3]
    return jnp.moveaxis(a, 0, 1).reshape((b, nb * size) + a.shape[3:])


def t5_bucket(rel):
    half = N_BUCKETS // 2
    max_exact = half // 2
    ret = jnp.where(rel < 0, half, 0)
    n = jnp.abs(rel)
    nf = jnp.maximum(n, 1).astype(F32)
    large = max_exact + (jnp.log(nf / max_exact) / math.log(MAX_DISTANCE / max_exact) * (half - max_exact)).astype(jnp.int32)
    large = jnp.minimum(large, half - 1)
    return ret + jnp.where(n < max_exact, n, large)


def rope(x, pos):
    half = QK_ROPE // 2
    inv = ROPE_THETA ** (-jnp.arange(half, dtype=F32) / half)
    ang = pos.astype(F32)[:, None] * inv[None, :]
    shape = (1, x.shape[1]) + (1,) * (x.ndim - 3) + (half,)
    cos = jnp.cos(ang).reshape(shape)
    sin = jnp.sin(ang).reshape(shape)
    x1 = x[..., :half].astype(F32)
    x2 = x[..., half:].astype(F32)
    return jnp.concatenate([x1 * cos - x2 * sin, x1 * sin + x2 * cos], axis=-1).astype(x.dtype)


def dsa_attend(q, qi, wi, q_pos, k, v, ki, k_pos, n_sel, bias_table):
    dots = jnp.einsum('bqhd,bld->bqhl', qi.astype(F32), ki.astype(F32))
    score = jnp.einsum('bqh,bqhl->bql', wi.astype(F32), jax.nn.relu(dots))
    score = jnp.where(chunk_visible(q_pos, k_pos)[None], score, -jnp.inf)
    _, sel = lax.top_k(score, n_sel)
    sel_pos = k_pos[sel]
    valid = (sel_pos // CHUNK) <= (q_pos[None, :, None] // CHUNK)
    kg = jax.vmap(lambda kb, ib: kb[ib])(k, sel)
    vg = jax.vmap(lambda vb, ib: vb[ib])(v, sel)
    logits = jnp.einsum('bqhd,bqnhd->bqhn', q, kg).astype(F32) * (A_HEAD_DIM ** -0.5)
    bias = bias_table[t5_bucket(q_pos[None, :, None] - sel_pos)]
    logits = logits + jnp.swapaxes(bias, -1, -2).astype(F32)
    logits = jnp.where(valid[:, :, None, :], logits, -jnp.inf)
    p = jax.nn.softmax(logits, axis=-1).astype(v.dtype)
    return jnp.einsum('bqhn,bqnhd->bqhd', p, vg)


def dsa_prompt(q, qi, wi, k, v, ki, bias_table):
    t = q.shape[1]
    n_sel = min(TOPK_MAX, t // 4)
    pos = jnp.arange(t, dtype=jnp.int32)

    def step(args):
        qb, qib, wib, pb = args
        return dsa_attend(qb, qib, wib, pb, k, v, ki, pos, n_sel, bias_table)

    out = lax.map(step, (to_blocks(q, A_QBLOCK), to_blocks(qi, A_QBLOCK), to_blocks(wi, A_QBLOCK),
                         pos.reshape(t // A_QBLOCK, A_QBLOCK)))
    return from_blocks(out)


def causal_conv(xpad, w, b):
    t = xpad.shape[1] - (CONV_W - 1)
    return b + sum(xpad[:, j:j + t] * w[j] for j in range(CONV_W))


def ssd_chunk(h0, x, dt, bm, cm, a):
    x = x.astype(F32)
    bm = bm.astype(F32)
    cm = cm.astype(F32)
    l = x.shape[1]
    acum = jnp.cumsum(dt * a, axis=1)
    seg = acum[:, :, None] - acum[:, None]
    causal = jnp.tril(jnp.ones((l, l), dtype=bool))[None, :, :, None, None]
    decay = jnp.exp(jnp.where(causal, seg, -jnp.inf))
    cb = jnp.einsum('btgn,bsgn->btsg', cm, bm)
    w = cb[..., None] * decay * dt[:, None]
    y = jnp.einsum('btsgr,bsgrp->btgrp', w, x)
    y = y + jnp.einsum('btgn,bgrpn->btgrp', cm, h0) * jnp.exp(acum)[..., None]
    tail = jnp.exp(acum[:, -1:] - acum) * dt
    h = h0 * jnp.exp(acum[:, -1])[..., None, None] + jnp.einsum('bsgr,bsgn,bsgrp->bgrpn', tail, bm, x)
    return h, y


def mamba_mix(z, xbc_pad, dt_raw, h0, conv_w, conv_b, dt_bias, a_log, d_skip, norm_w, scan_chunks):
    b, t = z.shape[:2]
    xbc = jax.nn.silu(causal_conv(xbc_pad, conv_w, conv_b))
    xs, bm, cm = split_cols(xbc, (B_WIDTH, B_GROUPS * B_STATE, B_GROUPS * B_STATE))
    xs = xs.reshape(b, t, B_GROUPS, B_HPG, B_HEAD_DIM)
    bm = bm.reshape(b, t, B_GROUPS, B_STATE)
    cm = cm.reshape(b, t, B_GROUPS, B_STATE)
    dt = jax.nn.softplus((dt_raw + dt_bias).astype(F32)).reshape(b, t, B_GROUPS, B_HPG)
    a = -jnp.exp(a_log.astype(F32)).reshape(B_GROUPS, B_HPG)
    hinit = h0.astype(F32).reshape(b, B_GROUPS, B_HPG, B_HEAD_DIM, B_STATE)
    if scan_chunks:
        h, ys = lax.scan(lambda hc, inp: ssd_chunk(hc, inp[0], inp[1], inp[2], inp[3], a), hinit,
                         (to_blocks(xs, CHUNK), to_blocks(dt, CHUNK), to_blocks(bm, CHUNK), to_blocks(cm, CHUNK)))
        y = from_blocks(ys)
    else:
        h, y = ssd_chunk(hinit, xs, dt, bm, cm, a)
    y = y + d_skip.astype(F32).reshape(B_GROUPS, B_HPG)[..., None] * xs.astype(F32)
    y = y.reshape(b, t, B_WIDTH) * jax.nn.silu(z.astype(F32))
    yg = y.reshape(b, t, B_GROUPS, B_WIDTH // B_GROUPS)
    yg = yg * lax.rsqrt(jnp.mean(jnp.square(yg), axis=-1, keepdims=True) + EPS)
    y = (yg.reshape(b, t, B_WIDTH) * norm_w.astype(F32)).astype(z.dtype)
    return y, h.reshape(b, B_HEADS, B_HEAD_DIM, B_STATE).astype(h0.dtype)


def even_layer(x, past, w_in, w_out, conv_w, conv_b, dt_bias, a_log, d_skip, norm_w, ln_g, ln_b, t5_bias):
    b, t, _ = x.shape
    aq, ak, av, ag, iq, ik, iw, bz, bxbc, bdt = split_cols(x @ w_in, SPLIT0)
    aq = aq.reshape(b, t, A_HEADS, A_HEAD_DIM)
    ak = ak.reshape(b, t, A_HEADS, A_HEAD_DIM)
    av = av.reshape(b, t, A_HEADS, A_HEAD_DIM)
    iq = iq.reshape(b, t, IDX_HEADS, IDX_DIM)
    iw = iw * IDX_W_SCALE
    if past is None:
        a_out = dsa_prompt(aq, iq, iw, ak, av, ik, t5_bias)
        xbc_pad = jnp.pad(bxbc, ((0, 0), (CONV_W - 1, 0), (0, 0)))
        h0 = jnp.zeros((b, B_HEADS, B_HEAD_DIM, B_STATE), x.dtype)
        scan_chunks = True
    else:
        pk, pv, pki, pconv, pssm = past
        p_len = pk.shape[1]
        q_pos = p_len + jnp.arange(t, dtype=jnp.int32)
        k_pos = jnp.arange(p_len + t, dtype=jnp.int32)
        n_sel = min(TOPK_MAX, (p_len + t) // 4)
        a_out = dsa_attend(aq, iq, iw, q_pos,
                           jnp.concatenate([pk, ak], axis=1), jnp.concatenate([pv, av], axis=1),
                           jnp.concatenate([pki, ik], axis=1), k_pos, n_sel, t5_bias)
        xbc_pad = jnp.concatenate([pconv, bxbc], axis=1)
        h0 = pssm
        scan_chunks = False
    b_out, ssm_new = mamba_mix(bz, xbc_pad, bdt, h0, conv_w, conv_b, dt_bias, a_log, d_skip, norm_w, scan_chunks)
    a_out = a_out.reshape(b, t, A_WIDTH) * jax.nn.silu(ag)
    mix = jnp.concatenate([a_out, b_out], axis=-1) @ w_out
    y = layer_norm(ALPHA * x + mix, ln_g, ln_b)
    return y, (ak, av, ik, xbc_pad[:, -(CONV_W - 1):], ssm_new)


def mla_attend(q_nope, q_rope, q_pos, k_nope, k_rope, v, k_pos):
    s = jnp.einsum('bqhd,bkhd->bhqk', q_nope, k_nope) + jnp.einsum('bqhd,bkd->bhqk', q_rope, k_rope)
    s = jnp.where(chunk_visible(q_pos, k_pos)[None, None], s.astype(F32) * MLA_SCALE, -jnp.inf)
    p = jax.nn.softmax(s, axis=-1).astype(v.dtype)
    return jnp.einsum('bhqk,bkhd->bqhd', p, v)


def mla_prompt(q_nope, q_rope, k_nope, k_rope, v):
    t = q_nope.shape[1]
    pos = jnp.arange(t, dtype=jnp.int32)

    def step(args):
        qn, qr, pb = args
        return mla_attend(qn, qr, pb, k_nope, k_rope, v, pos)

    out = lax.map(step, (to_blocks(q_nope, C_QBLOCK), to_blocks(q_rope, C_QBLOCK),
                         pos.reshape(t // C_QBLOCK, C_QBLOCK)))
    return from_blocks(out)


def odd_layer(x, past, w_in, q_norm_w, kv_norm_w, w_uq, w_ukv, w_out, ln_g, ln_b):
    b, t, _ = x.shape
    cq, ckv, kr, gate = split_cols(x @ w_in, SPLIT1)
    cq = rms_norm(cq, q_norm_w)
    ckv = rms_norm(ckv, kv_norm_w)
    q = (cq @ w_uq).reshape(b, t, C_HEADS, QK_NOPE + QK_ROPE)
    p_len = 0 if past is None else past[0].shape[1]
    q_pos = p_len + jnp.arange(t, dtype=jnp.int32)
    q_nope = q[..., :QK_NOPE]
    q_rope = rope(q[..., QK_NOPE:], q_pos)
    kr = rope(kr, q_pos)
    if past is None:
        lat, kr_all = ckv, kr
    else:
        lat = jnp.concatenate([past[0], ckv], axis=1)
        kr_all = jnp.concatenate([past[1], kr], axis=1)
    kv = (lat @ w_ukv).reshape(b, p_len + t, C_HEADS, QK_NOPE + V_DIM)
    k_nope = kv[..., :QK_NOPE]
    v = kv[..., QK_NOPE:]
    if past is None:
        o = mla_prompt(q_nope, q_rope, k_nope, kr_all, v)
    else:
        o = mla_attend(q_nope, q_rope, q_pos, k_nope, kr_all, v, jnp.arange(p_len + t, dtype=jnp.int32))
    o = o.reshape(b, t, C_WIDTH) * jax.nn.silu(gate)
    y = layer_norm(ALPHA * x + o @ w_out, ln_g, ln_b)
    return y, (ckv, kr)


def setup_inputs(seed: int = 0) -> dict:
    key = jax.random.key(seed)
    ks = iter(jax.random.split(key, 32))

    def nrm(shape, scale=1.0):
        return jax.random.normal(next(ks), shape, F32) * scale

    x_prompt = nrm((BATCH, SEQ, D_MODEL))
    x_sample = nrm((DEC_BATCH, DEC_SEQ, D_MODEL))
    cache_a_k = nrm((N_EVEN, DEC_BATCH, PAST_LEN, A_HEADS, A_HEAD_DIM))
    cache_a_v = nrm((N_EVEN, DEC_BATCH, PAST_LEN, A_HEADS, A_HEAD_DIM))
    cache_a_kidx = nrm((N_EVEN, DEC_BATCH, PAST_LEN, IDX_DIM))
    state_b_conv = nrm((N_EVEN, DEC_BATCH, CONV_W - 1, B_CONV_DIM))
    state_b_ssm = nrm((N_EVEN, DEC_BATCH, B_HEADS, B_HEAD_DIM, B_STATE), 0.1)
    cache_c_latent = nrm((N_ODD, DEC_BATCH, PAST_LEN, KV_LORA))
    cache_c_krope = nrm((N_ODD, DEC_BATCH, PAST_LEN, QK_ROPE))
    t5_bias = nrm((N_BUCKETS, A_HEADS), 0.5)
    w_in0 = nrm((N_EVEN, D_MODEL, W_IN0), D_MODEL ** -0.5)
    w_out0 = nrm((N_EVEN, A_WIDTH + B_WIDTH, D_MODEL), BETA * (A_WIDTH + B_WIDTH) ** -0.5)
    conv_w = nrm((N_EVEN, CONV_W, B_CONV_DIM), CONV_W ** -0.5)
    conv_b = nrm((N_EVEN, B_CONV_DIM), 0.02)
    u = jax.random.uniform(next(ks), (N_EVEN, B_HEADS), F32)
    dt0 = jnp.exp(u * (math.log(DT_MAX) - math.log(DT_MIN)) + math.log(DT_MIN))
    dt_bias = dt0 + jnp.log(-jnp.expm1(-dt0))
    a_log = jnp.log(jax.random.uniform(next(ks), (N_EVEN, B_HEADS), F32, 1.0, 16.0))
    d_skip = 1.0 + nrm((N_EVEN, B_HEADS), 0.1)
    ssm_norm_w = 1.0 + nrm((N_EVEN, B_WIDTH), 0.1)
    ln0_g = 1.0 + nrm((N_EVEN, D_MODEL), 0.1)
    ln0_b = nrm((N_EVEN, D_MODEL), 0.02)
    w_in1 = nrm((N_ODD, D_MODEL, W_IN1), D_MODEL ** -0.5)
    q_norm_w = 1.0 + nrm((N_ODD, Q_LORA), 0.1)
    kv_norm_w = 1.0 + nrm((N_ODD, KV_LORA), 0.1)
    w_uq = nrm((N_ODD, Q_LORA, C_HEADS * (QK_NOPE + QK_ROPE)), Q_LORA ** -0.5)
    w_ukv = nrm((N_ODD, KV_LORA, C_HEADS * (QK_NOPE + V_DIM)), KV_LORA ** -0.5)
    w_out1 = nrm((N_ODD, C_WIDTH, D_MODEL), BETA * C_WIDTH ** -0.5)
    ln1_g = 1.0 + nrm((N_ODD, D_MODEL), 0.1)
    ln1_b = nrm((N_ODD, D_MODEL), 0.02)
    return {'x_prompt': x_prompt, 'x_sample': x_sample,
            'cache_a_k': cache_a_k, 'cache_a_v': cache_a_v, 'cache_a_kidx': cache_a_kidx,
            'state_b_conv': state_b_conv, 'state_b_ssm': state_b_ssm,
            'cache_c_latent': cache_c_latent, 'cache_c_krope': cache_c_krope,
            't5_bias': t5_bias, 'w_in0': w_in0, 'w_out0': w_out0, 'conv_w': conv_w, 'conv_b': conv_b,
            'dt_bias': dt_bias, 'a_log': a_log, 'd_skip': d_skip, 'ssm_norm_w': ssm_norm_w,
            'ln0_g': ln0_g, 'ln0_b': ln0_b, 'w_in1': w_in1, 'q_norm_w': q_norm_w, 'kv_norm_w': kv_norm_w,
            'w_uq': w_uq, 'w_ukv': w_ukv, 'w_out1': w_out1, 'ln1_g': ln1_g, 'ln1_b': ln1_b}


def reference(x_prompt, x_sample, cache_a_k, cache_a_v, cache_a_kidx, state_b_conv, state_b_ssm,
              cache_c_latent, cache_c_krope, t5_bias, w_in0, w_out0, conv_w, conv_b, dt_bias, a_log,
              d_skip, ssm_norm_w, ln0_g, ln0_b, w_in1, q_norm_w, kv_norm_w, w_uq, w_ukv, w_out1,
              ln1_g, ln1_b):
    yp, ys = x_prompt, x_sample
    even_p, even_s, odd_p, odd_s = [], [], [], []
    for layer in range(DEPTH):
        if layer % 2 == 0:
            e = layer // 2
            prm = (w_in0[e], w_out0[e], conv_w[e], conv_b[e], dt_bias[e], a_log[e], d_skip[e],
                   ssm_norm_w[e], ln0_g[e], ln0_b[e], t5_bias)
            yp, st_p = even_layer(yp, None, *prm)
            past = (cache_a_k[e], cache_a_v[e], cache_a_kidx[e], state_b_conv[e], state_b_ssm[e])
            ys, st_s = even_layer(ys, past, *prm)
            even_p.append(st_p)
            even_s.append(st_s)
        else:
            o = layer // 2
            prm = (w_in1[o], q_norm_w[o], kv_norm_w[o], w_uq[o], w_ukv[o], w_out1[o], ln1_g[o], ln1_b[o])
            yp, st_p = odd_layer(yp, None, *prm)
            ys, st_s = odd_layer(ys, (cache_c_latent[o], cache_c_krope[o]), *prm)
            odd_p.append(st_p)
            odd_s.append(st_s)
    a_k_p, a_v_p, a_ki_p, b_conv_p, b_ssm_p = [jnp.stack(s) for s in zip(*even_p)]
    a_k_s, a_v_s, a_ki_s, b_conv_s, b_ssm_s = [jnp.stack(s) for s in zip(*even_s)]
    c_lat_p, c_kr_p = [jnp.stack(s) for s in zip(*odd_p)]
    c_lat_s, c_kr_s = [jnp.stack(s) for s in zip(*odd_s)]
    return (yp, ys, a_k_p, a_k_s, a_v_p, a_v_s, a_ki_p, a_ki_s, b_conv_p, b_conv_s,
            b_ssm_p, b_ssm_s, c_lat_p, c_lat_s, c_kr_p, c_kr_s)
```

```python
import functools
import math

import jax
import jax.numpy as jnp
from jax import lax
from jax.experimental import pallas as pl
from jax.experimental.pallas import tpu as pltpu

F32 = jnp.float32
BF16 = jnp.bfloat16
I32 = jnp.int32

D_MODEL = 2048
CHUNK = 64
A_HEADS = 8
A_HEAD_DIM = 128
A_WIDTH = A_HEADS * A_HEAD_DIM
IDX_HEADS = 16
IDX_DIM = 64
IDX_W_SCALE = (IDX_HEADS * IDX_DIM) ** -0.5
TOPK_MAX = 256
N_BUCKETS = 32
MAX_DISTANCE = 128
B_HEAD_DIM = 64
B_WIDTH = D_MODEL
B_HEADS = B_WIDTH // B_HEAD_DIM
B_GROUPS = 4
B_HPG = B_HEADS // B_GROUPS
B_STATE = 128
CONV_W = 4
B_CONV_DIM = B_WIDTH + 2 * B_GROUPS * B_STATE
C_HEADS = 16
Q_LORA = 512
KV_LORA = 512
QK_NOPE = 128
QK_ROPE = 64
V_DIM = 128
C_WIDTH = C_HEADS * V_DIM
ROPE_THETA = 10000.0
MLA_SCALE = (QK_NOPE + QK_ROPE) ** -0.5
DEPTH = 2
ALPHA = (2 * DEPTH) ** 0.25
EPS = 1e-5

LANES = 128
VMEM_LIMIT = 56 * 1024 * 1024
INT_MIN = -(2 ** 31)
INT_MAX = 2 ** 31 - 1
KEY_NEG_INF = (0xFF800000 ^ 0x7FFFFFFF) - (1 << 32)
NEG_BIG = -1e30


def _params(sem):
    return pltpu.CompilerParams(dimension_semantics=sem, vmem_limit_bytes=VMEM_LIMIT)


def _dot_nt(a, b):
    return lax.dot_general(a, b, (((1,), (1,)), ((), ())), preferred_element_type=F32)


def _silu(x):
    return x * (1.0 / (1.0 + jnp.exp(-x)))


def _mm_body(a_ref, w_ref, *o_refs):
    acc = jnp.dot(a_ref[...], w_ref[...], preferred_element_type=F32)
    for o in o_refs:
        o[...] = acc.astype(o.dtype)


def _mm_rms_body(a_ref, w_ref, g_ref, *o_refs):
    acc = jnp.dot(a_ref[...], w_ref[...], preferred_element_type=F32)
    y = acc * lax.rsqrt(jnp.mean(acc * acc, axis=-1, keepdims=True) + EPS) * g_ref[...]
    for o in o_refs:
        o[...] = y.astype(o.dtype)


def _rope_half(t):
    return t + pltpu.roll(t, QK_ROPE, 1)


def _mm_rope_k_body(a_ref, w_ref, tab_ref, o32_ref, o16_ref):
    acc = jnp.dot(a_ref[...], w_ref[...], preferred_element_type=F32)
    r = _rope_half(acc * tab_ref[...])
    lane = lax.broadcasted_iota(I32, r.shape, 1)
    o32_ref[...] = r[:, :QK_ROPE]
    o16_ref[...] = jnp.where(lane < QK_ROPE, r, 0.0).astype(o16_ref.dtype)


def _mm_rope_q_body(a_ref, w_ref, tab_ref, o_ref, *, heads):
    acc = jnp.dot(a_ref[...], w_ref[...], preferred_element_type=F32)
    tab = tab_ref[...]
    lane = lax.broadcasted_iota(I32, tab.shape, 1)
    for h in range(heads):
        base = h * 2 * LANES
        o_ref[:, base:base + LANES] = acc[:, base:base + LANES].astype(o_ref.dtype)
        r = _rope_half(acc[:, base + LANES:base + 2 * LANES] * tab)
        o_ref[:, base + LANES:base + 2 * LANES] = jnp.where(lane < QK_ROPE, r, 0.0).astype(o_ref.dtype)


def _mm_call(body, a, w, extra, extra_specs, out_cols, out_dtypes, tm, tn):
    m, k = a.shape
    n = w.shape[1]
    tm = math.gcd(tm, m)
    tn = min(tn, n)
    assert tm % 8 == 0 and n % tn == 0, (m, n, tm, tn)
    oc =[tn if c is None else c for c in out_cols]
    return pl.pallas_call(
        body,
        grid=(m // tm, n // tn),
        in_specs=[pl.BlockSpec((tm, k), lambda i, j: (i, 0)),
                  pl.BlockSpec((k, tn), lambda i, j: (0, j))] + extra_specs(tm, tn),
        out_specs=[pl.BlockSpec((tm, c), lambda i, j: (i, j)) for c in oc],
        out_shape=[jax.ShapeDtypeStruct((m, (n // tn) * c), d) for c, d in zip(oc, out_dtypes)],
        compiler_params=_params(("parallel", "parallel")),
    )(a, w, *extra)


def _mm(a, w, out_dtypes, tm=1024, tn=512):
    return _mm_call(_mm_body, a, w, [], lambda tm_, tn_: [], [None] * len(out_dtypes), out_dtypes, tm, tn)


def _mm_rms(a, w, g, out_dtypes, tm=1024):
    n = w.shape[1]
    return _mm_call(_mm_rms_body, a, w, [g.reshape(1, n)],
                    lambda tm_, tn_: [pl.BlockSpec((1, n), lambda i, j: (0, 0))],
                    [None] * len(out_dtypes), out_dtypes, tm, n)


def _tab_spec(t_rows):
    def spec(tm, tn):
        nt = t_rows // tm
        return [pl.BlockSpec((tm, LANES), lambda i, j: (i % nt, 0))]
    return spec


def _mm_rope_k(a, w, tab, tm=1024):
    tm = min(tm, tab.shape[0])
    return _mm_call(_mm_rope_k_body, a, w, [tab], _tab_spec(tab.shape[0]), [QK_ROPE, LANES], [F32, BF16], tm, LANES)


def _mm_rope_q(a, w, tab, tm=1024, heads_per_block=2):
    tm = min(tm, tab.shape[0])
    body = functools.partial(_mm_rope_q_body, heads=heads_per_block)
    return _mm_call(body, a, w, [tab], _tab_spec(tab.shape[0]), [None], [BF16], tm, heads_per_block * 2 * LANES)[0]


def _oproj_body(*refs, n_parts):
    parts = refs[:n_parts]
    ws = refs[n_parts:2 * n_parts]
    x_ref, g_ref, b_ref, o32_ref, o16_ref = refs[2 * n_parts:]
    acc = ALPHA * x_ref[...]
    for p, w in zip(parts, ws):
        acc = acc + jnp.dot(p[...], w[...], preferred_element_type=F32)
    mu = jnp.mean(acc, axis=-1, keepdims=True)
    d = acc - mu
    var = jnp.mean(d * d, axis=-1, keepdims=True)
    y = d * lax.rsqrt(var + EPS) * g_ref[...] + b_ref[...]
    o32_ref[...] = y
    o16_ref[...] = y.astype(o16_ref.dtype)


def _oproj_ln(parts, ws, x, g, b, tm=256):
    m, n = x.shape
    tm = min(tm, m)
    np_ = len(parts)
    const = lambda i: (0, 0)
    return pl.pallas_call(
        functools.partial(_oproj_body, n_parts=np_),
        grid=(m // tm,),
        in_specs=[pl.BlockSpec((tm, p.shape[1]), lambda i: (i, 0)) for p in parts]
        + [pl.BlockSpec(w.shape, const, pipeline_mode=pl.Buffered(1)) for w in ws]
        + [pl.BlockSpec((tm, n), lambda i: (i, 0)),
           pl.BlockSpec((1, n), const), pl.BlockSpec((1, n), const)],
        out_specs=[pl.BlockSpec((tm, n), lambda i: (i, 0))] * 2,
        out_shape=[jax.ShapeDtypeStruct((m, n), F32), jax.ShapeDtypeStruct((m, n), BF16)],
        compiler_params=_params(("parallel",)),
    )(*parts, *ws, x, g.reshape(1, n), b.reshape(1, n))


def _t5_bucket(rel):
    half = N_BUCKETS // 2
    max_exact = half // 2
    ret = jnp.where(rel < 0, half, 0)
    n = jnp.abs(rel)
    nf = jnp.maximum(n, 1).astype(F32)
    large = max_exact + (jnp.log(nf / max_exact) / math.log(MAX_DISTANCE / max_exact) * (half - max_exact)).astype(jnp.int32)
    large = jnp.minimum(large, half - 1)
    return ret + jnp.where(n < max_exact, n, large)


def _num_special_tiles(tb):
    return (MAX_DISTANCE - 2 + 2 * tb) // tb


def _bias_tables(t5_bias, tb):
    ns = _num_special_tiles(tb)
    i = jnp.arange(tb, dtype=I32)[:, None]
    j = jnp.arange(tb, dtype=I32)[None, :]
    tabs = [jnp.moveaxis(t5_bias[_t5_bucket(tb * d + i - j)], -1, 0) for d in range(ns + 1)]
    return jnp.stack(tabs).astype(F32)


def _dsa_body(iq_ref, iw_ref, ik_ref, q_ref, k_ref, v_ref, g_ref, bias_ref, o_ref,
              key_scr, mb_scr, wb_scr, x_scr, m_scr, l_scr, acc_scr, *, tb, qt0, ns, n_sel, idx_bits):
    qt = pl.program_id(1) + qt0
    nkv = qt + 1
    lw = min(tb, LANES)
    row_chunk = lax.broadcasted_iota(I32, (tb, tb), 0) // CHUNK
    col = lax.broadcasted_iota(I32, (tb, tb), 1)
    diag_vis = (col // CHUNK) <= row_chunk

    w = iw_ref[0] * IDX_W_SCALE
    for h in range(IDX_HEADS):
        wb_scr[h] = jnp.broadcast_to(w[:, h:h + 1], (tb, lw))

    def idx_tile(j, carry):
        kt = ik_ref[0, pl.ds(pl.multiple_of(j * tb, tb), tb), :]
        acc = jnp.zeros((tb, tb), F32)
        for h in range(IDX_HEADS):
            d = _dot_nt(iq_ref[0, h], kt)
            wbh = wb_scr[h]
            wfull = jnp.concatenate([wbh] * (tb // lw), axis=1) if tb > lw else wbh
            acc = acc + jnp.maximum(d, 0.0) * wfull
        acc = jnp.where(acc == 0.0, 0.0, acc)
        s = jnp.where(jnp.logical_or(diag_vis, j < qt), acc, -jnp.inf)
        bits = pltpu.bitcast(s, I32)
        key_scr[j] = bits ^ ((bits >> 31) & 0x7FFFFFFF)
        return carry

    lax.fori_loop(0, nkv, idx_tile, 0)

    def count(pred):
        def body(j, c):
            f = jnp.where(pred(key_scr[j], j), 1.0, 0.0)
            for t in range(tb // lw):
                c = c + f[:, t * lw:(t + 1) * lw]
            return c
        c = lax.fori_loop(0, nkv, body, jnp.zeros((tb, lw), F32))
        return jnp.sum(c, axis=-1, keepdims=True)

    nsel = float(n_sel)
    thr = jnp.where(count(lambda kt, j: kt >= 0) >= nsel, 0, INT_MIN).astype(I32)

    def bit_body(i, thr):
        cand = thr + jnp.left_shift(jnp.int32(1), 30 - i)
        return jnp.where(count(lambda kt, j: kt >= cand) >= nsel, cand, thr)

    thr = lax.fori_loop(0, 31, bit_body, thr)

    need = nsel - count(lambda kt, j: kt > thr)
    excess = jnp.logical_and(count(lambda kt, j: kt == thr) > need, thr > KEY_NEG_INF)
    x_scr[...] = jnp.full((tb, 1), INT_MAX, I32)

    @pl.when(jnp.max(jnp.where(excess, 1.0, 0.0)) > 0.0)
    def _():
        def xbit(i, x):
            cand = x + jnp.left_shift(jnp.int32(1), idx_bits - 1 - i)
            c = count(lambda kt, j: jnp.logical_and(kt == thr, col + j * tb < cand))
            return jnp.where(c < need, cand, x)
        x = lax.fori_loop(0, idx_bits, xbit, jnp.zeros((tb, 1), I32))
        x_scr[...] = jnp.where(excess, x, INT_MAX)

    xcut = x_scr[...]

    def mask_tile(j, carry):
        kt = key_scr[j]
        tie = jnp.logical_and(kt == thr, col + j * tb <= xcut)
        sel = jnp.logical_and(jnp.logical_or(kt > thr, tie), kt != KEY_NEG_INF)
        mb_scr[j] = jnp.where(sel, 0.0, -jnp.inf)
        return carry

    lax.fori_loop(0, nkv, mask_tile, 0)

    scale = A_HEAD_DIM ** -0.5
    for h in range(A_HEADS):
        hs = slice(h * A_HEAD_DIM, (h + 1) * A_HEAD_DIM)
        qh = q_ref[0, :, hs]
        m_scr[...] = jnp.full((tb, 1), NEG_BIG, F32)
        l_scr[...] = jnp.zeros((tb, 1), F32)
        acc_scr[...] = jnp.zeros((tb, A_HEAD_DIM), F32)

        def kv_tile(j, carry):
            rows = pl.ds(pl.multiple_of(j * tb, tb), tb)
            s = _dot_nt(qh, k_ref[0, rows, hs]) * scale
            s = s + bias_ref[jnp.minimum(qt - j, ns), h] + mb_scr[j]
            m_prev = m_scr[...]
            m_new = jnp.maximum(m_prev, jnp.max(s, axis=-1, keepdims=True))
            alpha = jnp.exp(m_prev - m_new)
            p = jnp.exp(s - m_new)
            l_scr[...] = alpha * l_scr[...] + jnp.sum(p, axis=-1, keepdims=True)
            acc_scr[...] = alpha * acc_scr[...] + jnp.dot(p.astype(BF16), v_ref[0, rows, hs],
                                                          preferred_element_type=F32)
            m_scr[...] = m_new
            return carry

        lax.fori_loop(0, nkv, kv_tile, 0)
        o = acc_scr[...] / l_scr[...]
        o_ref[0, :, hs] = (o * _silu(g_ref[0, :, hs])).astype(o_ref.dtype)


def _dsa(iq, iw, ik, q, k, v, gate, bias_tabs, tb, past_len):
    b, t, aw = q.shape
    l = k.shape[1]
    assert t % tb == 0 and l % tb == 0 and past_len % tb == 0 and l == past_len + t
    nq, nkv = t // tb, l // tb
    ns = _num_special_tiles(tb)
    n_sel = min(TOPK_MAX, l // 4)
    body = functools.partial(_dsa_body, tb=tb, qt0=past_len // tb, ns=ns, n_sel=n_sel,
                             idx_bits=max(1, (l - 1).bit_length()))
    once = pl.Buffered(1)
    return pl.pallas_call(
        body,
        grid=(b, nq),
        in_specs=[
            pl.BlockSpec((1, IDX_HEADS, tb, IDX_DIM), lambda bi, qi: (bi, 0, qi, 0)),
            pl.BlockSpec((1, tb, IDX_HEADS), lambda bi, qi: (bi, qi, 0)),
            pl.BlockSpec((1, l, IDX_DIM), lambda bi, qi: (bi, 0, 0), pipeline_mode=once),
            pl.BlockSpec((1, tb, aw), lambda bi, qi: (bi, qi, 0)),
            pl.BlockSpec((1, l, aw), lambda bi, qi: (bi, 0, 0), pipeline_mode=once),
            pl.BlockSpec((1, l, aw), lambda bi, qi: (bi, 0, 0), pipeline_mode=once),
            pl.BlockSpec((1, tb, aw), lambda bi, qi: (bi, qi, 0)),
            pl.BlockSpec(bias_tabs.shape, lambda bi, qi: (0, 0, 0, 0), pipeline_mode=once),
        ],
        out_specs=pl.BlockSpec((1, tb, aw), lambda bi, qi: (bi, qi, 0)),
        out_shape=jax.ShapeDtypeStruct((b, t, aw), BF16),
        scratch_shapes=[
            pltpu.VMEM((nkv, tb, tb), I32),
            pltpu.VMEM((nkv, tb, tb), F32),
            pltpu.VMEM((IDX_HEADS, tb, min(tb, LANES)), F32),
            pltpu.VMEM((tb, 1), I32),
            pltpu.VMEM((tb, 1), F32),
            pltpu.VMEM((tb, 1), F32),
            pltpu.VMEM((tb, A_HEAD_DIM), F32),
        ],
        compiler_params=_params(("parallel", "parallel")),
    )(iq, iw, ik, q, k, v, gate, bias_tabs)


def _ssd_body(z_ref, xbc_ref, dt_ref, dtt_ref, cw_ref, cb_ref, dtb_ref, dtbt_ref, alog_ref, alogt_ref,
              dsk_ref, nw_ref, exp_ref, c0_ref, h0_ref, y_ref, hout_ref, xpad_scr, h_scr, yi_scr):
    c = pl.program_id(1)
    l = CHUNK
    hi = lax.Precision.HIGHEST
    gw = B_WIDTH // B_GROUPS

    @pl.when(c == 0)
    def _():
        xpad_scr[0:8, :] = c0_ref[0]
        h_scr[...] = h0_ref[0]

    xpad_scr[8:8 + l, :] = xbc_ref[0]
    conv = cb_ref[...]
    for j in range(CONV_W):
        conv = conv + xpad_scr[8 - (CONV_W - 1) + j:8 - (CONV_W - 1) + j + l, :] * cw_ref[j:j + 1, :]
    xpad_scr[0:8, :] = xpad_scr[l:l + 8, :]
    xbc = _silu(conv)
    xs = xbc[:, :B_WIDTH]

    dt = jax.nn.softplus(dt_ref[0, 0] + dtb_ref[...])
    dtt = jax.nn.softplus(dtt_ref[0, 0] + dtbt_ref[...])
    a = -jnp.exp(alog_ref[...])
    at = -jnp.exp(alogt_ref[...])
    ti = lax.broadcasted_iota(I32, (l, l), 0)
    si = lax.broadcasted_iota(I32, (l, l), 1)
    causal = si <= ti
    acum = jnp.dot(jnp.where(causal, 1.0, 0.0), dt * a, precision=hi, preferred_element_type=F32)
    acumt = jnp.dot(dtt * at, jnp.where(ti <= si, 1.0, 0.0), precision=hi, preferred_element_type=F32)
    a_last = acum[l - 1:l, :]
    expand = exp_ref[...]
    e_full = jnp.dot(jnp.exp(acum), expand, precision=hi, preferred_element_type=F32)
    tail_full = jnp.dot(jnp.exp(a_last - acum) * dt, expand, precision=hi, preferred_element_type=F32)
    xt = (xs * tail_full).astype(BF16)
    xs16 = xs.astype(BF16)
    lane = lax.broadcasted_iota(I32, (l, 2 * B_HEAD_DIM), 1)

    for g in range(B_GROUPS):
        bm = xbc[:, B_WIDTH + g * B_STATE:B_WIDTH + (g + 1) * B_STATE].astype(BF16)
        cm = xbc[:, B_WIDTH + (B_GROUPS + g) * B_STATE:B_WIDTH + (B_GROUPS + g + 1) * B_STATE].astype(BF16)
        cb = _dot_nt(cm, bm)
        gs = slice(g * gw, (g + 1) * gw)
        hg = h_scr[g]
        y_state = jnp.dot(cm, hg.astype(BF16), preferred_element_type=F32) * e_full[:, gs]
        for pr in range(B_HPG // 2):
            ws = []
            for r in (g * B_HPG + 2 * pr, g * B_HPG + 2 * pr + 1):
                seg = acum[:, r:r + 1] - acumt[r:r + 1, :]
                decay = jnp.exp(jnp.where(causal, seg, -jnp.inf))
                ws.append((cb * decay * dtt[r:r + 1, :]).astype(BF16))
            c0 = g * gw + pr * 2 * B_HEAD_DIM
            xp = xs16[:, c0:c0 + 2 * B_HEAD_DIM]
            y0 = jnp.dot(ws[0], xp, preferred_element_type=F32)
            y1 = jnp.dot(ws[1], xp, preferred_element_type=F32)
            yi_scr[:, c0:c0 + 2 * B_HEAD_DIM] = jnp.where(lane < B_HEAD_DIM, y0, y1)
        yi_scr[:, gs] = yi_scr[:, gs] + y_state
        upd = lax.dot_general(bm, xt[:, gs], (((0,), (0,)), ((), ())), preferred_element_type=F32)
        h_scr[g] = hg * e_full[l - 1:l, gs] + upd

    y = (yi_scr[...] + dsk_ref[...] * xs) * _silu(z_ref[0])
    for g in range(B_GROUPS):
        gs = slice(g * gw, (g + 1) * gw)
        yg = y[:, gs]
        yg = yg * lax.rsqrt(jnp.mean(yg * yg, axis=-1, keepdims=True) + EPS)
        y_ref[0, :, gs] = (yg * nw_ref[:, gs]).astype(y_ref.dtype)

    @pl.when(c == pl.num_programs(1) - 1)
    def _():
        hout_ref[0] = h_scr[...]


def _ssd(z, xbc, dt_raw, conv_w, conv_b, dt_bias, a_log, d_skip, norm_w, conv0, h0):
    b, t, _ = z.shape
    nc = t // CHUNK
    gw = B_WIDTH // B_GROUPS
    dt4 = dt_raw.reshape(b, nc, CHUNK, B_HEADS)
    dtt4 = jnp.swapaxes(dt4, 2, 3)
    c0 = jnp.pad(conv0, ((0, 0), (8 - (CONV_W - 1), 0), (0, 0)))
    h0t = jnp.transpose(h0.reshape(b, B_GROUPS, B_HPG, B_HEAD_DIM, B_STATE), (0, 1, 4, 2, 3)).reshape(b, B_GROUPS, B_STATE, gw)
    expand = jnp.repeat(jnp.eye(B_HEADS, dtype=F32), B_HEAD_DIM, axis=1)
    dsk = jnp.repeat(d_skip, B_HEAD_DIM).reshape(1, B_WIDTH)
    row = lambda v: v.reshape(1, -1)
    colv = lambda v: v.reshape(-1, 1)
    const2 = lambda bi, ci: (0, 0)
    y, hout = pl.pallas_call(
        _ssd_body,
        grid=(b, nc),
        in_specs=[
            pl.BlockSpec((1, CHUNK, B_WIDTH), lambda bi, ci: (bi, ci, 0)),
            pl.BlockSpec((1, CHUNK, B_CONV_DIM), lambda bi, ci: (bi, ci, 0)),
            pl.BlockSpec((1, 1, CHUNK, B_HEADS), lambda bi, ci: (bi, ci, 0, 0)),
            pl.BlockSpec((1, 1, B_HEADS, CHUNK), lambda bi, ci: (bi, ci, 0, 0)),
            pl.BlockSpec((CONV_W, B_CONV_DIM), const2),
            pl.BlockSpec((1, B_CONV_DIM), const2),
            pl.BlockSpec((1, B_HEADS), const2),
            pl.BlockSpec((B_HEADS, 1), const2),
            pl.BlockSpec((1, B_HEADS), const2),
            pl.BlockSpec((B_HEADS, 1), const2),
            pl.BlockSpec((1, B_WIDTH), const2),
            pl.BlockSpec((1, B_WIDTH), const2),
            pl.BlockSpec((B_HEADS, B_WIDTH), const2),
            pl.BlockSpec((1, 8, B_CONV_DIM), lambda bi, ci: (bi, 0, 0)),
            pl.BlockSpec((1, B_GROUPS, B_STATE, gw), lambda bi, ci: (bi, 0, 0, 0)),
        ],
        out_specs=[pl.BlockSpec((1, CHUNK, B_WIDTH), lambda bi, ci: (bi, ci, 0)),
                   pl.BlockSpec((1, B_GROUPS, B_STATE, gw), lambda bi, ci: (bi, 0, 0, 0))],
        out_shape=[jax.ShapeDtypeStruct((b, t, B_WIDTH), BF16),
                   jax.ShapeDtypeStruct((b, B_GROUPS, B_STATE, gw), F32)],
        scratch_shapes=[pltpu.VMEM((CHUNK + 8, B_CONV_DIM), F32),
                        pltpu.VMEM((B_GROUPS, B_STATE, gw), F32),
                        pltpu.VMEM((CHUNK, B_WIDTH), F32)],
        compiler_params=_params(("parallel", "arbitrary")),
    )(z, xbc, dt4, dtt4, conv_w, row(conv_b), row(dt_bias), colv(dt_bias), row(a_log), colv(a_log),
      dsk, row(norm_w), expand, c0, h0t)
    hnew = jnp.transpose(hout.reshape(b, B_GROUPS, B_STATE, B_HPG, B_HEAD_DIM), (0, 1, 3, 4, 2))
    return y, hnew.reshape(b, B_HEADS, B_HEAD_DIM, B_STATE)


def _mla_body(q_ref, kn_ref, kr_ref, v_ref, g_ref, o_ref, m_scr, l_scr, acc_scr, *, tb, qt0):
    qt = pl.program_id(2) + qt0
    q = q_ref[0]
    m_scr[...] = jnp.full((tb, 1), NEG_BIG, F32)
    l_scr[...] = jnp.zeros((tb, 1), F32)
    acc_scr[...] = jnp.zeros((tb, V_DIM), F32)
    row_chunk = lax.broadcasted_iota(I32, (tb, tb), 0) // CHUNK
    col_chunk = lax.broadcasted_iota(I32, (tb, tb), 1) // CHUNK
    diag_vis = col_chunk <= row_chunk

    def tile(j, masked):
        rows = pl.ds(pl.multiple_of(j * tb, tb), tb)
        kc = jnp.concatenate([kn_ref[0, rows, :], kr_ref[0, rows, :]], axis=1)
        s = _dot_nt(q, kc) * MLA_SCALE
        if masked:
            s = jnp.where(diag_vis, s, -jnp.inf)
        m_prev = m_scr[...]
        m_new = jnp.maximum(m_prev, jnp.max(s, axis=-1, keepdims=True))
        alpha = jnp.exp(m_prev - m_new)
        p = jnp.exp(s - m_new)
        l_scr[...] = alpha * l_scr[...] + jnp.sum(p, axis=-1, keepdims=True)
        acc_scr[...] = alpha * acc_scr[...] + jnp.dot(p.astype(BF16), v_ref[0, rows, :], preferred_element_type=F32)
        m_scr[...] = m_new

    def full_tile(j, carry):
        tile(j, False)
        return carry

    lax.fori_loop(0, qt, full_tile, 0)
    tile(qt, True)
    o = acc_scr[...] / l_scr[...]
    o_ref[0] = (o * _silu(g_ref[0])).astype(o_ref.dtype)


def _mla(q, kn, kr, v, gate, tb, past_len):
    b, t, _ = q.shape
    l = kn.shape[1]
    assert t % tb == 0 and past_len % tb == 0 and l == past_len + t
    body = functools.partial(_mla_body, tb=tb, qt0=past_len // tb)
    return pl.pallas_call(
        body,
        grid=(b, C_HEADS, t // tb),
        in_specs=[
            pl.BlockSpec((1, tb, 2 * LANES), lambda bi, h, qi: (bi, qi, h)),
            pl.BlockSpec((1, l, QK_NOPE), lambda bi, h, qi: (bi, 0, h)),
            pl.BlockSpec((1, l, LANES), lambda bi, h, qi: (bi, 0, 0)),
            pl.BlockSpec((1, l, V_DIM), lambda bi, h, qi: (bi, 0, h)),
            pl.BlockSpec((1, tb, V_DIM), lambda bi, h, qi: (bi, qi, h)),
        ],
        out_specs=pl.BlockSpec((1, tb, V_DIM), lambda bi, h, qi: (bi, qi, h)),
        out_shape=jax.ShapeDtypeStruct((b, t, C_WIDTH), BF16),
        scratch_shapes=[pltpu.VMEM((tb, 1), F32), pltpu.VMEM((tb, 1), F32), pltpu.VMEM((tb, V_DIM), F32)],
        compiler_params=_params(("parallel", "parallel", "parallel")),
    )(q, kn, kr, v, gate)


def _even_weights(w_in):
    offs = [0]
    for s in (A_WIDTH, A_WIDTH, A_WIDTH, A_WIDTH, IDX_HEADS * IDX_DIM, IDX_DIM, IDX_HEADS, B_WIDTH, B_CONV_DIM, B_HEADS):
        offs.append(offs[-1] + s)
    cols = [w_in[:, offs[i]:offs[i + 1]] for i in range(10)]
    aq, ak, av, ag, iq, ik, iw, bz, bxbc, bdt = cols
    pad = jnp.zeros((w_in.shape[0], LANES - IDX_DIM - IDX_HEADS - B_HEADS), w_in.dtype)
    small = jnp.concatenate([ik, iw, bdt, pad], axis=1)
    return [c.astype(BF16) for c in (aq, ak, av, ag, iq, small, bz, bxbc)]


def _even_layer(x, past, wts, w_out, conv_w, conv_b, dt_bias, a_log, d_skip, norm_w, ln_g, ln_b, t5_bias, tb):
    b, t, _ = x.shape
    m = b * t
    x2 = x.reshape(m, D_MODEL)
    xb = x2.astype(BF16)
    w_aq, w_ak, w_av, w_ag, w_iq, w_small, w_bz, w_bxbc = wts
    (aq,) = _mm(xb, w_aq, [BF16])
    ak, ak16 = _mm(xb, w_ak, [F32, BF16])
    av, av16 = _mm(xb, w_av, [F32, BF16])
    (ag,) = _mm(xb, w_ag, [F32])
    (iq,) = _mm(xb, w_iq, [BF16])
    (small,) = _mm(xb, w_small, [F32])
    (bz,) = _mm(xb, w_bz, [F32])
    (bxbc,) = _mm(xb, w_bxbc, [F32])
    ik = small[:, :IDX_DIM].reshape(b, t, IDX_DIM)
    iw = small[:, IDX_DIM:IDX_DIM + IDX_HEADS].reshape(b, t, IDX_HEADS)
    bdt = small[:, IDX_DIM + IDX_HEADS:IDX_DIM + IDX_HEADS + B_HEADS].reshape(b, t, B_HEADS)
    iqt = jnp.transpose(iq.reshape(b, t, IDX_HEADS, IDX_DIM), (0, 2, 1, 3))
    bxbc3 = bxbc.reshape(b, t, B_CONV_DIM)
    k16 = ak16.reshape(b, t, A_WIDTH)
    v16 = av16.reshape(b, t, A_WIDTH)
    ik16 = ik.astype(BF16)
    if past is None:
        p_len = 0
        conv0 = jnp.zeros((b, CONV_W - 1, B_CONV_DIM), F32)
        h0 = jnp.zeros((b, B_HEADS, B_HEAD_DIM, B_STATE), F32)
        conv_new = bxbc3[:, t - (CONV_W - 1):]
    else:
        pk, pv, pki, pconv, pssm = past
        p_len = pk.shape[1]
        k16 = jnp.concatenate([pk.reshape(b, p_len, A_WIDTH).astype(BF16), k16], axis=1)
        v16 = jnp.concatenate([pv.reshape(b, p_len, A_WIDTH).astype(BF16), v16], axis=1)
        ik16 = jnp.concatenate([pki.astype(BF16), ik16], axis=1)
        conv0, h0 = pconv, pssm
        conv_new = jnp.concatenate([pconv, bxbc3], axis=1)[:, -(CONV_W - 1):]
    a_out = _dsa(iqt, iw, ik16, aq.reshape(b, t, A_WIDTH), k16, v16, ag.reshape(b, t, A_WIDTH),
                 _bias_tables(t5_bias, tb), tb, p_len)
    b_out, ssm_new = _ssd(bz.reshape(b, t, B_WIDTH), bxbc3, bdt, conv_w, conv_b, dt_bias, a_log, d_skip, norm_w,
                          conv0, h0)
    wo = w_out.astype(BF16)
    y, y16 = _oproj_ln([a_out.reshape(m, A_WIDTH), b_out.reshape(m, B_WIDTH)], [wo[:A_WIDTH], wo[A_WIDTH:]],
                       x2, ln_g, ln_b)
    state = (ak.reshape(b, t, A_HEADS, A_HEAD_DIM), av.reshape(b, t, A_HEADS, A_HEAD_DIM), ik, conv_new, ssm_new)
    return y.reshape(b, t, D_MODEL), y16, state


def _rope_rot_cols(w):
    half = QK_ROPE // 2
    return jnp.concatenate([-w[..., half:], w[..., :half]], axis=-1)


def _odd_weights(w_in, w_uq, w_ukv):
    w_cq = w_in[:, :Q_LORA]
    w_ckv = w_in[:, Q_LORA:Q_LORA + KV_LORA]
    w_kr = w_in[:, Q_LORA + KV_LORA:Q_LORA + KV_LORA + QK_ROPE]
    w_gate = w_in[:, Q_LORA + KV_LORA + QK_ROPE:]
    w_kr2 = jnp.concatenate([w_kr, _rope_rot_cols(w_kr)], axis=1)
    uq = w_uq.reshape(Q_LORA, C_HEADS, QK_NOPE + QK_ROPE)
    uq_rope = uq[..., QK_NOPE:]
    uq2 = jnp.concatenate([uq[..., :QK_NOPE], uq_rope, _rope_rot_cols(uq_rope)], axis=-1).reshape(Q_LORA, C_HEADS * 2 * LANES)
    ukv = w_ukv.reshape(KV_LORA, C_HEADS, QK_NOPE + V_DIM)
    w_uk = ukv[..., :QK_NOPE].reshape(KV_LORA, C_HEADS * QK_NOPE)
    w_uv = ukv[..., QK_NOPE:].reshape(KV_LORA, C_HEADS * V_DIM)
    return [c.astype(BF16) for c in (w_cq, w_ckv, w_kr2, w_gate, uq2, w_uk, w_uv)]


def _rope_table(pos):
    half = QK_ROPE // 2
    inv = ROPE_THETA ** (-jnp.arange(half, dtype=F32) / half)
    ang = pos.astype(F32)[:, None] * inv[None, :]
    cos, sin = jnp.cos(ang), jnp.sin(ang)
    return jnp.concatenate([cos, cos, sin, sin], axis=1)


def _odd_layer(x, x16, past, wts, q_norm_w, kv_norm_w, w_out, ln_g, ln_b, tb):
    b, t, _ = x.shape
    m = b * t
    w_cq, w_ckv, w_kr2, w_gate, w_uq2, w_uk, w_uv = wts
    p_len = 0 if past is None else past[0].shape[1]
    tab = _rope_table(p_len + jnp.arange(t, dtype=I32))
    (cq16,) = _mm_rms(x16, w_cq, q_norm_w, [BF16])
    ckv, ckv16 = _mm_rms(x16, w_ckv, kv_norm_w, [F32, BF16])
    kr, kr16 = _mm_rope_k(x16, w_kr2, tab)
    (gate,) = _mm(x16, w_gate, [F32])
    q = _mm_rope_q(cq16, w_uq2, tab)
    lat16 = ckv16.reshape(b, t, KV_LORA)
    kr16 = kr16.reshape(b, t, LANES)
    if past is not None:
        lat16 = jnp.concatenate([past[0].astype(BF16), lat16], axis=1)
        kr_past = jnp.pad(past[1], ((0, 0), (0, 0), (0, LANES - QK_ROPE))).astype(BF16)
        kr16 = jnp.concatenate([kr_past, kr16], axis=1)
    l = p_len + t
    lat2 = lat16.reshape(b * l, KV_LORA)
    (kn,) = _mm(lat2, w_uk, [BF16])
    (v,) = _mm(lat2, w_uv, [BF16])
    o = _mla(q.reshape(b, t, C_HEADS * 2 * LANES), kn.reshape(b, l, C_HEADS * QK_NOPE), kr16,
             v.reshape(b, l, C_WIDTH), gate.reshape(b, t, C_WIDTH), tb, p_len)
    y, _ = _oproj_ln([o.reshape(m, C_WIDTH)], [w_out.astype(BF16)], x.reshape(m, D_MODEL), ln_g, ln_b)
    return y.reshape(b, t, D_MODEL), (ckv.reshape(b, t, KV_LORA), kr.reshape(b, t, QK_ROPE))


def kernel(x_prompt, x_sample, cache_a_k, cache_a_v, cache_a_kidx, state_b_conv, state_b_ssm, cache_c_latent, cache_c_krope, t5_bias, w_in0, w_out0, conv_w, conv_b, dt_bias, a_log, d_skip, ssm_norm_w, ln0_g, ln0_b, w_in1, q_norm_w, kv_norm_w, w_uq, w_ukv, w_out1, ln1_g, ln1_b):
    tb_prompt = 256
    tb_sample = CHUNK
    ew = _even_weights(w_in0[0])
    eprm = (w_out0[0], conv_w[0], conv_b[0], dt_bias[0], a_log[0], d_skip[0], ssm_norm_w[0], ln0_g[0], ln0_b[0], t5_bias)
    yp, yp16, st_p = _even_layer(x_prompt, None, ew, *eprm, tb_prompt)
    past = (cache_a_k[0], cache_a_v[0], cache_a_kidx[0], state_b_conv[0], state_b_ssm[0])
    ys, ys16, st_s = _even_layer(x_sample, past, ew, *eprm, tb_sample)
    ow = _odd_weights(w_in1[0], w_uq[0], w_ukv[0])
    oprm = (q_norm_w[0], kv_norm_w[0], w_out1[0], ln1_g[0], ln1_b[0])
    yp, od_p = _odd_layer(yp, yp16, None, ow, *oprm, tb_prompt)
    ys, od_s = _odd_layer(ys, ys16, (cache_c_latent[0], cache_c_krope[0]), ow, *oprm, tb_sample)
    e = lambda a: a[None]
    return (yp, ys, e(st_p[0]), e(st_s[0]), e(st_p[1]), e(st_s[1]), e(st_p[2]), e(st_s[2]),
            e(st_p[3]), e(st_s[3]), e(st_p[4]), e(st_s[4]), e(od_p[0]), e(od_s[0]), e(od_p[1]), e(od_s[1]))
```

```python
import functools
import math

import jax
import jax.numpy as jnp
from jax import lax
from jax.experimental import pallas as pl
from jax.experimental.pallas import tpu as pltpu

F32 = jnp.float32
BF16 = jnp.bfloat16
I32 = jnp.int32

D_MODEL = 2048
CHUNK = 64
A_HEADS = 8
A_HEAD_DIM = 128
A_WIDTH = A_HEADS * A_HEAD_DIM
IDX_HEADS = 16
IDX_DIM = 64
IDX_W_SCALE = (IDX_HEADS * IDX_DIM) ** -0.5
TOPK_MAX = 256
N_BUCKETS = 32
MAX_DISTANCE = 128
B_HEAD_DIM = 64
B_WIDTH = D_MODEL
B_HEADS = B_WIDTH // B_HEAD_DIM
B_GROUPS = 4
B_HPG = B_HEADS // B_GROUPS
B_STATE = 128
CONV_W = 4
B_CONV_DIM = B_WIDTH + 2 * B_GROUPS * B_STATE
C_HEADS = 16
Q_LORA = 512
KV_LORA = 512
QK_NOPE = 128
QK_ROPE = 64
V_DIM = 128
C_WIDTH = C_HEADS * V_DIM
ROPE_THETA = 10000.0
MLA_SCALE = (QK_NOPE + QK_ROPE) ** -0.5
DEPTH = 2
ALPHA = (2 * DEPTH) ** 0.25
EPS = 1e-5

LANES = 128
SUBLANES = 8
VMEM_LIMIT = 56 * 1024 * 1024
LOG2E = math.log2(math.e)
INT_MIN = -(2 ** 31)
INT_MAX = 2 ** 31 - 1
KEY_NEG_INF = (0xFF800000 ^ 0x7FFFFFFF) - (1 << 32)
NEG_BIG = -1e30


def _params(sem):
    return pltpu.CompilerParams(dimension_semantics=sem, vmem_limit_bytes=VMEM_LIMIT)


def _dot_nt(a, b):
    return lax.dot_general(a, b, (((1,), (1,)), ((), ())), preferred_element_type=F32)


def _silu(x):
    return x * (1.0 / (1.0 + jnp.exp(-x)))


def _mm_body(a_ref, w_ref, *o_refs):
    acc = jnp.dot(a_ref[...], w_ref[...], preferred_element_type=F32)
    for o in o_refs:
        o[...] = acc.astype(o.dtype)


def _mm_rms_body(a_ref, w_ref, g_ref, *o_refs):
    acc = jnp.dot(a_ref[...], w_ref[...], preferred_element_type=F32)
    y = acc * lax.rsqrt(jnp.mean(acc * acc, axis=-1, keepdims=True) + EPS) * g_ref[...]
    for o in o_refs:
        o[...] = y.astype(o.dtype)


def _rope_half(t):
    return t + pltpu.roll(t, QK_ROPE, 1)


def _mm_rope_k_body(a_ref, w_ref, tab_ref, o32_ref, o16_ref):
    acc = jnp.dot(a_ref[...], w_ref[...], preferred_element_type=F32)
    r = _rope_half(acc * tab_ref[...])
    lane = lax.broadcasted_iota(I32, r.shape, 1)
    o32_ref[...] = r[:, :QK_ROPE]
    o16_ref[...] = jnp.where(lane < QK_ROPE, r, 0.0).astype(o16_ref.dtype)


def _mm_rope_q_body(a_ref, w_ref, tab_ref, o_ref, *, heads):
    acc = jnp.dot(a_ref[...], w_ref[...], preferred_element_type=F32)
    tab = tab_ref[...]
    lane = lax.broadcasted_iota(I32, tab.shape, 1)
    for h in range(heads):
        base = h * 2 * LANES
        o_ref[:, base:base + LANES] = acc[:, base:base + LANES].astype(o_ref.dtype)
        r = _rope_half(acc[:, base + LANES:base + 2 * LANES] * tab)
        o_ref[:, base + LANES:base + 2 * LANES] = jnp.where(lane < QK_ROPE, r, 0.0).astype(o_ref.dtype)


def _mm_call(name, body, a, w, extra, extra_specs, out_cols, out_dtypes, tm, tn):
    m, k = a.shape
    n = w.shape[1]
    tm = math.gcd(tm, m)
    tn = min(tn, n)
    assert tm % SUBLANES == 0 and n % tn == 0, (m, n, tm, tn)
    oc = [tn if c is None else c for c in out_cols]
    return pl.pallas_call(
        body,
        grid=(m // tm, n // tn),
        in_specs=[pl.BlockSpec((tm, k), lambda i, j: (i, 0)),
                  pl.BlockSpec((k, tn), lambda i, j: (0, j))] + extra_specs(tm, tn),
        out_specs=[pl.BlockSpec((tm, c), lambda i, j: (i, j)) for c in oc],
        out_shape=[jax.ShapeDtypeStruct((m, (n // tn) * c), d) for c, d in zip(oc, out_dtypes)],
        compiler_params=_params(("parallel", "parallel")),
        name=name,
    )(a, w, *extra)


def _mm(name, a, w, out_dtypes, tm=1024, tn=512):
    return _mm_call(name, _mm_body, a, w, [], lambda tm_, tn_: [], [None] * len(out_dtypes), out_dtypes, tm, tn)


def _mm_rms(name, a, w, g, out_dtypes, tm=1024):
    n = w.shape[1]
    return _mm_call(name, _mm_rms_body, a, w, [g.reshape(1, n)],
                    lambda tm_, tn_: [pl.BlockSpec((1, n), lambda i, j: (0, 0))],
                    [None] * len(out_dtypes), out_dtypes, tm, n)


def _tab_spec(t_rows):
    def spec(tm, tn):
        nt = t_rows // tm
        return [pl.BlockSpec((tm, LANES), lambda i, j: (i % nt, 0))]
    return spec


def _mm_rope_k(name, a, w, tab, tm=1024):
    tm = min(tm, tab.shape[0])
    return _mm_call(name, _mm_rope_k_body, a, w, [tab], _tab_spec(tab.shape[0]), [QK_ROPE, LANES], [F32, BF16],
                    tm, LANES)


def _mm_rope_q(name, a, w, tab, tm=1024, heads_per_block=2):
    tm = min(tm, tab.shape[0])
    body = functools.partial(_mm_rope_q_body, heads=heads_per_block)
    return _mm_call(name, body, a, w, [tab], _tab_spec(tab.shape[0]), [None], [BF16], tm,
                    heads_per_block * 2 * LANES)[0]


def _oproj_body(*refs, n_parts):
    parts = refs[:n_parts]
    ws = refs[n_parts:2 * n_parts]
    x_ref, g_ref, b_ref, o32_ref, o16_ref = refs[2 * n_parts:]
    acc = ALPHA * x_ref[...]
    for p, w in zip(parts, ws):
        acc = acc + jnp.dot(p[...], w[...], preferred_element_type=F32)
    mu = jnp.mean(acc, axis=-1, keepdims=True)
    d = acc - mu
    var = jnp.mean(d * d, axis=-1, keepdims=True)
    y = d * lax.rsqrt(var + EPS) * g_ref[...] + b_ref[...]
    o32_ref[...] = y
    o16_ref[...] = y.astype(o16_ref.dtype)


def _oproj_ln(name, parts, ws, x, g, b, tm=256):
    m, n = x.shape
    tm = min(tm, m)
    np_ = len(parts)
    const = lambda i: (0, 0)
    return pl.pallas_call(
        functools.partial(_oproj_body, n_parts=np_),
        grid=(m // tm,),
        in_specs=[pl.BlockSpec((tm, p.shape[1]), lambda i: (i, 0)) for p in parts]
        + [pl.BlockSpec(w.shape, const, pipeline_mode=pl.Buffered(1)) for w in ws]
        + [pl.BlockSpec((tm, n), lambda i: (i, 0)),
           pl.BlockSpec((1, n), const), pl.BlockSpec((1, n), const)],
        out_specs=[pl.BlockSpec((tm, n), lambda i: (i, 0))] * 2,
        out_shape=[jax.ShapeDtypeStruct((m, n), F32), jax.ShapeDtypeStruct((m, n), BF16)],
        compiler_params=_params(("parallel",)),
        name=name,
    )(*parts, *ws, x, g.reshape(1, n), b.reshape(1, n))


def _to_vt(v, tb):
    b, l, hd = v.shape
    h = hd // LANES
    return jnp.transpose(v.reshape(b, l // tb, tb, h, LANES), (0, 3, 1, 4, 2))


def _softmax_init(m_scr, l_scr, acc_scr, heads, tb):
    for h in range(heads):
        m_scr[h] = jnp.full((1, tb), NEG_BIG, F32)
        l_scr[h] = jnp.zeros((1, tb), F32)
        acc_scr[h] = jnp.zeros(acc_scr.shape[1:], F32)


def _softmax_step(s, vt, m_scr, l_scr, acc_scr, h):
    m_prev = m_scr[h]
    m_new = jnp.maximum(m_prev, jnp.max(s, axis=0, keepdims=True))
    alpha = jnp.exp2(m_prev - m_new)
    p = jnp.exp2(s - m_new)
    l_scr[h] = alpha * l_scr[h] + jnp.sum(p, axis=0, keepdims=True)
    acc_scr[h] = alpha * acc_scr[h] + jnp.dot(vt, p.astype(BF16), preferred_element_type=F32)
    m_scr[h] = m_new


def _softmax_finish(l_scr, acc_scr, g_ref, o_ref, heads):
    for h in range(heads):
        hs = slice(h * LANES, (h + 1) * LANES)
        o = (acc_scr[h] / l_scr[h]).T
        o_ref[0, :, hs] = (o * _silu(g_ref[0, :, hs])).astype(o_ref.dtype)


def _t5_bucket(rel):
    half = N_BUCKETS // 2
    max_exact = half // 2
    ret = jnp.where(rel < 0, half, 0)
    n = jnp.abs(rel)
    nf = jnp.maximum(n, 1).astype(F32)
    large = max_exact + (jnp.log(nf / max_exact) / math.log(MAX_DISTANCE / max_exact) * (half - max_exact)).astype(jnp.int32)
    large = jnp.minimum(large, half - 1)
    return ret + jnp.where(n < max_exact, n, large)


def _num_special_tiles(tb):
    return (MAX_DISTANCE - 2 + 2 * tb) // tb


def _bias_tables(t5_bias, tb):
    ns = _num_special_tiles(tb)
    s = jnp.arange(tb, dtype=I32)[:, None]
    t = jnp.arange(tb, dtype=I32)[None, :]
    bucket = jnp.stack([_t5_bucket(tb * d + t - s) for d in range(ns)])
    far = t5_bias[_t5_bucket(jnp.int32(tb * ns))]
    out = jnp.zeros((ns, A_HEADS, tb, tb), F32)
    for k in range(N_BUCKETS):
        out = jnp.where(bucket[:, None] == k, t5_bias[k][None, :, None, None], out)
    return (out - far[None, :, None, None]) * LOG2E


def _dsa_body(iq_ref, iwt_ref, ik_ref, q_ref, k_ref, vt_ref, g_ref, bias_ref, o_ref,
              key_scr, mb_scr, x_scr, m_scr, l_scr, acc_scr, *, tb, qt0, ns, n_sel, idx_bits):
    qt = pl.program_id(1) + qt0
    nkv = qt + 1
    krow = lax.broadcasted_iota(I32, (tb, tb), 0)
    qcol = lax.broadcasted_iota(I32, (tb, tb), 1)
    diag_vis = (krow // CHUNK) <= (qcol // CHUNK)

    wt = iwt_ref[0] * IDX_W_SCALE

    def idx_tile(j, carry):
        kt = ik_ref[0, pl.ds(pl.multiple_of(j * tb, tb), tb), :]
        acc = jnp.zeros((tb, tb), F32)
        for h in range(IDX_HEADS):
            acc = acc + jnp.maximum(_dot_nt(kt, iq_ref[0, h]), 0.0) * wt[h:h + 1, :]
        acc = jnp.where(acc == 0.0, 0.0, acc)
        s = jnp.where(jnp.logical_or(diag_vis, j < qt), acc, -jnp.inf)
        bits = pltpu.bitcast(s, I32)
        key_scr[j] = bits ^ ((bits >> 31) & 0x7FFFFFFF)
        return carry

    lax.fori_loop(0, nkv, idx_tile, 0)

    def count(pred):
        def body(j, c):
            f = jnp.where(pred(key_scr[j], j), 1.0, 0.0)
            return c + jnp.sum(f.reshape(tb // SUBLANES, SUBLANES, tb), axis=0)
        c = lax.fori_loop(0, nkv, body, jnp.zeros((SUBLANES, tb), F32))
        return jnp.sum(c, axis=0, keepdims=True)

    nsel = float(n_sel)
    thr = jnp.where(count(lambda kt, j: kt >= 0) >= nsel, 0, INT_MIN).astype(I32)

    def bit_body(i, thr):
        cand = thr + jnp.left_shift(jnp.int32(1), 30 - i)
        return jnp.where(count(lambda kt, j: kt >= cand) >= nsel, cand, thr)

    thr = lax.fori_loop(0, 31, bit_body, thr)

    need = nsel - count(lambda kt, j: kt > thr)
    excess = jnp.logical_and(count(lambda kt, j: kt == thr) > need, thr > KEY_NEG_INF)
    x_scr[...] = jnp.full((1, tb), INT_MAX, I32)

    @pl.when(jnp.max(jnp.where(excess, 1.0, 0.0)) > 0.0)
    def _():
        def xbit(i, x):
            cand = x + jnp.left_shift(jnp.int32(1), idx_bits - 1 - i)
            c = count(lambda kt, j: jnp.logical_and(kt == thr, krow + j * tb < cand))
            return jnp.where(c < need, cand, x)
        x = lax.fori_loop(0, idx_bits, xbit, jnp.zeros((1, tb), I32))
        x_scr[...] = jnp.where(excess, x, INT_MAX)

    xcut = x_scr[...]

    def mask_tile(j, carry):
        kt = key_scr[j]
        tie = jnp.logical_and(kt == thr, krow + j * tb <= xcut)
        sel = jnp.logical_and(jnp.logical_or(kt > thr, tie), kt != KEY_NEG_INF)
        mb_scr[j] = jnp.where(sel, 0.0, -jnp.inf)
        return carry

    lax.fori_loop(0, nkv, mask_tile, 0)

    _softmax_init(m_scr, l_scr, acc_scr, A_HEADS, tb)

    def attn_tile(j, biased):
        rows = pl.ds(pl.multiple_of(j * tb, tb), tb)
        mb = mb_scr[j]
        qk = [_dot_nt(k_ref[0, rows, h * A_HEAD_DIM:(h + 1) * A_HEAD_DIM],
                      q_ref[0, :, h * A_HEAD_DIM:(h + 1) * A_HEAD_DIM]) for h in range(A_HEADS)]
        for h in range(A_HEADS):
            s = mb + qk[h]
            if biased:
                s = bias_ref[qt - j, h] + s
            _softmax_step(s, vt_ref[0, h, j], m_scr, l_scr, acc_scr, h)

    def far_tile(j, carry):
        attn_tile(j, False)
        return carry

    def near_tile(j, carry):
        attn_tile(j, True)
        return carry

    n_far = jnp.maximum(qt - (ns - 1), 0)
    lax.fori_loop(0, n_far, far_tile, 0)
    lax.fori_loop(n_far, nkv, near_tile, 0)
    _softmax_finish(l_scr, acc_scr, g_ref, o_ref, A_HEADS)


def _dsa(name, iq, iwt, ik, q, k, vt, gate, bias_tabs, tb, past_len):
    b, t, aw = q.shape
    l = k.shape[1]
    assert t % tb == 0 and l % tb == 0 and past_len % tb == 0 and l == past_len + t
    nq, nkv = t // tb, l // tb
    ns = _num_special_tiles(tb)
    n_sel = min(TOPK_MAX, l // 4)
    body = functools.partial(_dsa_body, tb=tb, qt0=past_len // tb, ns=ns, n_sel=n_sel,
                             idx_bits=max(1, (l - 1).bit_length()))
    once = pl.Buffered(1)
    return pl.pallas_call(
        body,
        grid=(b, nq),
        in_specs=[
            pl.BlockSpec((1, IDX_HEADS, tb, IDX_DIM), lambda bi, qi: (bi, 0, qi, 0)),
            pl.BlockSpec((1, IDX_HEADS, tb), lambda bi, qi: (bi, 0, qi)),
            pl.BlockSpec((1, l, IDX_DIM), lambda bi, qi: (bi, 0, 0), pipeline_mode=once),
            pl.BlockSpec((1, tb, aw), lambda bi, qi: (bi, qi, 0)),
            pl.BlockSpec((1, l, aw), lambda bi, qi: (bi, 0, 0), pipeline_mode=once),
            pl.BlockSpec((1, A_HEADS, nkv, A_HEAD_DIM, tb), lambda bi, qi: (bi, 0, 0, 0, 0), pipeline_mode=once),
            pl.BlockSpec((1, tb, aw), lambda bi, qi: (bi, qi, 0)),
            pl.BlockSpec(bias_tabs.shape, lambda bi, qi: (0, 0, 0, 0), pipeline_mode=once),
        ],
        out_specs=pl.BlockSpec((1, tb, aw), lambda bi, qi: (bi, qi, 0)),
        out_shape=jax.ShapeDtypeStruct((b, t, aw), BF16),
        scratch_shapes=[
            pltpu.VMEM((nkv, tb, tb), I32),
            pltpu.VMEM((nkv, tb, tb), F32),
            pltpu.VMEM((1, tb), I32),
            pltpu.VMEM((A_HEADS, 1, tb), F32),
            pltpu.VMEM((A_HEADS, 1, tb), F32),
            pltpu.VMEM((A_HEADS, A_HEAD_DIM, tb), F32),
        ],
        compiler_params=_params(("parallel", "parallel")),
        name=name,
    )(iq, iwt, ik, q, k, vt, gate, bias_tabs)


def _ssd_body(z_ref, xbc_ref, dt_ref, dtt_ref, cw_ref, cb_ref, dtb_ref, dtbt_ref, alog_ref, alogt_ref,
              dsk_ref, nw_ref, exp_ref, c0_ref, h0_ref, y_ref, hout_ref, xpad_scr, h_scr, yi_scr):
    c = pl.program_id(1)
    l = CHUNK
    hi = lax.Precision.HIGHEST
    gw = B_WIDTH // B_GROUPS

    @pl.when(c == 0)
    def _():
        xpad_scr[0:8, :] = c0_ref[0]
        h_scr[...] = h0_ref[0]

    xpad_scr[8:8 + l, :] = xbc_ref[0]
    conv = cb_ref[...]
    for j in range(CONV_W):
        conv = conv + xpad_scr[8 - (CONV_W - 1) + j:8 - (CONV_W - 1) + j + l, :] * cw_ref[j:j + 1, :]
    xpad_scr[0:8, :] = xpad_scr[l:l + 8, :]
    xbc = _silu(conv)
    xs = xbc[:, :B_WIDTH]

    dt = jax.nn.softplus(dt_ref[0, 0] + dtb_ref[...])
    dtt = jax.nn.softplus(dtt_ref[0, 0] + dtbt_ref[...])
    a = -jnp.exp(alog_ref[...])
    at = -jnp.exp(alogt_ref[...])
    ti = lax.broadcasted_iota(I32, (l, l), 0)
    si = lax.broadcasted_iota(I32, (l, l), 1)
    causal = si <= ti
    acum = jnp.dot(jnp.where(causal, 1.0, 0.0), dt * a, precision=hi, preferred_element_type=F32)
    acumt = jnp.dot(dtt * at, jnp.where(ti <= si, 1.0, 0.0), precision=hi, preferred_element_type=F32)
    a_last = acum[l - 1:l, :]
    expand = exp_ref[...]
    e_full = jnp.dot(jnp.exp(acum), expand, precision=hi, preferred_element_type=F32)
    tail_full = jnp.dot(jnp.exp(a_last - acum) * dt, expand, precision=hi, preferred_element_type=F32)
    xt = (xs * tail_full).astype(BF16)
    xs16 = xs.astype(BF16)
    lane = lax.broadcasted_iota(I32, (l, 2 * B_HEAD_DIM), 1)

    for g in range(B_GROUPS):
        bm = xbc[:, B_WIDTH + g * B_STATE:B_WIDTH + (g + 1) * B_STATE].astype(BF16)
        cm = xbc[:, B_WIDTH + (B_GROUPS + g) * B_STATE:B_WIDTH + (B_GROUPS + g + 1) * B_STATE].astype(BF16)
        cb = _dot_nt(cm, bm)
        gs = slice(g * gw, (g + 1) * gw)
        hg = h_scr[g]
        y_state = jnp.dot(cm, hg.astype(BF16), preferred_element_type=F32) * e_full[:, gs]
        for pr in range(B_HPG // 2):
            ws = []
            for r in (g * B_HPG + 2 * pr, g * B_HPG + 2 * pr + 1):
                seg = acum[:, r:r + 1] - acumt[r:r + 1, :]
                decay = jnp.exp(jnp.where(causal, seg, -jnp.inf))
                ws.append((cb * decay * dtt[r:r + 1, :]).astype(BF16))
            c0 = g * gw + pr * 2 * B_HEAD_DIM
            xp = xs16[:, c0:c0 + 2 * B_HEAD_DIM]
            y0 = jnp.dot(ws[0], xp, preferred_element_type=F32)
            y1 = jnp.dot(ws[1], xp, preferred_element_type=F32)
            yi_scr[:, c0:c0 + 2 * B_HEAD_DIM] = jnp.where(lane < B_HEAD_DIM, y0, y1)
        yi_scr[:, gs] = yi_scr[:, gs] + y_state
        upd = lax.dot_general(bm, xt[:, gs], (((0,), (0,)), ((), ())), preferred_element_type=F32)
        h_scr[g] = hg * e_full[l - 1:l, gs] + upd

    y = (yi_scr[...] + dsk_ref[...] * xs) * _silu(z_ref[0])
    for g in range(B_GROUPS):
        gs = slice(g * gw, (g + 1) * gw)
        yg = y[:, gs]
        yg = yg * lax.rsqrt(jnp.mean(yg * yg, axis=-1, keepdims=True) + EPS)
        y_ref[0, :, gs] = (yg * nw_ref[:, gs]).astype(y_ref.dtype)

    @pl.when(c == pl.num_programs(1) - 1)
    def _():
        hout_ref[0] = h_scr[...]


def _ssd(name, z, xbc, dt_raw, conv_w, conv_b, dt_bias, a_log, d_skip, norm_w, conv0, h0):
    b, t, _ = z.shape
    nc = t // CHUNK
    gw = B_WIDTH // B_GROUPS
    dt4 = dt_raw.reshape(b, nc, CHUNK, B_HEADS)
    dtt4 = jnp.swapaxes(dt4, 2, 3)
    c0 = jnp.pad(conv0, ((0, 0), (8 - (CONV_W - 1), 0), (0, 0)))
    h0t = jnp.transpose(h0.reshape(b, B_GROUPS, B_HPG, B_HEAD_DIM, B_STATE), (0, 1, 4, 2, 3)).reshape(b, B_GROUPS, B_STATE, gw)
    expand = jnp.repeat(jnp.eye(B_HEADS, dtype=F32), B_HEAD_DIM, axis=1)
    dsk = jnp.repeat(d_skip, B_HEAD_DIM).reshape(1, B_WIDTH)
    row = lambda v: v.reshape(1, -1)
    colv = lambda v: v.reshape(-1, 1)
    const2 = lambda bi, ci: (0, 0)
    y, hout = pl.pallas_call(
        _ssd_body,
        grid=(b, nc),
        in_specs=[
            pl.BlockSpec((1, CHUNK, B_WIDTH), lambda bi, ci: (bi, ci, 0)),
            pl.BlockSpec((1, CHUNK, B_CONV_DIM), lambda bi, ci: (bi, ci, 0)),
            pl.BlockSpec((1, 1, CHUNK, B_HEADS), lambda bi, ci: (bi, ci, 0, 0)),
            pl.BlockSpec((1, 1, B_HEADS, CHUNK), lambda bi, ci: (bi, ci, 0, 0)),
            pl.BlockSpec((CONV_W, B_CONV_DIM), const2),
            pl.BlockSpec((1, B_CONV_DIM), const2),
            pl.BlockSpec((1, B_HEADS), const2),
            pl.BlockSpec((B_HEADS, 1), const2),
            pl.BlockSpec((1, B_HEADS), const2),
            pl.BlockSpec((B_HEADS, 1), const2),
            pl.BlockSpec((1, B_WIDTH), const2),
            pl.BlockSpec((1, B_WIDTH), const2),
            pl.BlockSpec((B_HEADS, B_WIDTH), const2),
            pl.BlockSpec((1, 8, B_CONV_DIM), lambda bi, ci: (bi, 0, 0)),
            pl.BlockSpec((1, B_GROUPS, B_STATE, gw), lambda bi, ci: (bi, 0, 0, 0)),
        ],
        out_specs=[pl.BlockSpec((1, CHUNK, B_WIDTH), lambda bi, ci: (bi, ci, 0)),
                   pl.BlockSpec((1, B_GROUPS, B_STATE, gw), lambda bi, ci: (bi, 0, 0, 0))],
        out_shape=[jax.ShapeDtypeStruct((b, t, B_WIDTH), BF16),
                   jax.ShapeDtypeStruct((b, B_GROUPS, B_STATE, gw), F32)],
        scratch_shapes=[pltpu.VMEM((CHUNK + 8, B_CONV_DIM), F32),
                        pltpu.VMEM((B_GROUPS, B_STATE, gw), F32),
                        pltpu.VMEM((CHUNK, B_WIDTH), F32)],
        compiler_params=_params(("parallel", "arbitrary")),
        name=name,
    )(z, xbc, dt4, dtt4, conv_w, row(conv_b), row(dt_bias), colv(dt_bias), row(a_log), colv(a_log),
      dsk, row(norm_w), expand, c0, h0t)
    hnew = jnp.transpose(hout.reshape(b, B_GROUPS, B_STATE, B_HPG, B_HEAD_DIM), (0, 1, 3, 4, 2))
    return y, hnew.reshape(b, B_HEADS, B_HEAD_DIM, B_STATE)


def _mla_body(q_ref, kn_ref, kr_ref, vt_ref, g_ref, o_ref, m_scr, l_scr, acc_scr, *, tb, qt0, heads):
    qt = pl.program_id(2) + qt0
    _softmax_init(m_scr, l_scr, acc_scr, heads, tb)
    krow = lax.broadcasted_iota(I32, (tb, tb), 0)
    qcol = lax.broadcasted_iota(I32, (tb, tb), 1)
    diag_vis = (krow // CHUNK) <= (qcol // CHUNK)

    def tile(j, masked):
        rows = pl.ds(pl.multiple_of(j * tb, tb), tb)
        kr = kr_ref[0, rows, :]
        qk = [_dot_nt(jnp.concatenate([kn_ref[0, rows, h * QK_NOPE:(h + 1) * QK_NOPE], kr], axis=1),
                      q_ref[0, :, h * 2 * LANES:(h + 1) * 2 * LANES]) for h in range(heads)]
        for h in range(heads):
            s = jnp.where(diag_vis, qk[h], -jnp.inf) if masked else qk[h]
            _softmax_step(s, vt_ref[0, h, j], m_scr, l_scr, acc_scr, h)

    def full_tile(j, carry):
        tile(j, False)
        return carry

    lax.fori_loop(0, qt, full_tile, 0)
    tile(qt, True)
    _softmax_finish(l_scr, acc_scr, g_ref, o_ref, heads)


def _mla(name, q, kn, kr, vt, gate, tb, past_len, heads=4):
    b, t, _ = q.shape
    l = kn.shape[1]
    assert t % tb == 0 and past_len % tb == 0 and l == past_len + t and C_HEADS % heads == 0
    nkv = l // tb
    body = functools.partial(_mla_body, tb=tb, qt0=past_len // tb, heads=heads)
    return pl.pallas_call(
        body,
        grid=(b, C_HEADS // heads, t // tb),
        in_specs=[
            pl.BlockSpec((1, tb, heads * 2 * LANES), lambda bi, h, qi: (bi, qi, h)),
            pl.BlockSpec((1, l, heads * QK_NOPE), lambda bi, h, qi: (bi, 0, h)),
            pl.BlockSpec((1, l, LANES), lambda bi, h, qi: (bi, 0, 0)),
            pl.BlockSpec((1, heads, nkv, V_DIM, tb), lambda bi, h, qi: (bi, h, 0, 0, 0)),
            pl.BlockSpec((1, tb, heads * V_DIM), lambda bi, h, qi: (bi, qi, h)),
        ],
        out_specs=pl.BlockSpec((1, tb, heads * V_DIM), lambda bi, h, qi: (bi, qi, h)),
        out_shape=jax.ShapeDtypeStruct((b, t, C_WIDTH), BF16),
        scratch_shapes=[pltpu.VMEM((heads, 1, tb), F32), pltpu.VMEM((heads, 1, tb), F32),
                        pltpu.VMEM((heads, V_DIM, tb), F32)],
        compiler_params=_params(("parallel", "parallel", "parallel")),
        name=name,
    )(q, kn, kr, vt, gate)


def _even_weights(w_in):
    offs = [0]
    for s in (A_WIDTH, A_WIDTH, A_WIDTH, A_WIDTH, IDX_HEADS * IDX_DIM, IDX_DIM, IDX_HEADS, B_WIDTH, B_CONV_DIM, B_HEADS):
        offs.append(offs[-1] + s)
    cols = [w_in[:, offs[i]:offs[i + 1]] for i in range(10)]
    aq, ak, av, ag, iq, ik, iw, bz, bxbc, bdt = cols
    aq = aq * (A_HEAD_DIM ** -0.5 * LOG2E)
    pad = jnp.zeros((w_in.shape[0], LANES - IDX_DIM - IDX_HEADS - B_HEADS), w_in.dtype)
    small = jnp.concatenate([ik, iw, bdt, pad], axis=1)
    return [c.astype(BF16) for c in (aq, ak, av, ag, iq, small, bz, bxbc)]


def _even_layer(tag, x, past, wts, w_out, conv_w, conv_b, dt_bias, a_log, d_skip, norm_w, ln_g, ln_b, t5_bias, tb):
    b, t, _ = x.shape
    m = b * t
    x2 = x.reshape(m, D_MODEL)
    xb = x2.astype(BF16)
    w_aq, w_ak, w_av, w_ag, w_iq, w_small, w_bz, w_bxbc = wts
    (aq,) = _mm(tag + "_in_aq", xb, w_aq, [BF16])
    ak, ak16 = _mm(tag + "_in_ak", xb, w_ak, [F32, BF16])
    av, av16 = _mm(tag + "_in_av", xb, w_av, [F32, BF16])
    (ag,) = _mm(tag + "_in_ag", xb, w_ag, [F32])
    (iq,) = _mm(tag + "_in_iq", xb, w_iq, [BF16])
    (small,) = _mm(tag + "_in_small", xb, w_small, [F32])
    (bz,) = _mm(tag + "_in_bz", xb, w_bz, [F32])
    (bxbc,) = _mm(tag + "_in_bxbc", xb, w_bxbc, [F32])
    ik = small[:, :IDX_DIM].reshape(b, t, IDX_DIM)
    iwt = jnp.swapaxes(small[:, IDX_DIM:IDX_DIM + IDX_HEADS].reshape(b, t, IDX_HEADS), 1, 2)
    bdt = small[:, IDX_DIM + IDX_HEADS:IDX_DIM + IDX_HEADS + B_HEADS].reshape(b, t, B_HEADS)
    iqt = jnp.transpose(iq.reshape(b, t, IDX_HEADS, IDX_DIM), (0, 2, 1, 3))
    bxbc3 = bxbc.reshape(b, t, B_CONV_DIM)
    k16 = ak16.reshape(b, t, A_WIDTH)
    v16 = av16.reshape(b, t, A_WIDTH)
    ik16 = ik.astype(BF16)
    if past is None:
        p_len = 0
        conv0 = jnp.zeros((b, CONV_W - 1, B_CONV_DIM), F32)
        h0 = jnp.zeros((b, B_HEADS, B_HEAD_DIM, B_STATE), F32)
        conv_new = bxbc3[:, t - (CONV_W - 1):]
    else:
        pk, pv, pki, pconv, pssm = past
        p_len = pk.shape[1]
        k16 = jnp.concatenate([pk.reshape(b, p_len, A_WIDTH).astype(BF16), k16], axis=1)
        v16 = jnp.concatenate([pv.reshape(b, p_len, A_WIDTH).astype(BF16), v16], axis=1)
        ik16 = jnp.concatenate([pki.astype(BF16), ik16], axis=1)
        conv0, h0 = pconv, pssm
        conv_new = jnp.concatenate([pconv, bxbc3], axis=1)[:, -(CONV_W - 1):]
    a_out = _dsa(tag + "_dsa", iqt, iwt, ik16, aq.reshape(b, t, A_WIDTH), k16, _to_vt(v16, tb),
                 ag.reshape(b, t, A_WIDTH), _bias_tables(t5_bias, tb), tb, p_len)
    b_out, ssm_new = _ssd(tag + "_ssd", bz.reshape(b, t, B_WIDTH), bxbc3, bdt, conv_w, conv_b, dt_bias, a_log,
                          d_skip, norm_w, conv0, h0)
    wo = w_out.astype(BF16)
    y, y16 = _oproj_ln(tag + "_out0", [a_out.reshape(m, A_WIDTH), b_out.reshape(m, B_WIDTH)],
                       [wo[:A_WIDTH], wo[A_WIDTH:]], x2, ln_g, ln_b)
    state = (ak.reshape(b, t, A_HEADS, A_HEAD_DIM), av.reshape(b, t, A_HEADS, A_HEAD_DIM), ik, conv_new, ssm_new)
    return y.reshape(b, t, D_MODEL), y16, state


def _rope_rot_cols(w):
    half = QK_ROPE // 2
    return jnp.concatenate([-w[..., half:], w[..., :half]], axis=-1)


def _odd_weights(w_in, w_uq, w_ukv):
    w_cq = w_in[:, :Q_LORA]
    w_ckv = w_in[:, Q_LORA:Q_LORA + KV_LORA]
    w_kr = w_in[:, Q_LORA + KV_LORA:Q_LORA + KV_LORA + QK_ROPE]
    w_gate = w_in[:, Q_LORA + KV_LORA + QK_ROPE:]
    w_kr2 = jnp.concatenate([w_kr, _rope_rot_cols(w_kr)], axis=1)
    uq = w_uq.reshape(Q_LORA, C_HEADS, QK_NOPE + QK_ROPE) * (MLA_SCALE * LOG2E)
    uq_rope = uq[..., QK_NOPE:]
    uq2 = jnp.concatenate([uq[..., :QK_NOPE], uq_rope, _rope_rot_cols(uq_rope)], axis=-1).reshape(Q_LORA, C_HEADS * 2 * LANES)
    ukv = w_ukv.reshape(KV_LORA, C_HEADS, QK_NOPE + V_DIM)
    w_uk = ukv[..., :QK_NOPE].reshape(KV_LORA, C_HEADS * QK_NOPE)
    w_uv = ukv[..., QK_NOPE:].reshape(KV_LORA, C_HEADS * V_DIM)
    return [c.astype(BF16) for c in (w_cq, w_ckv, w_kr2, w_gate, uq2, w_uk, w_uv)]


def _rope_table(pos):
    half = QK_ROPE // 2
    inv = ROPE_THETA ** (-jnp.arange(half, dtype=F32) / half)
    ang = pos.astype(F32)[:, None] * inv[None, :]
    cos, sin = jnp.cos(ang), jnp.sin(ang)
    return jnp.concatenate([cos, cos, sin, sin], axis=1)


def _odd_layer(tag, x, x16, past, wts, q_norm_w, kv_norm_w, w_out, ln_g, ln_b, tb):
    b, t, _ = x.shape
    m = b * t
    w_cq, w_ckv, w_kr2, w_gate, w_uq2, w_uk, w_uv = wts
    p_len = 0 if past is None else past[0].shape[1]
    tab = _rope_table(p_len + jnp.arange(t, dtype=I32))
    (cq16,) = _mm_rms(tag + "_in_cq", x16, w_cq, q_norm_w, [BF16])
    ckv, ckv16 = _mm_rms(tag + "_in_ckv", x16, w_ckv, kv_norm_w, [F32, BF16])
    kr, kr16 = _mm_rope_k(tag + "_in_kr", x16, w_kr2, tab)
    (gate,) = _mm(tag + "_in_gate", x16, w_gate, [F32])
    q = _mm_rope_q(tag + "_uq", cq16, w_uq2, tab)
    lat16 = ckv16.reshape(b, t, KV_LORA)
    kr16 = kr16.reshape(b, t, LANES)
    if past is not None:
        lat16 = jnp.concatenate([past[0].astype(BF16), lat16], axis=1)
        kr_past = jnp.pad(past[1], ((0, 0), (0, 0), (0, LANES - QK_ROPE))).astype(BF16)
        kr16 = jnp.concatenate([kr_past, kr16], axis=1)
    l = p_len + t
    lat2 = lat16.reshape(b * l, KV_LORA)
    (kn,) = _mm(tag + "_uk", lat2, w_uk, [BF16])
    (v,) = _mm(tag + "_uv", lat2, w_uv, [BF16])
    o = _mla(tag + "_mla", q.reshape(b, t, C_HEADS * 2 * LANES), kn.reshape(b, l, C_HEADS * QK_NOPE), kr16,
             _to_vt(v.reshape(b, l, C_WIDTH), tb), gate.reshape(b, t, C_WIDTH), tb, p_len)
    y, _ = _oproj_ln(tag + "_out1", [o.reshape(m, C_WIDTH)], [w_out.astype(BF16)], x.reshape(m, D_MODEL), ln_g, ln_b)
    return y.reshape(b, t, D_MODEL), (ckv.reshape(b, t, KV_LORA), kr.reshape(b, t, QK_ROPE))


def kernel(x_prompt, x_sample, cache_a_k, cache_a_v, cache_a_kidx, state_b_conv, state_b_ssm, cache_c_latent, cache_c_krope, t5_bias, w_in0, w_out0, conv_w, conv_b, dt_bias, a_log, d_skip, ssm_norm_w, ln0_g, ln0_b, w_in1, q_norm_w, kv_norm_w, w_uq, w_ukv, w_out1, ln1_g, ln1_b):
    tb_prompt = 256
    tb_sample = CHUNK
    ew = _even_weights(w_in0[0])
    eprm = (w_out0[0], conv_w[0], conv_b[0], dt_bias[0], a_log[0], d_skip[0], ssm_norm_w[0], ln0_g[0], ln0_b[0], t5_bias)
    yp, yp16, st_p = _even_layer("p0", x_prompt, None, ew, *eprm, tb_prompt)
    past = (cache_a_k[0], cache_a_v[0], cache_a_kidx[0], state_b_conv[0], state_b_ssm[0])
    ys, ys16, st_s = _even_layer("s0", x_sample, past, ew, *eprm, tb_sample)
    ow = _odd_weights(w_in1[0], w_uq[0], w_ukv[0])
    oprm = (q_norm_w[0], kv_norm_w[0], w_out1[0], ln1_g[0], ln1_b[0])
    yp, od_p = _odd_layer("p1", yp, yp16, None, ow, *oprm, tb_prompt)
    ys, od_s = _odd_layer("s1", ys, ys16, (cache_c_latent[0], cache_c_krope[0]), ow, *oprm, tb_sample)
    e = lambda a: a[None]
    return (yp, ys, e(st_p[0]), e(st_s[0]), e(st_p[1]), e(st_s[1]), e(st_p[2]), e(st_s[2]),
            e(st_p[3]), e(st_s[3]), e(st_p[4]), e(st_s[4]), e(od_p[0]), e(od_s[0]), e(od_p[1]), e(od_s[1]))
```

```python
import functools
import math

import jax
import jax.numpy as jnp
from jax import lax
from jax.experimental import pallas as pl
from jax.experimental.pallas import tpu as pltpu

F32 = jnp.float32
BF16 = jnp.bfloat16
I32 = jnp.int32

D_MODEL = 2048
CHUNK = 64
A_HEADS = 8
A_HEAD_DIM = 128
A_WIDTH = A_HEADS * A_HEAD_DIM
IDX_HEADS = 16
IDX_DIM = 64
IDX_W_SCALE = (IDX_HEADS * IDX_DIM) ** -0.5
TOPK_MAX = 256
N_BUCKETS = 32
MAX_DISTANCE = 128
B_HEAD_DIM = 64
B_WIDTH = D_MODEL
B_HEADS = B_WIDTH // B_HEAD_DIM
B_GROUPS = 4
B_HPG = B_HEADS // B_GROUPS
B_STATE = 128
CONV_W = 4
B_CONV_DIM = B_WIDTH + 2 * B_GROUPS * B_STATE
C_HEADS = 16
Q_LORA = 512
KV_LORA = 512
QK_NOPE = 128
QK_ROPE = 64
V_DIM = 128
C_WIDTH = C_HEADS * V_DIM
ROPE_THETA = 10000.0
MLA_SCALE = (QK_NOPE + QK_ROPE) ** -0.5
DEPTH = 2
ALPHA = (2 * DEPTH) ** 0.25
EPS = 1e-5

LANES = 128
SUBLANES = 8
VMEM_LIMIT = 56 * 1024 * 1024
LOG2E = math.log2(math.e)
INT_MIN = -(2 ** 31)
INT_MAX = 2 ** 31 - 1
KEY_NEG_INF = (0xFF800000 ^ 0x7FFFFFFF) - (1 << 32)
NEG_BIG = -1e30


def _params(sem):
    return pltpu.CompilerParams(dimension_semantics=sem, vmem_limit_bytes=VMEM_LIMIT)


def _dot_nt(a, b):
    return lax.dot_general(a, b, (((1,), (1,)), ((), ())), preferred_element_type=F32)


def _silu(x):
    return x * (1.0 / (1.0 + jnp.exp(-x)))


def _split3(v):
    hi = v.astype(BF16)
    r = v - hi.astype(F32)
    mid = r.astype(BF16)
    return hi, mid, (r - mid.astype(F32)).astype(BF16)


def _dot01(a, b, split_rhs):
    if split_rhs:
        parts = [jnp.dot(a, t, preferred_element_type=F32) for t in _split3(b)]
    else:
        parts = [jnp.dot(t, b, preferred_element_type=F32) for t in _split3(a)]
    return parts[0] + parts[1] + parts[2]


def _mm_body(a_ref, w_ref, *o_refs, kinds):
    a = a_ref[...]
    if a.dtype != w_ref.dtype:
        a = a.astype(w_ref.dtype)
    acc = jnp.dot(a, w_ref[...], preferred_element_type=F32)
    tm, tn = acc.shape
    for kind, o in zip(kinds, o_refs):
        if kind[0] == "plain":
            o[...] = acc.astype(o.dtype)
        elif kind[0] == "silu":
            o[...] = _silu(acc).astype(o.dtype)
        elif kind[0] == "acopy":
            @pl.when(pl.program_id(1) == 0)
            def _():
                o[...] = a
        elif kind[0] == "vt":
            tb = kind[2]
            for hh in range(tn // LANES):
                for kt in range(tm // tb):
                    o[0, hh, kt] = acc[kt * tb:(kt + 1) * tb, hh * LANES:(hh + 1) * LANES].T.astype(o.dtype)
        elif kind[0] == "heads":
            hd = kind[2]
            for hh in range(tn // hd):
                o[0, hh] = acc[:, hh * hd:(hh + 1) * hd].astype(o.dtype)


def _mm_rms_body(a_ref, w_ref, g_ref, *o_refs):
    acc = jnp.dot(a_ref[...], w_ref[...], preferred_element_type=F32)
    y = acc * lax.rsqrt(jnp.mean(acc * acc, axis=-1, keepdims=True) + EPS) * g_ref[...]
    for o in o_refs:
        o[...] = y.astype(o.dtype)


def _rope_half(t):
    return t + pltpu.roll(t, QK_ROPE, 1)


def _mm_rope_k_body(a_ref, w_ref, tab_ref, o32_ref, o16_ref):
    acc = jnp.dot(a_ref[...], w_ref[...], preferred_element_type=F32)
    r = _rope_half(acc * tab_ref[...])
    lane = lax.broadcasted_iota(I32, r.shape, 1)
    o32_ref[...] = r[:, :QK_ROPE]
    o16_ref[...] = jnp.where(lane < QK_ROPE, r, 0.0).astype(o16_ref.dtype)


def _mm_rope_q_body(a_ref, w_ref, tab_ref, o_ref, *, heads):
    acc = jnp.dot(a_ref[...], w_ref[...], preferred_element_type=F32)
    tab = tab_ref[...]
    lane = lax.broadcasted_iota(I32, tab.shape, 1)
    for h in range(heads):
        base = h * 2 * LANES
        o_ref[:, base:base + LANES] = acc[:, base:base + LANES].astype(o_ref.dtype)
        r = _rope_half(acc[:, base + LANES:base + 2 * LANES] * tab)
        o_ref[:, base + LANES:base + 2 * LANES] = jnp.where(lane < QK_ROPE, r, 0.0).astype(o_ref.dtype)


def _mm_call(name, body, a, w, extra, extra_specs, out_cols, out_dtypes, tm, tn):
    m, k = a.shape
    n = w.shape[1]
    tm = math.gcd(tm, m)
    tn = min(tn, n)
    assert tm % SUBLANES == 0 and n % tn == 0, (m, n, tm, tn)
    oc = [tn if c is None else c for c in out_cols]
    return pl.pallas_call(
        body,
        grid=(m // tm, n // tn),
        in_specs=[pl.BlockSpec((tm, k), lambda i, j: (i, 0)),
                  pl.BlockSpec((k, tn), lambda i, j: (0, j))] + extra_specs(tm, tn),
        out_specs=[pl.BlockSpec((tm, c), lambda i, j: (i, j)) for c in oc],
        out_shape=[jax.ShapeDtypeStruct((m, (n // tn) * c), d) for c, d in zip(oc, out_dtypes)],
        compiler_params=_params(("parallel", "parallel")),
        name=name,
    )(a, w, *extra)


def _mm(name, a, w, outs, tm=1024, tn=512):
    m, k = a.shape
    n = w.shape[1]
    tm = math.gcd(tm, m, *[o[1] for o in outs if isinstance(o, tuple) and o[0] in ("vt", "heads")])
    tn = min(tn, n)
    assert tm % SUBLANES == 0 and n % tn == 0, (m, n, tm, tn)
    kinds, specs, shapes = [], [], []
    for o in outs:
        o = o if isinstance(o, tuple) else ("plain", o)
        if o[0] in ("plain", "silu"):
            kinds.append((o[0],))
            specs.append(pl.BlockSpec((tm, tn), lambda i, j: (i, j)))
            shapes.append(jax.ShapeDtypeStruct((m, n), o[1]))
        elif o[0] == "acopy":
            kinds.append(("acopy",))
            specs.append(pl.BlockSpec((tm, k), lambda i, j: (i, 0)))
            shapes.append(jax.ShapeDtypeStruct((m, k), w.dtype))
        elif o[0] == "vt":
            _, t, tb, dt = o
            assert t % tm == 0 and tm % tb == 0 and tn % LANES == 0
            per = t // tm
            kinds.append(("vt", t, tb))
            specs.append(pl.BlockSpec((1, tn // LANES, tm // tb, LANES, tb),
                                      lambda i, j: (i // per, j, i % per, 0, 0)))
            shapes.append(jax.ShapeDtypeStruct((m // t, n // LANES, t // tb, LANES, tb), dt))
        elif o[0] == "heads":
            _, t, hd, dt = o
            assert t % tm == 0 and tn % hd == 0
            per = t // tm
            kinds.append(("heads", t, hd))
            specs.append(pl.BlockSpec((1, tn // hd, tm, hd), lambda i, j: (i // per, j, i % per, 0)))
            shapes.append(jax.ShapeDtypeStruct((m // t, n // hd, t, hd), dt))
        else:
            raise ValueError(o)
    return pl.pallas_call(
        functools.partial(_mm_body, kinds=tuple(kinds)),
        grid=(m // tm, n // tn),
        in_specs=[pl.BlockSpec((tm, k), lambda i, j: (i, 0)), pl.BlockSpec((k, tn), lambda i, j: (0, j))],
        out_specs=specs,
        out_shape=shapes,
        compiler_params=_params(("parallel", "arbitrary")),
        name=name,
    )(a, w)


def _mm_conv_body(a_ref, w_ref, st_ref, cw_ref, cb_ref, o_ref, tail_ref, ext_scr, prev_scr, *, tiles_per_seq):
    i, j = pl.program_id(0), pl.program_id(1)
    tm, tn = o_ref.shape

    @pl.when(i % tiles_per_seq == 0)
    def _():
        ext_scr[0:SUBLANES, :] = st_ref[0]

    @pl.when(i % tiles_per_seq != 0)
    def _():
        ext_scr[0:SUBLANES, :] = prev_scr[j]

    cw_ = min(tn, 2 * LANES)
    for c in range(tn // cw_):
        cs = slice(c * cw_, (c + 1) * cw_)
        ext_scr[SUBLANES:SUBLANES + tm, cs] = jnp.dot(a_ref[...], w_ref[:, cs], preferred_element_type=F32)
        conv = cb_ref[:, cs]
        for tap in range(CONV_W):
            lo = SUBLANES - (CONV_W - 1) + tap
            conv = conv + ext_scr[lo:lo + tm, cs] * cw_ref[tap:tap + 1, cs]
        o_ref[:, cs] = _silu(conv).astype(o_ref.dtype)
    tail = ext_scr[tm:tm + SUBLANES, :]
    prev_scr[j] = tail
    tail_ref[0] = tail


def _mm_conv(name, a, w, state, conv_w, conv_b, t, tm=1024, tn=512):
    m, k = a.shape
    n = w.shape[1]
    tm = math.gcd(tm, t)
    tn = min(tn, n)
    assert n % tn == 0 and tm % SUBLANES == 0
    per = t // tm
    return pl.pallas_call(
        functools.partial(_mm_conv_body, tiles_per_seq=per),
        grid=(m // tm, n // tn),
        in_specs=[pl.BlockSpec((tm, k), lambda i, j: (i, 0)),
                  pl.BlockSpec((k, tn), lambda i, j: (0, j)),
                  pl.BlockSpec((1, SUBLANES, tn), lambda i, j: (i // per, 0, j)),
                  pl.BlockSpec((CONV_W, tn), lambda i, j: (0, j)),
                  pl.BlockSpec((1, tn), lambda i, j: (0, j))],
        out_specs=[pl.BlockSpec((tm, tn), lambda i, j: (i, j)),
                   pl.BlockSpec((1, SUBLANES, tn), lambda i, j: (i, 0, j))],
        out_shape=[jax.ShapeDtypeStruct((m, n), F32), jax.ShapeDtypeStruct((m // tm, SUBLANES, n), F32)],
        scratch_shapes=[pltpu.VMEM((tm + SUBLANES, tn), F32), pltpu.VMEM((n // tn, SUBLANES, tn), F32)],
        compiler_params=_params(("arbitrary", "arbitrary")),
        name=name,
    )(a, w, state, conv_w, conv_b.reshape(1, n))


def _mm_rms(name, a, w, g, out_dtypes, tm=1024):
    n = w.shape[1]
    return _mm_call(name, _mm_rms_body, a, w, [g.reshape(1, n)],
                    lambda tm_, tn_: [pl.BlockSpec((1, n), lambda i, j: (0, 0))],
                    [None] * len(out_dtypes), out_dtypes, tm, n)


def _tab_spec(t_rows):
    def spec(tm, tn):
        nt = t_rows // tm
        return [pl.BlockSpec((tm, LANES), lambda i, j: (i % nt, 0))]
    return spec


def _mm_rope_k(name, a, w, tab, tm=1024):
    tm = min(tm, tab.shape[0])
    return _mm_call(name, _mm_rope_k_body, a, w, [tab], _tab_spec(tab.shape[0]), [QK_ROPE, LANES], [F32, BF16],
                    tm, LANES)


def _mm_rope_q(name, a, w, tab, tm=1024, heads_per_block=2):
    tm = min(tm, tab.shape[0])
    body = functools.partial(_mm_rope_q_body, heads=heads_per_block)
    return _mm_call(name, body, a, w, [tab], _tab_spec(tab.shape[0]), [None], [BF16], tm,
                    heads_per_block * 2 * LANES)[0]


def _oproj_body(*refs, n_parts):
    parts = refs[:n_parts]
    ws = refs[n_parts:2 * n_parts]
    x_ref, g_ref, b_ref, o32_ref, o16_ref = refs[2 * n_parts:]
    acc = ALPHA * x_ref[...]
    for p, w in zip(parts, ws):
        acc = acc + jnp.dot(p[...], w[...], preferred_element_type=F32)
    mu = jnp.mean(acc, axis=-1, keepdims=True)
    d = acc - mu
    var = jnp.mean(d * d, axis=-1, keepdims=True)
    y = d * lax.rsqrt(var + EPS) * g_ref[...] + b_ref[...]
    o32_ref[...] = y
    o16_ref[...] = y.astype(o16_ref.dtype)


def _oproj_ln(name, parts, ws, x, g, b, tm=256):
    m, n = x.shape
    tm = min(tm, m)
    np_ = len(parts)
    const = lambda i: (0, 0)
    return pl.pallas_call(
        functools.partial(_oproj_body, n_parts=np_),
        grid=(m // tm,),
        in_specs=[pl.BlockSpec((tm, p.shape[1]), lambda i: (i, 0)) for p in parts]
        + [pl.BlockSpec(w.shape, const, pipeline_mode=pl.Buffered(1)) for w in ws]
        + [pl.BlockSpec((tm, n), lambda i: (i, 0)),
           pl.BlockSpec((1, n), const), pl.BlockSpec((1, n), const)],
        out_specs=[pl.BlockSpec((tm, n), lambda i: (i, 0))] * 2,
        out_shape=[jax.ShapeDtypeStruct((m, n), F32), jax.ShapeDtypeStruct((m, n), BF16)],
        compiler_params=_params(("parallel",)),
        name=name,
    )(*parts, *ws, x, g.reshape(1, n), b.reshape(1, n))


def _to_vt(v, tb):
    b, l, hd = v.shape
    h = hd // LANES
    return jnp.transpose(v.reshape(b, l // tb, tb, h, LANES), (0, 3, 1, 4, 2))


def _softmax_init(m_scr, l_scr, acc_scr, heads, tb):
    for h in range(heads):
        m_scr[h] = jnp.full((1, tb), NEG_BIG, F32)
        l_scr[h] = jnp.zeros((1, tb), F32)
        acc_scr[h] = jnp.zeros(acc_scr.shape[1:], F32)


def _softmax_step(s, vts, m_scr, l_scr, acc_scr, h):
    m_prev = m_scr[h]
    m_new = jnp.maximum(m_prev, jnp.max(s, axis=0, keepdims=True))
    alpha = jnp.exp2(m_prev - m_new)
    p = jnp.exp2(s - m_new)
    l_scr[h] = alpha * l_scr[h] + jnp.sum(p, axis=0, keepdims=True)
    pb = p.astype(BF16)
    tk = s.shape[0] // len(vts)
    acc = alpha * acc_scr[h]
    for i, vt in enumerate(vts):
        acc = acc + jnp.dot(vt, pb[i * tk:(i + 1) * tk], preferred_element_type=F32)
    acc_scr[h] = acc
    m_scr[h] = m_new


def _softmax_finish(l_scr, acc_scr, g_ref, o_ref, heads):
    for h in range(heads):
        hs = slice(h * LANES, (h + 1) * LANES)
        o = (acc_scr[h] / l_scr[h]).T
        o_ref[0, :, hs] = (o * g_ref[0, :, hs]).astype(o_ref.dtype)


def _t5_bucket(rel):
    half = N_BUCKETS // 2
    max_exact = half // 2
    ret = jnp.where(rel < 0, half, 0)
    n = jnp.abs(rel)
    nf = jnp.maximum(n, 1).astype(F32)
    large = max_exact + (jnp.log(nf / max_exact) / math.log(MAX_DISTANCE / max_exact) * (half - max_exact)).astype(jnp.int32)
    large = jnp.minimum(large, half - 1)
    return ret + jnp.where(n < max_exact, n, large)


def _num_special_tiles(tb):
    return (MAX_DISTANCE - 2 + 2 * tb) // tb


def _bias_tables(t5_bias, tb):
    ns = _num_special_tiles(tb)
    s = jnp.arange(tb, dtype=I32)[:, None]
    t = jnp.arange(tb, dtype=I32)[None, :]
    bucket = jnp.stack([_t5_bucket(tb * d + t - s) for d in range(ns)])
    far = t5_bias[_t5_bucket(jnp.int32(tb * ns))]
    out = jnp.zeros((ns, A_HEADS, tb, tb), F32)
    for k in range(N_BUCKETS):
        out = jnp.where(bucket[:, None] == k, t5_bias[k][None, :, None, None], out)
    return (out - far[None, :, None, None]) * LOG2E


def _dsa_body(iq_ref, iwt_ref, ik_ref, q_ref, k_ref, vt_ref, g_ref, bias_ref, o_ref,
              key_scr, mb_scr, x_scr, m_scr, l_scr, acc_scr, *, tb, qt0, ns, n_sel, idx_bits):
    qt = pl.program_id(1) + qt0
    nkv = qt + 1
    krow = lax.broadcasted_iota(I32, (tb, tb), 0)
    qcol = lax.broadcasted_iota(I32, (tb, tb), 1)
    diag_vis = (krow // CHUNK) <= (qcol // CHUNK)

    wt = iwt_ref[0] * IDX_W_SCALE

    def idx_tile(j, carry):
        kt = ik_ref[0, pl.ds(pl.multiple_of(j * tb, tb), tb), :]
        acc = jnp.zeros((tb, tb), F32)
        for h in range(IDX_HEADS):
            acc = acc + jnp.maximum(_dot_nt(kt, iq_ref[0, h]), 0.0) * wt[h:h + 1, :]
        acc = jnp.where(acc == 0.0, 0.0, acc)
        s = jnp.where(jnp.logical_or(diag_vis, j < qt), acc, -jnp.inf)
        bits = pltpu.bitcast(s, I32)
        key_scr[j] = bits ^ ((bits >> 31) & 0x7FFFFFFF)
        return carry

    lax.fori_loop(0, nkv, idx_tile, 0)

    def count(pred):
        def body(j, c):
            f = jnp.where(pred(key_scr[j], j), 1.0, 0.0)
            return c + jnp.sum(f.reshape(tb // SUBLANES, SUBLANES, tb), axis=0)
        c = lax.fori_loop(0, nkv, body, jnp.zeros((SUBLANES, tb), F32))
        return jnp.sum(c, axis=0, keepdims=True)

    nsel = float(n_sel)
    thr = jnp.where(count(lambda kt, j: kt >= 0) >= nsel, 0, INT_MIN).astype(I32)

    def bit_body(i, thr):
        cand = thr + jnp.left_shift(jnp.int32(1), 30 - i)
        return jnp.where(count(lambda kt, j: kt >= cand) >= nsel, cand, thr)

    thr = lax.fori_loop(0, 31, bit_body, thr)

    need = nsel - count(lambda kt, j: kt > thr)
    excess = jnp.logical_and(count(lambda kt, j: kt == thr) > need, thr > KEY_NEG_INF)
    x_scr[...] = jnp.full((1, tb), INT_MAX, I32)

    @pl.when(jnp.max(jnp.where(excess, 1.0, 0.0)) > 0.0)
    def _():
        def xbit(i, x):
            cand = x + jnp.left_shift(jnp.int32(1), idx_bits - 1 - i)
            c = count(lambda kt, j: jnp.logical_and(kt == thr, krow + j * tb < cand))
            return jnp.where(c < need, cand, x)
        x = lax.fori_loop(0, idx_bits, xbit, jnp.zeros((1, tb), I32))
        x_scr[...] = jnp.where(excess, x, INT_MAX)

    xcut = x_scr[...]

    def mask_tile(j, carry):
        kt = key_scr[j]
        tie = jnp.logical_and(kt == thr, krow + j * tb <= xcut)
        sel = jnp.logical_and(jnp.logical_or(kt > thr, tie), kt != KEY_NEG_INF)
        mb_scr[j] = jnp.where(sel, 0.0, -jnp.inf)
        return carry

    lax.fori_loop(0, nkv, mask_tile, 0)

    _softmax_init(m_scr, l_scr, acc_scr, A_HEADS, tb)

    def attn_tiles(j, nt, biased):
        rows = pl.ds(pl.multiple_of(j * tb, tb), nt * tb)
        mb = mb_scr[pl.ds(j, nt)].reshape(nt * tb, tb)
        qk = [_dot_nt(k_ref[0, rows, h * A_HEAD_DIM:(h + 1) * A_HEAD_DIM],
                      q_ref[0, :, h * A_HEAD_DIM:(h + 1) * A_HEAD_DIM]) for h in range(A_HEADS)]
        for h in range(A_HEADS):
            s = mb + qk[h]
            if biased:
                s = bias_ref[qt - j, h] + s
            _softmax_step(s, [vt_ref[0, h, j + i] for i in range(nt)], m_scr, l_scr, acc_scr, h)

    def far_pair(j2, carry):
        attn_tiles(2 * j2, 2, False)
        return carry

    def far_tile(j, carry):
        attn_tiles(j, 1, False)
        return carry

    def near_tile(j, carry):
        attn_tiles(j, 1, True)
        return carry

    n_far = jnp.maximum(qt - (ns - 1), 0)
    n_pair = n_far // 2
    lax.fori_loop(0, n_pair, far_pair, 0)
    lax.fori_loop(2 * n_pair, n_far, far_tile, 0)
    lax.fori_loop(n_far, nkv, near_tile, 0)
    _softmax_finish(l_scr, acc_scr, g_ref, o_ref, A_HEADS)


def _dsa(name, iq, iwt, ik, q, k, vt, gate, bias_tabs, tb, past_len):
    b, t, aw = q.shape
    l = k.shape[1]
    assert t % tb == 0 and l % tb == 0 and past_len % tb == 0 and l == past_len + t
    nq, nkv = t // tb, l // tb
    ns = _num_special_tiles(tb)
    n_sel = min(TOPK_MAX, l // 4)
    body = functools.partial(_dsa_body, tb=tb, qt0=past_len // tb, ns=ns, n_sel=n_sel,
                             idx_bits=max(1, (l - 1).bit_length()))
    once = pl.Buffered(1)
    return pl.pallas_call(
        body,
        grid=(b, nq),
        in_specs=[
            pl.BlockSpec((1, IDX_HEADS, tb, IDX_DIM), lambda bi, qi: (bi, 0, qi, 0)),
            pl.BlockSpec((1, IDX_HEADS, tb), lambda bi, qi: (bi, 0, qi)),
            pl.BlockSpec((1, l, IDX_DIM), lambda bi, qi: (bi, 0, 0), pipeline_mode=once),
            pl.BlockSpec((1, tb, aw), lambda bi, qi: (bi, qi, 0)),
            pl.BlockSpec((1, l, aw), lambda bi, qi: (bi, 0, 0), pipeline_mode=once),
            pl.BlockSpec((1, A_HEADS, nkv, A_HEAD_DIM, tb), lambda bi, qi: (bi, 0, 0, 0, 0), pipeline_mode=once),
            pl.BlockSpec((1, tb, aw), lambda bi, qi: (bi, qi, 0)),
            pl.BlockSpec(bias_tabs.shape, lambda bi, qi: (0, 0, 0, 0), pipeline_mode=once),
        ],
        out_specs=pl.BlockSpec((1, tb, aw), lambda bi, qi: (bi, qi, 0)),
        out_shape=jax.ShapeDtypeStruct((b, t, aw), BF16),
        scratch_shapes=[
            pltpu.VMEM((nkv, tb, tb), I32),
            pltpu.VMEM((nkv, tb, tb), F32),
            pltpu.VMEM((1, tb), I32),
            pltpu.VMEM((A_HEADS, 1, tb), F32),
            pltpu.VMEM((A_HEADS, 1, tb), F32),
            pltpu.VMEM((A_HEADS, A_HEAD_DIM, tb), F32),
        ],
        compiler_params=_params(("parallel", "parallel")),
        name=name,
    )(iq, iwt, ik, q, k, vt, gate, bias_tabs)


def _ssd_body(zg_ref, xbc_ref, dt_ref, dtt_ref, dtb_ref, dtbt_ref, alog_ref, alogt_ref,
              dsk_ref, nw_ref, exp_ref, h0_ref, y_ref, hout_ref, h_scr, yi_scr):
    c = pl.program_id(1)
    l = CHUNK
    gw = B_WIDTH // B_GROUPS

    @pl.when(c == 0)
    def _():
        h_scr[...] = h0_ref[0]

    xbc = xbc_ref[0]
    xs = xbc[:, :B_WIDTH]

    dt = jax.nn.softplus(dt_ref[0, 0] + dtb_ref[...])
    dtt = jax.nn.softplus(dtt_ref[0, 0] + dtbt_ref[...])
    a = -jnp.exp(alog_ref[...])
    at = -jnp.exp(alogt_ref[...])
    ti = lax.broadcasted_iota(I32, (l, l), 0)
    si = lax.broadcasted_iota(I32, (l, l), 1)
    causal = si <= ti
    lower = jnp.where(causal, 1.0, 0.0).astype(BF16)
    upper = jnp.where(ti <= si, 1.0, 0.0).astype(BF16)
    acum = _dot01(lower, dt * a, split_rhs=True)
    acumt = _dot01(dtt * at, upper, split_rhs=False)
    a_last = acum[l - 1:l, :]
    expand = exp_ref[...]
    e_full = _dot01(jnp.exp(acum), expand, split_rhs=False)
    tail_full = _dot01(jnp.exp(a_last - acum) * dt, expand, split_rhs=False)
    xt = (xs * tail_full).astype(BF16)
    xs16 = xs.astype(BF16)
    lane = lax.broadcasted_iota(I32, (l, 2 * B_HEAD_DIM), 1)

    for g in range(B_GROUPS):
        bm = xbc[:, B_WIDTH + g * B_STATE:B_WIDTH + (g + 1) * B_STATE].astype(BF16)
        cm = xbc[:, B_WIDTH + (B_GROUPS + g) * B_STATE:B_WIDTH + (B_GROUPS + g + 1) * B_STATE].astype(BF16)
        cb = _dot_nt(cm, bm)
        gs = slice(g * gw, (g + 1) * gw)
        hg = h_scr[g]
        y_state = jnp.dot(cm, hg.astype(BF16), preferred_element_type=F32) * e_full[:, gs]
        for pr in range(B_HPG // 2):
            ws = []
            for r in (g * B_HPG + 2 * pr, g * B_HPG + 2 * pr + 1):
                seg = acum[:, r:r + 1] - acumt[r:r + 1, :]
                decay = jnp.exp(jnp.where(causal, seg, -jnp.inf))
                ws.append((cb * decay * dtt[r:r + 1, :]).astype(BF16))
            c0 = g * gw + pr * 2 * B_HEAD_DIM
            xp = xs16[:, c0:c0 + 2 * B_HEAD_DIM]
            y0 = jnp.dot(ws[0], xp, preferred_element_type=F32)
            y1 = jnp.dot(ws[1], xp, preferred_element_type=F32)
            yi_scr[:, c0:c0 + 2 * B_HEAD_DIM] = jnp.where(lane < B_HEAD_DIM, y0, y1)
        yi_scr[:, gs] = yi_scr[:, gs] + y_state
        upd = lax.dot_general(bm, xt[:, gs], (((0,), (0,)), ((), ())), preferred_element_type=F32)
        h_scr[g] = hg * e_full[l - 1:l, gs] + upd

    y = (yi_scr[...] + dsk_ref[...] * xs) * zg_ref[0]
    for g in range(B_GROUPS):
        gs = slice(g * gw, (g + 1) * gw)
        yg = y[:, gs]
        yg = yg * lax.rsqrt(jnp.mean(yg * yg, axis=-1, keepdims=True) + EPS)
        y_ref[0, :, gs] = (yg * nw_ref[:, gs]).astype(y_ref.dtype)

    @pl.when(c == pl.num_programs(1) - 1)
    def _():
        hout_ref[0] = h_scr[...]


def _ssd(name, zg, xbc, dt_raw, dt_bias, a_log, d_skip, norm_w, h0):
    b, t, _ = zg.shape
    nc = t // CHUNK
    gw = B_WIDTH // B_GROUPS
    dt4 = dt_raw.reshape(b, nc, CHUNK, B_HEADS)
    dtt4 = jnp.swapaxes(dt4, 2, 3)
    h0t = jnp.transpose(h0.reshape(b, B_GROUPS, B_HPG, B_HEAD_DIM, B_STATE), (0, 1, 4, 2, 3)).reshape(b, B_GROUPS, B_STATE, gw)
    expand = jnp.repeat(jnp.eye(B_HEADS, dtype=BF16), B_HEAD_DIM, axis=1)
    dsk = jnp.repeat(d_skip, B_HEAD_DIM).reshape(1, B_WIDTH)
    row = lambda v: v.reshape(1, -1)
    colv = lambda v: v.reshape(-1, 1)
    const2 = lambda bi, ci: (0, 0)
    y, hout = pl.pallas_call(
        _ssd_body,
        grid=(b, nc),
        in_specs=[
            pl.BlockSpec((1, CHUNK, B_WIDTH), lambda bi, ci: (bi, ci, 0)),
            pl.BlockSpec((1, CHUNK, B_CONV_DIM), lambda bi, ci: (bi, ci, 0)),
            pl.BlockSpec((1, 1, CHUNK, B_HEADS), lambda bi, ci: (bi, ci, 0, 0)),
            pl.BlockSpec((1, 1, B_HEADS, CHUNK), lambda bi, ci: (bi, ci, 0, 0)),
            pl.BlockSpec((1, B_HEADS), const2),
            pl.BlockSpec((B_HEADS, 1), const2),
            pl.BlockSpec((1, B_HEADS), const2),
            pl.BlockSpec((B_HEADS, 1), const2),
            pl.BlockSpec((1, B_WIDTH), const2),
            pl.BlockSpec((1, B_WIDTH), const2),
            pl.BlockSpec((B_HEADS, B_WIDTH), const2),
            pl.BlockSpec((1, B_GROUPS, B_STATE, gw), lambda bi, ci: (bi, 0, 0, 0)),
        ],
        out_specs=[pl.BlockSpec((1, CHUNK, B_WIDTH), lambda bi, ci: (bi, ci, 0)),
                   pl.BlockSpec((1, B_GROUPS, B_STATE, gw), lambda bi, ci: (bi, 0, 0, 0))],
        out_shape=[jax.ShapeDtypeStruct((b, t, B_WIDTH), BF16),
                   jax.ShapeDtypeStruct((b, B_GROUPS, B_STATE, gw), F32)],
        scratch_shapes=[pltpu.VMEM((B_GROUPS, B_STATE, gw), F32),
                        pltpu.VMEM((CHUNK, B_WIDTH), F32)],
        compiler_params=_params(("parallel", "arbitrary")),
        name=name,
    )(zg, xbc, dt4, dtt4, row(dt_bias), colv(dt_bias), row(a_log), colv(a_log),
      dsk, row(norm_w), expand, h0t)
    hnew = jnp.transpose(hout.reshape(b, B_GROUPS, B_STATE, B_HPG, B_HEAD_DIM), (0, 1, 3, 4, 2))
    return y, hnew.reshape(b, B_HEADS, B_HEAD_DIM, B_STATE)


def _mla_body(q_ref, kn_ref, kr_ref, vt_ref, g_ref, o_ref, m_scr, l_scr, acc_scr, *, tb, qt0, heads):
    qt = pl.program_id(2) + qt0
    _softmax_init(m_scr, l_scr, acc_scr, heads, tb)
    krow = lax.broadcasted_iota(I32, (tb, tb), 0)
    qcol = lax.broadcasted_iota(I32, (tb, tb), 1)
    diag_vis = (krow // CHUNK) <= (qcol // CHUNK)

    def tiles(j, nt, masked):
        rows = pl.ds(pl.multiple_of(j * tb, tb), nt * tb)
        kr = kr_ref[0, rows, :]
        qk = [_dot_nt(jnp.concatenate([kn_ref[0, rows, h * QK_NOPE:(h + 1) * QK_NOPE], kr], axis=1),
                      q_ref[0, :, h * 2 * LANES:(h + 1) * 2 * LANES]) for h in range(heads)]
        for h in range(heads):
            s = jnp.where(diag_vis, qk[h], -jnp.inf) if masked else qk[h]
            _softmax_step(s, [vt_ref[0, h, j + i] for i in range(nt)], m_scr, l_scr, acc_scr, h)

    def full_pair(j2, carry):
        tiles(2 * j2, 2, False)
        return carry

    def full_tile(j, carry):
        tiles(j, 1, False)
        return carry

    n_pair = qt // 2
    lax.fori_loop(0, n_pair, full_pair, 0)
    lax.fori_loop(2 * n_pair, qt, full_tile, 0)
    tiles(qt, 1, True)
    _softmax_finish(l_scr, acc_scr, g_ref, o_ref, heads)


def _mla(name, q, kn, kr, vt, gate, tb, past_len, heads=8):
    b, t, _ = q.shape
    l = kn.shape[1]
    assert t % tb == 0 and past_len % tb == 0 and l == past_len + t and C_HEADS % heads == 0
    nkv = l // tb
    body = functools.partial(_mla_body, tb=tb, qt0=past_len // tb, heads=heads)
    return pl.pallas_call(
        body,
        grid=(b, C_HEADS // heads, t // tb),
        in_specs=[
            pl.BlockSpec((1, tb, heads * 2 * LANES), lambda bi, h, qi: (bi, qi, h)),
            pl.BlockSpec((1, l, heads * QK_NOPE), lambda bi, h, qi: (bi, 0, h)),
            pl.BlockSpec((1, l, LANES), lambda bi, h, qi: (bi, 0, 0)),
            pl.BlockSpec((1, heads, nkv, V_DIM, tb), lambda bi, h, qi: (bi, h, 0, 0, 0)),
            pl.BlockSpec((1, tb, heads * V_DIM), lambda bi, h, qi: (bi, qi, h)),
        ],
        out_specs=pl.BlockSpec((1, tb, heads * V_DIM), lambda bi, h, qi: (bi, qi, h)),
        out_shape=jax.ShapeDtypeStruct((b, t, C_WIDTH), BF16),
        scratch_shapes=[pltpu.VMEM((heads, 1, tb), F32), pltpu.VMEM((heads, 1, tb), F32),
                        pltpu.VMEM((heads, V_DIM, tb), F32)],
        compiler_params=_params(("parallel", "parallel", "parallel")),
        name=name,
    )(q, kn, kr, vt, gate)


def _even_weights(w_in):
    offs = [0]
    for s in (A_WIDTH, A_WIDTH, A_WIDTH, A_WIDTH, IDX_HEADS * IDX_DIM, IDX_DIM, IDX_HEADS, B_WIDTH, B_CONV_DIM, B_HEADS):
        offs.append(offs[-1] + s)
    cols = [w_in[:, offs[i]:offs[i + 1]] for i in range(10)]
    aq, ak, av, ag, iq, ik, iw, bz, bxbc, bdt = cols
    aq = aq * (A_HEAD_DIM ** -0.5 * LOG2E)
    pad = jnp.zeros((w_in.shape[0], LANES - IDX_DIM - IDX_HEADS - B_HEADS), w_in.dtype)
    small = jnp.concatenate([ik, iw, bdt, pad], axis=1)
    return [c.astype(BF16) for c in (aq, ak, av, ag, iq, small, bz, bxbc)]


def _even_layer(tag, x, past, wts, w_out, conv_w, conv_b, dt_bias, a_log, d_skip, norm_w, ln_g, ln_b, t5_bias, tb):
    b, t, _ = x.shape
    m = b * t
    x2 = x.reshape(m, D_MODEL)
    w_aq, w_ak, w_av, w_ag, w_iq, w_small, w_bz, w_bxbc = wts
    aq, xb = _mm(tag + "_in_aq", x2, w_aq, [BF16, ("acopy",)])
    ak, ak16 = _mm(tag + "_in_ak", xb, w_ak, [F32, BF16])
    (ag,) = _mm(tag + "_in_ag", xb, w_ag, [("silu", F32)])
    (iqt,) = _mm(tag + "_in_iq", xb, w_iq, [("heads", t, IDX_DIM, BF16)])
    (small,) = _mm(tag + "_in_small", xb, w_small, [F32])
    (bzg,) = _mm(tag + "_in_bz", xb, w_bz, [("silu", F32)])
    ik = small[:, :IDX_DIM].reshape(b, t, IDX_DIM)
    iwt = jnp.swapaxes(small[:, IDX_DIM:IDX_DIM + IDX_HEADS].reshape(b, t, IDX_HEADS), 1, 2)
    bdt = small[:, IDX_DIM + IDX_HEADS:IDX_DIM + IDX_HEADS + B_HEADS].reshape(b, t, B_HEADS)
    k16 = ak16.reshape(b, t, A_WIDTH)
    ik16 = ik.astype(BF16)
    if past is None:
        p_len = 0
        av, vt = _mm(tag + "_in_av", xb, w_av, [F32, ("vt", t, tb, BF16)])
        conv0 = jnp.zeros((b, SUBLANES, B_CONV_DIM), F32)
        h0 = jnp.zeros((b, B_HEADS, B_HEAD_DIM, B_STATE), F32)
    else:
        pk, pv, pki, pconv, pssm = past
        p_len = pk.shape[1]
        av, av16 = _mm(tag + "_in_av", xb, w_av, [F32, BF16])
        k16 = jnp.concatenate([pk.reshape(b, p_len, A_WIDTH).astype(BF16), k16], axis=1)
        vt = _to_vt(jnp.concatenate([pv.reshape(b, p_len, A_WIDTH).astype(BF16), av16.reshape(b, t, A_WIDTH)], axis=1), tb)
        ik16 = jnp.concatenate([pki.astype(BF16), ik16], axis=1)
        conv0 = jnp.pad(pconv, ((0, 0), (SUBLANES - (CONV_W - 1), 0), (0, 0)))
        h0 = pssm
    xbc, tails = _mm_conv(tag + "_in_bxbc", xb, w_bxbc, conv0, conv_w, conv_b, t)
    conv_new = tails.reshape(b, -1, SUBLANES, B_CONV_DIM)[:, -1, SUBLANES - (CONV_W - 1):]
    a_out = _dsa(tag + "_dsa", iqt, iwt, ik16, aq.reshape(b, t, A_WIDTH), k16, vt,
                 ag.reshape(b, t, A_WIDTH), _bias_tables(t5_bias, tb), tb, p_len)
    b_out, ssm_new = _ssd(tag + "_ssd", bzg.reshape(b, t, B_WIDTH), xbc.reshape(b, t, B_CONV_DIM), bdt, dt_bias, a_log,
                          d_skip, norm_w, h0)
    wo = w_out.astype(BF16)
    y, y16 = _oproj_ln(tag + "_out0", [a_out.reshape(m, A_WIDTH), b_out.reshape(m, B_WIDTH)],
                       [wo[:A_WIDTH], wo[A_WIDTH:]], x2, ln_g, ln_b)
    state = (ak.reshape(b, t, A_HEADS, A_HEAD_DIM), av.reshape(b, t, A_HEADS, A_HEAD_DIM), ik, conv_new, ssm_new)
    return y.reshape(b, t, D_MODEL), y16, state


def _rope_rot_cols(w):
    half = QK_ROPE // 2
    return jnp.concatenate([-w[..., half:], w[..., :half]], axis=-1)


def _odd_weights(w_in, w_uq, w_ukv):
    w_cq = w_in[:, :Q_LORA]
    w_ckv = w_in[:, Q_LORA:Q_LORA + KV_LORA]
    w_kr = w_in[:, Q_LORA + KV_LORA:Q_LORA + KV_LORA + QK_ROPE]
    w_gate = w_in[:, Q_LORA + KV_LORA + QK_ROPE:]
    w_kr2 = jnp.concatenate([w_kr, _rope_rot_cols(w_kr)], axis=1)
    uq = w_uq.reshape(Q_LORA, C_HEADS, QK_NOPE + QK_ROPE) * (MLA_SCALE * LOG2E)
    uq_rope = uq[..., QK_NOPE:]
    uq2 = jnp.concatenate([uq[..., :QK_NOPE], uq_rope, _rope_rot_cols(uq_rope)], axis=-1).reshape(Q_LORA, C_HEADS * 2 * LANES)
    ukv = w_ukv.reshape(KV_LORA, C_HEADS, QK_NOPE + V_DIM)
    w_uk = ukv[..., :QK_NOPE].reshape(KV_LORA, C_HEADS * QK_NOPE)
    w_uv = ukv[..., QK_NOPE:].reshape(KV_LORA, C_HEADS * V_DIM)
    return [c.astype(BF16) for c in (w_cq, w_ckv, w_kr2, w_gate, uq2, w_uk, w_uv)]


def _rope_table(pos):
    half = QK_ROPE // 2
    inv = ROPE_THETA ** (-jnp.arange(half, dtype=F32) / half)
    ang = pos.astype(F32)[:, None] * inv[None, :]
    cos, sin = jnp.cos(ang), jnp.sin(ang)
    return jnp.concatenate([cos, cos, sin, sin], axis=1)


def _odd_layer(tag, x, x16, past, wts, q_norm_w, kv_norm_w, w_out, ln_g, ln_b, tb):
    b, t, _ = x.shape
    m = b * t
    w_cq, w_ckv, w_kr2, w_gate, w_uq2, w_uk, w_uv = wts
    p_len = 0 if past is None else past[0].shape[1]
    tab = _rope_table(p_len + jnp.arange(t, dtype=I32))
    (cq16,) = _mm_rms(tag + "_in_cq", x16, w_cq, q_norm_w, [BF16])
    ckv, ckv16 = _mm_rms(tag + "_in_ckv", x16, w_ckv, kv_norm_w, [F32, BF16])
    kr, kr16 = _mm_rope_k(tag + "_in_kr", x16, w_kr2, tab)
    (gate,) = _mm(tag + "_in_gate", x16, w_gate, [("silu", F32)])
    q = _mm_rope_q(tag + "_uq", cq16, w_uq2, tab)
    lat16 = ckv16.reshape(b, t, KV_LORA)
    kr16 = kr16.reshape(b, t, LANES)
    if past is not None:
        lat16 = jnp.concatenate([past[0].astype(BF16), lat16], axis=1)
        kr_past = jnp.pad(past[1], ((0, 0), (0, 0), (0, LANES - QK_ROPE))).astype(BF16)
        kr16 = jnp.concatenate([kr_past, kr16], axis=1)
    l = p_len + t
    lat2 = lat16.reshape(b * l, KV_LORA)
    (kn,) = _mm(tag + "_uk", lat2, w_uk, [BF16])
    if tb % LANES == 0:
        (vt,) = _mm(tag + "_uv", lat2, w_uv, [("vt", l, tb, BF16)])
    else:
        (v,) = _mm(tag + "_uv", lat2, w_uv, [BF16])
        vt = _to_vt(v.reshape(b, l, C_WIDTH), tb)
    o = _mla(tag + "_mla", q.reshape(b, t, C_HEADS * 2 * LANES), kn.reshape(b, l, C_HEADS * QK_NOPE), kr16,
             vt, gate.reshape(b, t, C_WIDTH), tb, p_len)
    y, _ = _oproj_ln(tag + "_out1", [o.reshape(m, C_WIDTH)], [w_out.astype(BF16)], x.reshape(m, D_MODEL), ln_g, ln_b)
    return y.reshape(b, t, D_MODEL), (ckv.reshape(b, t, KV_LORA), kr.reshape(b, t, QK_ROPE))


def kernel(x_prompt, x_sample, cache_a_k, cache_a_v, cache_a_kidx, state_b_conv, state_b_ssm, cache_c_latent, cache_c_krope, t5_bias, w_in0, w_out0, conv_w, conv_b, dt_bias, a_log, d_skip, ssm_norm_w, ln0_g, ln0_b, w_in1, q_norm_w, kv_norm_w, w_uq, w_ukv, w_out1, ln1_g, ln1_b):
    tb_prompt = 256
    tb_sample = CHUNK
    ew = _even_weights(w_in0[0])
    eprm = (w_out0[0], conv_w[0], conv_b[0], dt_bias[0], a_log[0], d_skip[0], ssm_norm_w[0], ln0_g[0], ln0_b[0], t5_bias)
    yp, yp16, st_p = _even_layer("p0", x_prompt, None, ew, *eprm, tb_prompt)
    past = (cache_a_k[0], cache_a_v[0], cache_a_kidx[0], state_b_conv[0], state_b_ssm[0])
    ys, ys16, st_s = _even_layer("s0", x_sample, past, ew, *eprm, tb_sample)
    ow = _odd_weights(w_in1[0], w_uq[0], w_ukv[0])
    oprm = (q_norm_w[0], kv_norm_w[0], w_out1[0], ln1_g[0], ln1_b[0])
    yp, od_p = _odd_layer("p1", yp, yp16, None, ow, *oprm, tb_prompt)
    ys, od_s = _odd_layer("s1", ys, ys16, (cache_c_latent[0], cache_c_krope[0]), ow, *oprm, tb_sample)
    e = lambda a: a[None]
    return (yp, ys, e(st_p[0]), e(st_s[0]), e(st_p[1]), e(st_s[1]), e(st_p[2]), e(st_s[2]),
            e(st_p[3]), e(st_s[3]), e(st_p[4]), e(st_s[4]), e(od_p[0]), e(od_s[0]), e(od_p[1]), e(od_s[1]))
```

```python
import functools
import math

import jax
import jax.numpy as jnp
from jax import lax
from jax.experimental import pallas as pl
from jax.experimental.pallas import tpu as pltpu

F32 = jnp.float32
BF16 = jnp.bfloat16
I32 = jnp.int32
I16 = jnp.int16

D_MODEL = 2048
CHUNK = 64
A_HEADS = 8
A_HEAD_DIM = 128
A_WIDTH = A_HEADS * A_HEAD_DIM
IDX_HEADS = 16
IDX_DIM = 64
IDX_W_SCALE = (IDX_HEADS * IDX_DIM) ** -0.5
TOPK_MAX = 256
N_BUCKETS = 32
MAX_DISTANCE = 128
B_HEAD_DIM = 64
B_WIDTH = D_MODEL
B_HEADS = B_WIDTH // B_HEAD_DIM
B_GROUPS = 4
B_HPG = B_HEADS // B_GROUPS
B_STATE = 128
CONV_W = 4
B_CONV_DIM = B_WIDTH + 2 * B_GROUPS * B_STATE
C_HEADS = 16
Q_LORA = 512
KV_LORA = 512
QK_NOPE = 128
QK_ROPE = 64
V_DIM = 128
C_WIDTH = C_HEADS * V_DIM
ROPE_THETA = 10000.0
MLA_SCALE = (QK_NOPE + QK_ROPE) ** -0.5
DEPTH = 2
ALPHA = (2 * DEPTH) ** 0.25
EPS = 1e-5

LANES = 128
SUBLANES = 8
VT_ROWS = LANES + 2 * SUBLANES
VMEM_LIMIT = 56 * 1024 * 1024
LOG2E = math.log2(math.e)
INT_MIN = -(2 ** 31)
INT_MAX = 2 ** 31 - 1
KEY_NEG_INF = (0xFF800000 ^ 0x7FFFFFFF) - (1 << 32)
HALF_BITS = 16
HALF_MASK = (1 << HALF_BITS) - 1
HALF_OFFSET = 1 << (HALF_BITS - 1)
HALF_MIN = -HALF_OFFSET
NEG_BIG = -1e30


def _params(sem):
    return pltpu.CompilerParams(dimension_semantics=sem, vmem_limit_bytes=VMEM_LIMIT)


def _dot_nt(a, b):
    return lax.dot_general(a, b, (((1,), (1,)), ((), ())), preferred_element_type=F32)


def _silu(x):
    return x * (1.0 / (1.0 + jnp.exp(-x)))


def _split3(v):
    hi = v.astype(BF16)
    r = v - hi.astype(F32)
    mid = r.astype(BF16)
    return hi, mid, (r - mid.astype(F32)).astype(BF16)


def _dot01(a, b, split_rhs):
    if split_rhs:
        parts = [jnp.dot(a, t, preferred_element_type=F32) for t in _split3(b)]
    else:
        parts = [jnp.dot(t, b, preferred_element_type=F32) for t in _split3(a)]
    return parts[0] + parts[1] + parts[2]


def _mm_body(a_ref, w_ref, *o_refs, kinds):
    a = a_ref[...]
    if a.dtype != w_ref.dtype:
        a = a.astype(w_ref.dtype)
    acc = jnp.dot(a, w_ref[...], preferred_element_type=F32)
    tm, tn = acc.shape
    for kind, o in zip(kinds, o_refs):
        if kind[0] == "plain":
            o[...] = acc.astype(o.dtype)
        elif kind[0] == "silu":
            o[...] = _silu(acc).astype(o.dtype)
        elif kind[0] == "acopy":
            @pl.when(pl.program_id(1) == 0)
            def _():
                o[...] = a
        elif kind[0] == "vt":
            tb = kind[2]
            for hh in range(tn // LANES):
                for kt in range(tm // tb):
                    o[0, hh, kt, 0:LANES, :] = acc[kt * tb:(kt + 1) * tb, hh * LANES:(hh + 1) * LANES].T.astype(o.dtype)
                    o[0, hh, kt, LANES:VT_ROWS, :] = _ones_rows(tb)
        elif kind[0] == "heads":
            hd = kind[2]
            for hh in range(tn // hd):
                o[0, hh] = acc[:, hh * hd:(hh + 1) * hd].astype(o.dtype)


def _mm_rms_body(a_ref, w_ref, g_ref, *o_refs):
    acc = jnp.dot(a_ref[...], w_ref[...], preferred_element_type=F32)
    y = acc * lax.rsqrt(jnp.mean(acc * acc, axis=-1, keepdims=True) + EPS) * g_ref[...]
    for o in o_refs:
        o[...] = y.astype(o.dtype)


def _rope_half(t):
    return t + pltpu.roll(t, QK_ROPE, 1)


def _mm_rope_k_body(a_ref, w_ref, tab_ref, o32_ref, o16_ref):
    acc = jnp.dot(a_ref[...], w_ref[...], preferred_element_type=F32)
    r = _rope_half(acc * tab_ref[...])
    lane = lax.broadcasted_iota(I32, r.shape, 1)
    o32_ref[...] = r[:, :QK_ROPE]
    o16_ref[...] = jnp.where(lane < QK_ROPE, r, 0.0).astype(o16_ref.dtype)


def _mm_rope_q_body(a_ref, w_ref, tab_ref, o_ref, *, heads):
    acc = jnp.dot(a_ref[...], w_ref[...], preferred_element_type=F32)
    tab = tab_ref[...]
    lane = lax.broadcasted_iota(I32, tab.shape, 1)
    for h in range(heads):
        base = h * 2 * LANES
        o_ref[:, base:base + LANES] = acc[:, base:base + LANES].astype(o_ref.dtype)
        r = _rope_half(acc[:, base + LANES:base + 2 * LANES] * tab)
        o_ref[:, base + LANES:base + 2 * LANES] = jnp.where(lane < QK_ROPE, r, 0.0).astype(o_ref.dtype)


def _mm_call(name, body, a, w, extra, extra_specs, out_cols, out_dtypes, tm, tn):
    m, k = a.shape
    n = w.shape[1]
    tm = math.gcd(tm, m)
    tn = min(tn, n)
    assert tm % SUBLANES == 0 and n % tn == 0, (m, n, tm, tn)
    oc = [tn if c is None else c for c in out_cols]
    return pl.pallas_call(
        body,
        grid=(m // tm, n // tn),
        in_specs=[pl.BlockSpec((tm, k), lambda i, j: (i, 0)),
                  pl.BlockSpec((k, tn), lambda i, j: (0, j))] + extra_specs(tm, tn),
        out_specs=[pl.BlockSpec((tm, c), lambda i, j: (i, j)) for c in oc],
        out_shape=[jax.ShapeDtypeStruct((m, (n // tn) * c), d) for c, d in zip(oc, out_dtypes)],
        compiler_params=_params(("parallel", "parallel")),
        name=name,
    )(a, w, *extra)


def _mm(name, a, w, outs, tm=1024, tn=512):
    m, k = a.shape
    n = w.shape[1]
    tm = math.gcd(tm, m, *[o[1] for o in outs if isinstance(o, tuple) and o[0] in ("vt", "heads")])
    tn = min(tn, n)
    assert tm % SUBLANES == 0 and n % tn == 0, (m, n, tm, tn)
    kinds, specs, shapes = [], [], []
    for o in outs:
        o = o if isinstance(o, tuple) else ("plain", o)
        if o[0] in ("plain", "silu"):
            kinds.append((o[0],))
            specs.append(pl.BlockSpec((tm, tn), lambda i, j: (i, j)))
            shapes.append(jax.ShapeDtypeStruct((m, n), o[1]))
        elif o[0] == "acopy":
            kinds.append(("acopy",))
            specs.append(pl.BlockSpec((tm, k), lambda i, j: (i, 0)))
            shapes.append(jax.ShapeDtypeStruct((m, k), w.dtype))
        elif o[0] == "vt":
            _, t, tb, dt = o
            assert t % tm == 0 and tm % tb == 0 and tn % LANES == 0
            per = t // tm
            kinds.append(("vt", t, tb))
            specs.append(pl.BlockSpec((1, tn // LANES, tm // tb, VT_ROWS, tb),
                                      lambda i, j: (i // per, j, i % per, 0, 0)))
            shapes.append(jax.ShapeDtypeStruct((m // t, n // LANES, t // tb, VT_ROWS, tb), dt))
        elif o[0] == "heads":
            _, t, hd, dt = o
            assert t % tm == 0 and tn % hd == 0
            per = t // tm
            kinds.append(("heads", t, hd))
            specs.append(pl.BlockSpec((1, tn // hd, tm, hd), lambda i, j: (i // per, j, i % per, 0)))
            shapes.append(jax.ShapeDtypeStruct((m // t, n // hd, t, hd), dt))
        else:
            raise ValueError(o)
    return pl.pallas_call(
        functools.partial(_mm_body, kinds=tuple(kinds)),
        grid=(m // tm, n // tn),
        in_specs=[pl.BlockSpec((tm, k), lambda i, j: (i, 0)), pl.BlockSpec((k, tn), lambda i, j: (0, j))],
        out_specs=specs,
        out_shape=shapes,
        compiler_params=_params(("parallel", "arbitrary")),
        name=name,
    )(a, w)


def _mm_conv_body(a_ref, w_ref, st_ref, cw_ref, cb_ref, o_ref, tail_ref, ext_scr, prev_scr, *, tiles_per_seq):
    i, j = pl.program_id(0), pl.program_id(1)
    tm, tn = o_ref.shape

    @pl.when(i % tiles_per_seq == 0)
    def _():
        ext_scr[0:SUBLANES, :] = st_ref[0]

    @pl.when(i % tiles_per_seq != 0)
    def _():
        ext_scr[0:SUBLANES, :] = prev_scr[j]

    cw_ = min(tn, 2 * LANES)
    for c in range(tn // cw_):
        cs = slice(c * cw_, (c + 1) * cw_)
        ext_scr[SUBLANES:SUBLANES + tm, cs] = jnp.dot(a_ref[...], w_ref[:, cs], preferred_element_type=F32)
        conv = cb_ref[:, cs]
        for tap in range(CONV_W):
            lo = SUBLANES - (CONV_W - 1) + tap
            conv = conv + ext_scr[lo:lo + tm, cs] * cw_ref[tap:tap + 1, cs]
        o_ref[:, cs] = _silu(conv).astype(o_ref.dtype)
    tail = ext_scr[tm:tm + SUBLANES, :]
    prev_scr[j] = tail
    tail_ref[0] = tail


def _mm_conv(name, a, w, state, conv_w, conv_b, t, tm=1024, tn=512):
    m, k = a.shape
    n = w.shape[1]
    tm = math.gcd(tm, t)
    tn = min(tn, n)
    assert n % tn == 0 and tm % SUBLANES == 0
    per = t // tm
    return pl.pallas_call(
        functools.partial(_mm_conv_body, tiles_per_seq=per),
        grid=(m // tm, n // tn),
        in_specs=[pl.BlockSpec((tm, k), lambda i, j: (i, 0)),
                  pl.BlockSpec((k, tn), lambda i, j: (0, j)),
                  pl.BlockSpec((1, SUBLANES, tn), lambda i, j: (i // per, 0, j)),
                  pl.BlockSpec((CONV_W, tn), lambda i, j: (0, j)),
                  pl.BlockSpec((1, tn), lambda i, j: (0, j))],
        out_specs=[pl.BlockSpec((tm, tn), lambda i, j: (i, j)),
                   pl.BlockSpec((1, SUBLANES, tn), lambda i, j: (i, 0, j))],
        out_shape=[jax.ShapeDtypeStruct((m, n), F32), jax.ShapeDtypeStruct((m // tm, SUBLANES, n), F32)],
        scratch_shapes=[pltpu.VMEM((tm + SUBLANES, tn), F32), pltpu.VMEM((n // tn, SUBLANES, tn), F32)],
        compiler_params=_params(("arbitrary", "arbitrary")),
        name=name,
    )(a, w, state, conv_w, conv_b.reshape(1, n))


def _mm_rms(name, a, w, g, out_dtypes, tm=1024):
    n = w.shape[1]
    return _mm_call(name, _mm_rms_body, a, w, [g.reshape(1, n)],
                    lambda tm_, tn_: [pl.BlockSpec((1, n), lambda i, j: (0, 0))],
                    [None] * len(out_dtypes), out_dtypes, tm, n)


def _tab_spec(t_rows):
    def spec(tm, tn):
        nt = t_rows // tm
        return [pl.BlockSpec((tm, LANES), lambda i, j: (i % nt, 0))]
    return spec


def _mm_rope_k(name, a, w, tab, tm=1024):
    tm = min(tm, tab.shape[0])
    return _mm_call(name, _mm_rope_k_body, a, w, [tab], _tab_spec(tab.shape[0]), [QK_ROPE, LANES], [F32, BF16],
                    tm, LANES)


def _mm_rope_q(name, a, w, tab, tm=1024, heads_per_block=2):
    tm = min(tm, tab.shape[0])
    body = functools.partial(_mm_rope_q_body, heads=heads_per_block)
    return _mm_call(name, body, a, w, [tab], _tab_spec(tab.shape[0]), [None], [BF16], tm,
                    heads_per_block * 2 * LANES)[0]


def _oproj_body(*refs, n_parts):
    parts = refs[:n_parts]
    ws = refs[n_parts:2 * n_parts]
    x_ref, g_ref, b_ref, o32_ref, o16_ref = refs[2 * n_parts:]
    acc = ALPHA * x_ref[...]
    for p, w in zip(parts, ws):
        acc = acc + jnp.dot(p[...], w[...], preferred_element_type=F32)
    mu = jnp.mean(acc, axis=-1, keepdims=True)
    d = acc - mu
    var = jnp.mean(d * d, axis=-1, keepdims=True)
    y = d * lax.rsqrt(var + EPS) * g_ref[...] + b_ref[...]
    o32_ref[...] = y
    o16_ref[...] = y.astype(o16_ref.dtype)


def _oproj_ln(name, parts, ws, x, g, b, tm=256):
    m, n = x.shape
    tm = min(tm, m)
    np_ = len(parts)
    const = lambda i: (0, 0)
    return pl.pallas_call(
        functools.partial(_oproj_body, n_parts=np_),
        grid=(m // tm,),
        in_specs=[pl.BlockSpec((tm, p.shape[1]), lambda i: (i, 0)) for p in parts]
        + [pl.BlockSpec(w.shape, const, pipeline_mode=pl.Buffered(1)) for w in ws]
        + [pl.BlockSpec((tm, n), lambda i: (i, 0)),
           pl.BlockSpec((1, n), const), pl.BlockSpec((1, n), const)],
        out_specs=[pl.BlockSpec((tm, n), lambda i: (i, 0))] * 2,
        out_shape=[jax.ShapeDtypeStruct((m, n), F32), jax.ShapeDtypeStruct((m, n), BF16)],
        compiler_params=_params(("parallel",)),
        name=name,
    )(*parts, *ws, x, g.reshape(1, n), b.reshape(1, n))


def _ones_rows(tb):
    return jnp.ones((VT_ROWS - LANES, tb), BF16)


def _to_vt(v, tb):
    b, l, hd = v.shape
    h = hd // LANES
    vt = jnp.transpose(v.reshape(b, l // tb, tb, h, LANES), (0, 3, 1, 4, 2))
    return jnp.concatenate([vt, jnp.ones(vt.shape[:3] + (VT_ROWS - LANES, tb), vt.dtype)], axis=3)


def _softmax_init(m_scr, acc_scr, heads, tb):
    for h in range(heads):
        m_scr[h] = jnp.full((1, tb), NEG_BIG, F32)
        acc_scr[h] = jnp.zeros(acc_scr.shape[1:], F32)


def _softmax_step(s, vts, m_scr, acc_scr, h):
    m_prev = m_scr[h]
    m_new = jnp.maximum(m_prev, jnp.max(s, axis=0, keepdims=True))
    alpha = jnp.exp2(m_prev - m_new)
    p = jnp.exp2(s - m_new)
    pb = p.astype(BF16)
    tk = s.shape[0] // len(vts)
    acc = alpha * acc_scr[h]
    for i, vt in enumerate(vts):
        acc = acc + jnp.dot(vt, pb[i * tk:(i + 1) * tk], preferred_element_type=F32)
    acc_scr[h] = acc
    m_scr[h] = m_new


def _pipelined_blocks(n_blocks, logits_fn, consume_fn, buf_a, buf_b):
    @pl.when(n_blocks > 0)
    def _():
        logits_fn(0, buf_a)

    def two_blocks(u, carry):
        i = 2 * u
        logits_fn(i + 1, buf_b)
        consume_fn(i, buf_a)
        logits_fn(jnp.minimum(i + 2, n_blocks - 1), buf_a)
        consume_fn(i + 1, buf_b)
        return carry

    lax.fori_loop(0, n_blocks // 2, two_blocks, 0)

    @pl.when(n_blocks % 2 == 1)
    def _():
        consume_fn(n_blocks - 1, buf_a)


def _softmax_finish(acc_scr, g_ref, o_ref, heads):
    for h in range(heads):
        hs = slice(h * LANES, (h + 1) * LANES)
        o = (acc_scr[h, 0:LANES, :] / acc_scr[h, LANES:LANES + 1, :]).T
        o_ref[0, :, hs] = (o * g_ref[0, :, hs]).astype(o_ref.dtype)


def _t5_bucket(rel):
    half = N_BUCKETS // 2
    max_exact = half // 2
    ret = jnp.where(rel < 0, half, 0)
    n = jnp.abs(rel)
    nf = jnp.maximum(n, 1).astype(F32)
    large = max_exact + (jnp.log(nf / max_exact) / math.log(MAX_DISTANCE / max_exact) * (half - max_exact)).astype(jnp.int32)
    large = jnp.minimum(large, half - 1)
    return ret + jnp.where(n < max_exact, n, large)


def _num_special_tiles(tb):
    return (MAX_DISTANCE - 2 + 2 * tb) // tb


def _bias_tables(t5_bias, tb):
    ns = _num_special_tiles(tb)
    s = jnp.arange(tb, dtype=I32)[:, None]
    t = jnp.arange(tb, dtype=I32)[None, :]
    bucket = jnp.stack([_t5_bucket(tb * d + t - s) for d in range(ns)])
    far = t5_bias[_t5_bucket(jnp.int32(tb * ns))]
    out = jnp.zeros((ns, A_HEADS, tb, tb), F32)
    for k in range(N_BUCKETS):
        out = jnp.where(bucket[:, None] == k, t5_bias[k][None, :, None, None], out)
    return (out - far[None, :, None, None]) * LOG2E


def _dsa_body(iq_ref, iwt_ref, ik_ref, q_ref, k_ref, vt_ref, g_ref, bias_ref, o_ref,
              key_scr, hi_scr, lo_scr, mb_scr, x_scr, m_scr, acc_scr, qka_scr, qkb_scr, *,
              tb, qt0, ns, n_sel, idx_bits):
    qt = pl.program_id(1) + qt0
    nkv = qt + 1
    krow = lax.broadcasted_iota(I32, (tb, tb), 0)
    qcol = lax.broadcasted_iota(I32, (tb, tb), 1)
    diag_vis = (krow // CHUNK) <= (qcol // CHUNK)

    wt = iwt_ref[0] * IDX_W_SCALE

    def idx_tile(j, carry):
        kt = ik_ref[0, pl.ds(pl.multiple_of(j * tb, tb), tb), :]
        acc = jnp.zeros((tb, tb), F32)
        for h in range(IDX_HEADS):
            acc = acc + jnp.maximum(_dot_nt(kt, iq_ref[0, h]), 0.0) * wt[h:h + 1, :]
        acc = jnp.where(acc == 0.0, 0.0, acc)
        s = jnp.where(jnp.logical_or(diag_vis, j < qt), acc, -jnp.inf)
        bits = pltpu.bitcast(s, I32)
        key = bits ^ ((bits >> 31) & 0x7FFFFFFF)
        key_scr[j] = key
        hi_scr[j] = (key >> HALF_BITS).astype(I16)
        lo_scr[j] = ((key & HALF_MASK) - HALF_OFFSET).astype(I16)
        return carry

    lax.fori_loop(0, nkv, idx_tile, 0)

    def count(pred):
        def body(j, c):
            f = jnp.where(pred(key_scr[j], j), 1.0, 0.0)
            return c + jnp.sum(f.reshape(tb // SUBLANES, SUBLANES, tb), axis=0)
        c = lax.fori_loop(0, nkv, body, jnp.zeros((SUBLANES, tb), F32))
        return jnp.sum(c, axis=0, keepdims=True)

    pack = 2 * SUBLANES
    one16, zero16 = jnp.ones((), I16), jnp.zeros((), I16)

    def count16(ref, cand, strict):
        c16 = cand.astype(I16)

        def body(j, c):
            v = ref[j]
            f = jnp.where(v > c16 if strict else v >= c16, one16, zero16).reshape(tb // pack, pack, tb)
            part = f[0]
            for r in range(1, tb // pack):
                part = part + f[r]
            return c + part.astype(I32)
        c = lax.fori_loop(0, nkv, body, jnp.zeros((pack, tb), I32))
        return jnp.sum(c, axis=0, keepdims=True)

    def search16(ref, target):
        v0 = jnp.where(count16(ref, jnp.zeros((1, tb), I32), False) >= target, 0, HALF_MIN).astype(I32)

        def bit_body(i, v):
            cand = v + jnp.left_shift(jnp.int32(1), HALF_BITS - 2 - i)
            return jnp.where(count16(ref, cand, False) >= target, cand, v)
        return lax.fori_loop(0, HALF_BITS - 1, bit_body, v0)

    thr_hi = search16(hi_scr, n_sel)
    need_lo = n_sel - count16(hi_scr, thr_hi, True)
    hi16 = thr_hi.astype(I16)

    def band_tile(j, carry):
        hi_scr[j] = jnp.where(hi_scr[j] == hi16, lo_scr[j], jnp.full((), HALF_MIN, I16))
        return carry

    lax.fori_loop(0, nkv, band_tile, 0)
    thr_lo = search16(hi_scr, need_lo)
    thr = thr_hi * (HALF_MASK + 1) + (thr_lo + HALF_OFFSET)
    nsel = float(n_sel)

    need = nsel - count(lambda kt, j: kt > thr)
    excess = jnp.logical_and(count(lambda kt, j: kt == thr) > need, thr > KEY_NEG_INF)
    x_scr[...] = jnp.full((1, tb), INT_MAX, I32)

    @pl.when(jnp.max(jnp.where(excess, 1.0, 0.0)) > 0.0)
    def _():
        def xbit(i, x):
            cand = x + jnp.left_shift(jnp.int32(1), idx_bits - 1 - i)
            c = count(lambda kt, j: jnp.logical_and(kt == thr, krow + j * tb < cand))
            return jnp.where(c < need, cand, x)
        x = lax.fori_loop(0, idx_bits, xbit, jnp.zeros((1, tb), I32))
        x_scr[...] = jnp.where(excess, x, INT_MAX)

    xcut = x_scr[...]

    def mask_tile(j, carry):
        kt = key_scr[j]
        tie = jnp.logical_and(kt == thr, krow + j * tb <= xcut)
        sel = jnp.logical_and(jnp.logical_or(kt > thr, tie), kt != KEY_NEG_INF)
        mb_scr[j] = jnp.where(sel, 0.0, -jnp.inf)
        return carry

    lax.fori_loop(0, nkv, mask_tile, 0)

    _softmax_init(m_scr, acc_scr, A_HEADS, tb)

    def attn_tiles(j, nt, biased):
        rows = pl.ds(pl.multiple_of(j * tb, tb), nt * tb)
        mb = mb_scr[pl.ds(j, nt)].reshape(nt * tb, tb)
        qk = [_dot_nt(k_ref[0, rows, h * A_HEAD_DIM:(h + 1) * A_HEAD_DIM],
                      q_ref[0, :, h * A_HEAD_DIM:(h + 1) * A_HEAD_DIM]) for h in range(A_HEADS)]
        for h in range(A_HEADS):
            s = mb + qk[h]
            if biased:
                s = bias_ref[qt - j, h] + s
            _softmax_step(s, [vt_ref[0, h, j + i] for i in range(nt)], m_scr, acc_scr, h)

    def pair_logits(p, buf):
        rows = pl.ds(pl.multiple_of(p * 2 * tb, 2 * tb), 2 * tb)
        for h in range(A_HEADS):
            hs = slice(h * A_HEAD_DIM, (h + 1) * A_HEAD_DIM)
            buf[h] = _dot_nt(k_ref[0, rows, hs], q_ref[0, :, hs])

    def pair_softmax(p, buf):
        mb = mb_scr[pl.ds(2 * p, 2)].reshape(2 * tb, tb)
        for h in range(A_HEADS):
            _softmax_step(mb + buf[h], [vt_ref[0, h, 2 * p], vt_ref[0, h, 2 * p + 1]], m_scr, acc_scr, h)

    def far_tile(j, carry):
        attn_tiles(j, 1, False)
        return carry

    def near_tile(j, carry):
        attn_tiles(j, 1, True)
        return carry

    n_far = jnp.maximum(qt - (ns - 1), 0)
    n_pair = n_far // 2
    _pipelined_blocks(n_pair, pair_logits, pair_softmax, qka_scr, qkb_scr)
    lax.fori_loop(2 * n_pair, n_far, far_tile, 0)
    lax.fori_loop(n_far, nkv, near_tile, 0)
    _softmax_finish(acc_scr, g_ref, o_ref, A_HEADS)


def _dsa(name, iq, iwt, ik, q, k, vt, gate, bias_tabs, tb, past_len):
    b, t, aw = q.shape
    l = k.shape[1]
    assert t % tb == 0 and l % tb == 0 and past_len % tb == 0 and l == past_len + t
    nq, nkv = t // tb, l // tb
    ns = _num_special_tiles(tb)
    n_sel = min(TOPK_MAX, l // 4)
    body = functools.partial(_dsa_body, tb=tb, qt0=past_len // tb, ns=ns, n_sel=n_sel,
                             idx_bits=max(1, (l - 1).bit_length()))
    once = pl.Buffered(1)
    return pl.pallas_call(
        body,
        grid=(b, nq),
        in_specs=[
            pl.BlockSpec((1, IDX_HEADS, tb, IDX_DIM), lambda bi, qi: (bi, 0, qi, 0)),
            pl.BlockSpec((1, IDX_HEADS, tb), lambda bi, qi: (bi, 0, qi)),
            pl.BlockSpec((1, l, IDX_DIM), lambda bi, qi: (bi, 0, 0), pipeline_mode=once),
            pl.BlockSpec((1, tb, aw), lambda bi, qi: (bi, qi, 0)),
            pl.BlockSpec((1, l, aw), lambda bi, qi: (bi, 0, 0), pipeline_mode=once),
            pl.BlockSpec((1, A_HEADS, nkv, VT_ROWS, tb), lambda bi, qi: (bi, 0, 0, 0, 0), pipeline_mode=once),
            pl.BlockSpec((1, tb, aw), lambda bi, qi: (bi, qi, 0)),
            pl.BlockSpec(bias_tabs.shape, lambda bi, qi: (0, 0, 0, 0), pipeline_mode=once),
        ],
        out_specs=pl.BlockSpec((1, tb, aw), lambda bi, qi: (bi, qi, 0)),
        out_shape=jax.ShapeDtypeStruct((b, t, aw), BF16),
        scratch_shapes=[
            pltpu.VMEM((nkv, tb, tb), I32),
            pltpu.VMEM((nkv, tb, tb), I16),
            pltpu.VMEM((nkv, tb, tb), I16),
            pltpu.VMEM((nkv, tb, tb), F32),
            pltpu.VMEM((1, tb), I32),
            pltpu.VMEM((A_HEADS, 1, tb), F32),
            pltpu.VMEM((A_HEADS, VT_ROWS, tb), F32),
            pltpu.VMEM((A_HEADS, 2 * tb, tb), F32),
            pltpu.VMEM((A_HEADS, 2 * tb, tb), F32),
        ],
        compiler_params=_params(("parallel", "parallel")),
        name=name,
    )(iq, iwt, ik, q, k, vt, gate, bias_tabs)


def _ssd_body(zg_ref, xbc_ref, dt_ref, dtt_ref, dtb_ref, dtbt_ref, alog_ref, alogt_ref,
              dsk_ref, nw_ref, exp_ref, h0_ref, y_ref, hout_ref, h_scr, yi_scr):
    c = pl.program_id(1)
    l = CHUNK
    gw = B_WIDTH // B_GROUPS

    @pl.when(c == 0)
    def _():
        h_scr[...] = h0_ref[0]

    xbc = xbc_ref[0]
    xs = xbc[:, :B_WIDTH]

    dt = jax.nn.softplus(dt_ref[0, 0] + dtb_ref[...])
    dtt = jax.nn.softplus(dtt_ref[0, 0] + dtbt_ref[...])
    a = -jnp.exp(alog_ref[...])
    at = -jnp.exp(alogt_ref[...])
    ti = lax.broadcasted_iota(I32, (l, l), 0)
    si = lax.broadcasted_iota(I32, (l, l), 1)
    causal = si <= ti
    lower = jnp.where(causal, 1.0, 0.0).astype(BF16)
    upper = jnp.where(ti <= si, 1.0, 0.0).astype(BF16)
    acum = _dot01(lower, dt * a, split_rhs=True)
    acumt = _dot01(dtt * at, upper, split_rhs=False)
    a_last = acum[l - 1:l, :]
    expand = exp_ref[...]
    e_full = _dot01(jnp.exp(acum), expand, split_rhs=False)
    tail_full = _dot01(jnp.exp(a_last - acum) * dt, expand, split_rhs=False)
    xt = (xs * tail_full).astype(BF16)
    xs16 = xs.astype(BF16)
    lane = lax.broadcasted_iota(I32, (l, 2 * B_HEAD_DIM), 1)

    for g in range(B_GROUPS):
        bm = xbc[:, B_WIDTH + g * B_STATE:B_WIDTH + (g + 1) * B_STATE].astype(BF16)
        cm = xbc[:, B_WIDTH + (B_GROUPS + g) * B_STATE:B_WIDTH + (B_GROUPS + g + 1) * B_STATE].astype(BF16)
        cb = _dot_nt(cm, bm)
        gs = slice(g * gw, (g + 1) * gw)
        hg = h_scr[g]
        y_state = jnp.dot(cm, hg.astype(BF16), preferred_element_type=F32) * e_full[:, gs]
        for pr in range(B_HPG // 2):
            ws = []
            for r in (g * B_HPG + 2 * pr, g * B_HPG + 2 * pr + 1):
                seg = acum[:, r:r + 1] - acumt[r:r + 1, :]
                decay = jnp.exp(jnp.where(causal, seg, -jnp.inf))
                ws.append((cb * decay * dtt[r:r + 1, :]).astype(BF16))
            c0 = g * gw + pr * 2 * B_HEAD_DIM
            xp = xs16[:, c0:c0 + 2 * B_HEAD_DIM]
            y0 = jnp.dot(ws[0], xp, preferred_element_type=F32)
            y1 = jnp.dot(ws[1], xp, preferred_element_type=F32)
            yi_scr[:, c0:c0 + 2 * B_HEAD_DIM] = jnp.where(lane < B_HEAD_DIM, y0, y1)
        yi_scr[:, gs] = yi_scr[:, gs] + y_state
        upd = lax.dot_general(bm, xt[:, gs], (((0,), (0,)), ((), ())), preferred_element_type=F32)
        h_scr[g] = hg * e_full[l - 1:l, gs] + upd

    y = (yi_scr[...] + dsk_ref[...] * xs) * zg_ref[0]
    for g in range(B_GROUPS):
        gs = slice(g * gw, (g + 1) * gw)
        yg = y[:, gs]
        yg = yg * lax.rsqrt(jnp.mean(yg * yg, axis=-1, keepdims=True) + EPS)
        y_ref[0, :, gs] = (yg * nw_ref[:, gs]).astype(y_ref.dtype)

    @pl.when(c == pl.num_programs(1) - 1)
    def _():
        hout_ref[0] = h_scr[...]


def _ssd(name, zg, xbc, dt_raw, dt_bias, a_log, d_skip, norm_w, h0):
    b, t, _ = zg.shape
    nc = t // CHUNK
    gw = B_WIDTH // B_GROUPS
    dt4 = dt_raw.reshape(b, nc, CHUNK, B_HEADS)
    dtt4 = jnp.swapaxes(dt4, 2, 3)
    h0t = jnp.transpose(h0.reshape(b, B_GROUPS, B_HPG, B_HEAD_DIM, B_STATE), (0, 1, 4, 2, 3)).reshape(b, B_GROUPS, B_STATE, gw)
    expand = jnp.repeat(jnp.eye(B_HEADS, dtype=BF16), B_HEAD_DIM, axis=1)
    dsk = jnp.repeat(d_skip, B_HEAD_DIM).reshape(1, B_WIDTH)
    row = lambda v: v.reshape(1, -1)
    colv = lambda v: v.reshape(-1, 1)
    const2 = lambda bi, ci: (0, 0)
    y, hout = pl.pallas_call(
        _ssd_body,
        grid=(b, nc),
        in_specs=[
            pl.BlockSpec((1, CHUNK, B_WIDTH), lambda bi, ci: (bi, ci, 0)),
            pl.BlockSpec((1, CHUNK, B_CONV_DIM), lambda bi, ci: (bi, ci, 0)),
            pl.BlockSpec((1, 1, CHUNK, B_HEADS), lambda bi, ci: (bi, ci, 0, 0)),
            pl.BlockSpec((1, 1, B_HEADS, CHUNK), lambda bi, ci: (bi, ci, 0, 0)),
            pl.BlockSpec((1, B_HEADS), const2),
            pl.BlockSpec((B_HEADS, 1), const2),
            pl.BlockSpec((1, B_HEADS), const2),
            pl.BlockSpec((B_HEADS, 1), const2),
            pl.BlockSpec((1, B_WIDTH), const2),
            pl.BlockSpec((1, B_WIDTH), const2),
            pl.BlockSpec((B_HEADS, B_WIDTH), const2),
            pl.BlockSpec((1, B_GROUPS, B_STATE, gw), lambda bi, ci: (bi, 0, 0, 0)),
        ],
        out_specs=[pl.BlockSpec((1, CHUNK, B_WIDTH), lambda bi, ci: (bi, ci, 0)),
                   pl.BlockSpec((1, B_GROUPS, B_STATE, gw), lambda bi, ci: (bi, 0, 0, 0))],
        out_shape=[jax.ShapeDtypeStruct((b, t, B_WIDTH), BF16),
                   jax.ShapeDtypeStruct((b, B_GROUPS, B_STATE, gw), F32)],
        scratch_shapes=[pltpu.VMEM((B_GROUPS, B_STATE, gw), F32),
                        pltpu.VMEM((CHUNK, B_WIDTH), F32)],
        compiler_params=_params(("parallel", "arbitrary")),
        name=name,
    )(zg, xbc, dt4, dtt4, row(dt_bias), colv(dt_bias), row(a_log), colv(a_log),
      dsk, row(norm_w), expand, h0t)
    hnew = jnp.transpose(hout.reshape(b, B_GROUPS, B_STATE, B_HPG, B_HEAD_DIM), (0, 1, 3, 4, 2))
    return y, hnew.reshape(b, B_HEADS, B_HEAD_DIM, B_STATE)


def _mla_body(q_ref, kn_ref, kr_ref, vt_ref, g_ref, o_ref, m_scr, acc_scr, qka_scr, qkb_scr, *,
              tb, qt0, heads):
    qt = pl.program_id(2) + qt0
    _softmax_init(m_scr, acc_scr, heads, tb)
    krow = lax.broadcasted_iota(I32, (tb, tb), 0)
    qcol = lax.broadcasted_iota(I32, (tb, tb), 1)
    diag_vis = (krow // CHUNK) <= (qcol // CHUNK)

    def tiles(j, nt, masked):
        rows = pl.ds(pl.multiple_of(j * tb, tb), nt * tb)
        kr = kr_ref[0, rows, :]
        qk = [_dot_nt(jnp.concatenate([kn_ref[0, rows, h * QK_NOPE:(h + 1) * QK_NOPE], kr], axis=1),
                      q_ref[0, :, h * 2 * LANES:(h + 1) * 2 * LANES]) for h in range(heads)]
        for h in range(heads):
            s = jnp.where(diag_vis, qk[h], -jnp.inf) if masked else qk[h]
            _softmax_step(s, [vt_ref[0, h, j + i] for i in range(nt)], m_scr, acc_scr, h)

    def pair_logits(p, buf):
        rows = pl.ds(pl.multiple_of(p * 2 * tb, 2 * tb), 2 * tb)
        kr = kr_ref[0, rows, :]
        for h in range(heads):
            kc = jnp.concatenate([kn_ref[0, rows, h * QK_NOPE:(h + 1) * QK_NOPE], kr], axis=1)
            buf[h] = _dot_nt(kc, q_ref[0, :, h * 2 * LANES:(h + 1) * 2 * LANES])

    def pair_softmax(p, buf):
        for h in range(heads):
            _softmax_step(buf[h], [vt_ref[0, h, 2 * p], vt_ref[0, h, 2 * p + 1]], m_scr, acc_scr, h)

    def full_tile(j, carry):
        tiles(j, 1, False)
        return carry

    n_pair = qt // 2
    _pipelined_blocks(n_pair, pair_logits, pair_softmax, qka_scr, qkb_scr)
    lax.fori_loop(2 * n_pair, qt, full_tile, 0)
    tiles(qt, 1, True)
    _softmax_finish(acc_scr, g_ref, o_ref, heads)


def _mla(name, q, kn, kr, vt, gate, tb, past_len, heads=8):
    b, t, _ = q.shape
    l = kn.shape[1]
    assert t % tb == 0 and past_len % tb == 0 and l == past_len + t and C_HEADS % heads == 0
    nkv = l // tb
    body = functools.partial(_mla_body, tb=tb, qt0=past_len // tb, heads=heads)
    return pl.pallas_call(
        body,
        grid=(b, C_HEADS // heads, t // tb),
        in_specs=[
            pl.BlockSpec((1, tb, heads * 2 * LANES), lambda bi, h, qi: (bi, qi, h)),
            pl.BlockSpec((1, l, heads * QK_NOPE), lambda bi, h, qi: (bi, 0, h)),
            pl.BlockSpec((1, l, LANES), lambda bi, h, qi: (bi, 0, 0)),
            pl.BlockSpec((1, heads, nkv, VT_ROWS, tb), lambda bi, h, qi: (bi, h, 0, 0, 0)),
            pl.BlockSpec((1, tb, heads * V_DIM), lambda bi, h, qi: (bi, qi, h)),
        ],
        out_specs=pl.BlockSpec((1, tb, heads * V_DIM), lambda bi, h, qi: (bi, qi, h)),
        out_shape=jax.ShapeDtypeStruct((b, t, C_WIDTH), BF16),
        scratch_shapes=[pltpu.VMEM((heads, 1, tb), F32), pltpu.VMEM((heads, VT_ROWS, tb), F32),
                        pltpu.VMEM((heads, 2 * tb, tb), F32), pltpu.VMEM((heads, 2 * tb, tb), F32)],
        compiler_params=_params(("parallel", "parallel", "parallel")),
        name=name,
    )(q, kn, kr, vt, gate)


def _even_weights(w_in):
    offs = [0]
    for s in (A_WIDTH, A_WIDTH, A_WIDTH, A_WIDTH, IDX_HEADS * IDX_DIM, IDX_DIM, IDX_HEADS, B_WIDTH, B_CONV_DIM, B_HEADS):
        offs.append(offs[-1] + s)
    cols = [w_in[:, offs[i]:offs[i + 1]] for i in range(10)]
    aq, ak, av, ag, iq, ik, iw, bz, bxbc, bdt = cols
    aq = aq * (A_HEAD_DIM ** -0.5 * LOG2E)
    pad = jnp.zeros((w_in.shape[0], LANES - IDX_DIM - IDX_HEADS - B_HEADS), w_in.dtype)
    small = jnp.concatenate([ik, iw, bdt, pad], axis=1)
    return [c.astype(BF16) for c in (aq, ak, av, ag, iq, small, bz, bxbc)]


def _even_layer(tag, x, past, wts, w_out, conv_w, conv_b, dt_bias, a_log, d_skip, norm_w, ln_g, ln_b, t5_bias, tb):
    b, t, _ = x.shape
    m = b * t
    x2 = x.reshape(m, D_MODEL)
    w_aq, w_ak, w_av, w_ag, w_iq, w_small, w_bz, w_bxbc = wts
    aq, xb = _mm(tag + "_in_aq", x2, w_aq, [BF16, ("acopy",)])
    ak, ak16 = _mm(tag + "_in_ak", xb, w_ak, [F32, BF16])
    (ag,) = _mm(tag + "_in_ag", xb, w_ag, [("silu", F32)])
    (iqt,) = _mm(tag + "_in_iq", xb, w_iq, [("heads", t, IDX_DIM, BF16)])
    (small,) = _mm(tag + "_in_small", xb, w_small, [F32])
    (bzg,) = _mm(tag + "_in_bz", xb, w_bz, [("silu", F32)])
    ik = small[:, :IDX_DIM].reshape(b, t, IDX_DIM)
    iwt = jnp.swapaxes(small[:, IDX_DIM:IDX_DIM + IDX_HEADS].reshape(b, t, IDX_HEADS), 1, 2)
    bdt = small[:, IDX_DIM + IDX_HEADS:IDX_DIM + IDX_HEADS + B_HEADS].reshape(b, t, B_HEADS)
    k16 = ak16.reshape(b, t, A_WIDTH)
    ik16 = ik.astype(BF16)
    if past is None:
        p_len = 0
        av, vt = _mm(tag + "_in_av", xb, w_av, [F32, ("vt", t, tb, BF16)])
        conv0 = jnp.zeros((b, SUBLANES, B_CONV_DIM), F32)
        h0 = jnp.zeros((b, B_HEADS, B_HEAD_DIM, B_STATE), F32)
    else:
        pk, pv, pki, pconv, pssm = past
        p_len = pk.shape[1]
        av, av16 = _mm(tag + "_in_av", xb, w_av, [F32, BF16])
        k16 = jnp.concatenate([pk.reshape(b, p_len, A_WIDTH).astype(BF16), k16], axis=1)
        vt = _to_vt(jnp.concatenate([pv.reshape(b, p_len, A_WIDTH).astype(BF16), av16.reshape(b, t, A_WIDTH)], axis=1), tb)
        ik16 = jnp.concatenate([pki.astype(BF16), ik16], axis=1)
        conv0 = jnp.pad(pconv, ((0, 0), (SUBLANES - (CONV_W - 1), 0), (0, 0)))
        h0 = pssm
    xbc, tails = _mm_conv(tag + "_in_bxbc", xb, w_bxbc, conv0, conv_w, conv_b, t)
    conv_new = tails.reshape(b, -1, SUBLANES, B_CONV_DIM)[:, -1, SUBLANES - (CONV_W - 1):]
    a_out = _dsa(tag + "_dsa", iqt, iwt, ik16, aq.reshape(b, t, A_WIDTH), k16, vt,
                 ag.reshape(b, t, A_WIDTH), _bias_tables(t5_bias, tb), tb, p_len)
    b_out, ssm_new = _ssd(tag + "_ssd", bzg.reshape(b, t, B_WIDTH), xbc.reshape(b, t, B_CONV_DIM), bdt, dt_bias, a_log,
                          d_skip, norm_w, h0)
    wo = w_out.astype(BF16)
    y, y16 = _oproj_ln(tag + "_out0", [a_out.reshape(m, A_WIDTH), b_out.reshape(m, B_WIDTH)],
                       [wo[:A_WIDTH], wo[A_WIDTH:]], x2, ln_g, ln_b)
    state = (ak.reshape(b, t, A_HEADS, A_HEAD_DIM), av.reshape(b, t, A_HEADS, A_HEAD_DIM), ik, conv_new, ssm_new)
    return y.reshape(b, t, D_MODEL), y16, state


def _rope_rot_cols(w):
    half = QK_ROPE // 2
    return jnp.concatenate([-w[..., half:], w[..., :half]], axis=-1)


def _odd_weights(w_in, w_uq, w_ukv):
    w_cq = w_in[:, :Q_LORA]
    w_ckv = w_in[:, Q_LORA:Q_LORA + KV_LORA]
    w_kr = w_in[:, Q_LORA + KV_LORA:Q_LORA + KV_LORA + QK_ROPE]
    w_gate = w_in[:, Q_LORA + KV_LORA + QK_ROPE:]
    w_kr2 = jnp.concatenate([w_kr, _rope_rot_cols(w_kr)], axis=1)
    uq = w_uq.reshape(Q_LORA, C_HEADS, QK_NOPE + QK_ROPE) * (MLA_SCALE * LOG2E)
    uq_rope = uq[..., QK_NOPE:]
    uq2 = jnp.concatenate([uq[..., :QK_NOPE], uq_rope, _rope_rot_cols(uq_rope)], axis=-1).reshape(Q_LORA, C_HEADS * 2 * LANES)
    ukv = w_ukv.reshape(KV_LORA, C_HEADS, QK_NOPE + V_DIM)
    w_uk = ukv[..., :QK_NOPE].reshape(KV_LORA, C_HEADS * QK_NOPE)
    w_uv = ukv[..., QK_NOPE:].reshape(KV_LORA, C_HEADS * V_DIM)
    return [c.astype(BF16) for c in (w_cq, w_ckv, w_kr2, w_gate, uq2, w_uk, w_uv)]


def _rope_table(pos):
    half = QK_ROPE // 2
    inv = ROPE_THETA ** (-jnp.arange(half, dtype=F32) / half)
    ang = pos.astype(F32)[:, None] * inv[None, :]
    cos, sin = jnp.cos(ang), jnp.sin(ang)
    return jnp.concatenate([cos, cos, sin, sin], axis=1)


def _odd_layer(tag, x, x16, past, wts, q_norm_w, kv_norm_w, w_out, ln_g, ln_b, tb):
    b, t, _ = x.shape
    m = b * t
    w_cq, w_ckv, w_kr2, w_gate, w_uq2, w_uk, w_uv = wts
    p_len = 0 if past is None else past[0].shape[1]
    tab = _rope_table(p_len + jnp.arange(t, dtype=I32))
    (cq16,) = _mm_rms(tag + "_in_cq", x16, w_cq, q_norm_w, [BF16])
    ckv, ckv16 = _mm_rms(tag + "_in_ckv", x16, w_ckv, kv_norm_w, [F32, BF16])
    kr, kr16 = _mm_rope_k(tag + "_in_kr", x16, w_kr2, tab)
    (gate,) = _mm(tag + "_in_gate", x16, w_gate, [("silu", F32)])
    q = _mm_rope_q(tag + "_uq", cq16, w_uq2, tab)
    lat16 = ckv16.reshape(b, t, KV_LORA)
    kr16 = kr16.reshape(b, t, LANES)
    if past is not None:
        lat16 = jnp.concatenate([past[0].astype(BF16), lat16], axis=1)
        kr_past = jnp.pad(past[1], ((0, 0), (0, 0), (0, LANES - QK_ROPE))).astype(BF16)
        kr16 = jnp.concatenate([kr_past, kr16], axis=1)
    l = p_len + t
    lat2 = lat16.reshape(b * l, KV_LORA)
    (kn,) = _mm(tag + "_uk", lat2, w_uk, [BF16])
    if tb % LANES == 0:
        (vt,) = _mm(tag + "_uv", lat2, w_uv, [("vt", l, tb, BF16)])
    else:
        (v,) = _mm(tag + "_uv", lat2, w_uv, [BF16])
        vt = _to_vt(v.reshape(b, l, C_WIDTH), tb)
    o = _mla(tag + "_mla", q.reshape(b, t, C_HEADS * 2 * LANES), kn.reshape(b, l, C_HEADS * QK_NOPE), kr16,
             vt, gate.reshape(b, t, C_WIDTH), tb, p_len)
    y, _ = _oproj_ln(tag + "_out1", [o.reshape(m, C_WIDTH)], [w_out.astype(BF16)], x.reshape(m, D_MODEL), ln_g, ln_b)
    return y.reshape(b, t, D_MODEL), (ckv.reshape(b, t, KV_LORA), kr.reshape(b, t, QK_ROPE))


def kernel(x_prompt, x_sample, cache_a_k, cache_a_v, cache_a_kidx, state_b_conv, state_b_ssm, cache_c_latent, cache_c_krope, t5_bias, w_in0, w_out0, conv_w, conv_b, dt_bias, a_log, d_skip, ssm_norm_w, ln0_g, ln0_b, w_in1, q_norm_w, kv_norm_w, w_uq, w_ukv, w_out1, ln1_g, ln1_b):
    tb_prompt = 256
    tb_sample = CHUNK
    ew = _even_weights(w_in0[0])
    eprm = (w_out0[0], conv_w[0], conv_b[0], dt_bias[0], a_log[0], d_skip[0], ssm_norm_w[0], ln0_g[0], ln0_b[0], t5_bias)
    yp, yp16, st_p = _even_layer("p0", x_prompt, None, ew, *eprm, tb_prompt)
    past = (cache_a_k[0], cache_a_v[0], cache_a_kidx[0], state_b_conv[0], state_b_ssm[0])
    ys, ys16, st_s = _even_layer("s0", x_sample, past, ew, *eprm, tb_sample)
    ow = _odd_weights(w_in1[0], w_uq[0], w_ukv[0])
    oprm = (q_norm_w[0], kv_norm_w[0], w_out1[0], ln1_g[0], ln1_b[0])
    yp, od_p = _odd_layer("p1", yp, yp16, None, ow, *oprm, tb_prompt)
    ys, od_s = _odd_layer("s1", ys, ys16, (cache_c_latent[0], cache_c_krope[0]), ow, *oprm, tb_sample)
    e = lambda a: a[None]
    return (yp, ys, e(st_p[0]), e(st_s[0]), e(st_p[1]), e(st_s[1]), e(st_p[2]), e(st_s[2]),
            e(st_p[3]), e(st_s[3]), e(st_p[4]), e(st_s[4]), e(od_p[0]), e(od_s[0]), e(od_p[1]), e(od_s[1]))
```

```python
import functools
import math

import jax
import jax.numpy as jnp
from jax import lax
from jax.experimental import pallas as pl
from jax.experimental.pallas import tpu as pltpu

F32 = jnp.float32
BF16 = jnp.bfloat16
I32 = jnp.int32
I16 = jnp.int16

D_MODEL = 2048
CHUNK = 64
A_HEADS = 8
A_HEAD_DIM = 128
A_WIDTH = A_HEADS * A_HEAD_DIM
IDX_HEADS = 16
IDX_DIM = 64
IDX_W_SCALE = (IDX_HEADS * IDX_DIM) ** -0.5
TOPK_MAX = 256
N_BUCKETS = 32
MAX_DISTANCE = 128
B_HEAD_DIM = 64
B_WIDTH = D_MODEL
B_HEADS = B_WIDTH // B_HEAD_DIM
B_GROUPS = 4
B_HPG = B_HEADS // B_GROUPS
B_STATE = 128
CONV_W = 4
B_CONV_DIM = B_WIDTH + 2 * B_GROUPS * B_STATE
C_HEADS = 16
Q_LORA = 512
KV_LORA = 512
QK_NOPE = 128
QK_ROPE = 64
V_DIM = 128
C_WIDTH = C_HEADS * V_DIM
ROPE_THETA = 10000.0
MLA_SCALE = (QK_NOPE + QK_ROPE) ** -0.5
DEPTH = 2
ALPHA = (2 * DEPTH) ** 0.25
EPS = 1e-5

LANES = 128
SUBLANES = 8
VT_ROWS = LANES + 2 * SUBLANES
VMEM_LIMIT = 56 * 1024 * 1024
LOG2E = math.log2(math.e)
INT_MIN = -(2 ** 31)
INT_MAX = 2 ** 31 - 1
KEY_NEG_INF = (0xFF800000 ^ 0x7FFFFFFF) - (1 << 32)
HALF_BITS = 16
HALF_MASK = (1 << HALF_BITS) - 1
HALF_OFFSET = 1 << (HALF_BITS - 1)
HALF_MIN = -HALF_OFFSET
NEG_BIG = -1e30


def _params(sem):
    return pltpu.CompilerParams(dimension_semantics=sem, vmem_limit_bytes=VMEM_LIMIT)


def _dot_nt(a, b):
    return lax.dot_general(a, b, (((1,), (1,)), ((), ())), preferred_element_type=F32)


def _silu(x):
    return x * (1.0 / (1.0 + jnp.exp(-x)))


def _split3(v):
    hi = v.astype(BF16)
    r = v - hi.astype(F32)
    mid = r.astype(BF16)
    return hi, mid, (r - mid.astype(F32)).astype(BF16)


def _dot01(a, b, split_rhs):
    if split_rhs:
        parts = [jnp.dot(a, t, preferred_element_type=F32) for t in _split3(b)]
    else:
        parts = [jnp.dot(t, b, preferred_element_type=F32) for t in _split3(a)]
    return parts[0] + parts[1] + parts[2]


def _mm_body(a_ref, w_ref, *o_refs, kinds):
    a = a_ref[...]
    if a.dtype != w_ref.dtype:
        a = a.astype(w_ref.dtype)
    acc = jnp.dot(a, w_ref[...], preferred_element_type=F32)
    tm, tn = acc.shape
    for kind, o in zip(kinds, o_refs):
        if kind[0] == "plain":
            o[...] = acc.astype(o.dtype)
        elif kind[0] == "silu":
            o[...] = _silu(acc).astype(o.dtype)
        elif kind[0] == "acopy":
            @pl.when(pl.program_id(1) == 0)
            def _():
                o[...] = a
        elif kind[0] == "vt":
            tb = kind[2]
            for hh in range(tn // LANES):
                for kt in range(tm // tb):
                    o[0, hh, kt, 0:LANES, :] = acc[kt * tb:(kt + 1) * tb, hh * LANES:(hh + 1) * LANES].T.astype(o.dtype)
                    o[0, hh, kt, LANES:VT_ROWS, :] = _ones_rows(tb)
        elif kind[0] == "heads":
            hd = kind[2]
            for hh in range(tn // hd):
                o[0, hh] = acc[:, hh * hd:(hh + 1) * hd].astype(o.dtype)


def _mm_rms_body(a_ref, w_ref, g_ref, *o_refs):
    acc = jnp.dot(a_ref[...], w_ref[...], preferred_element_type=F32)
    y = acc * lax.rsqrt(jnp.mean(acc * acc, axis=-1, keepdims=True) + EPS) * g_ref[...]
    for o in o_refs:
        o[...] = y.astype(o.dtype)


def _rope_half(t):
    return t + pltpu.roll(t, QK_ROPE, 1)


def _mm_rope_k_body(a_ref, w_ref, tab_ref, o32_ref, o16_ref):
    acc = jnp.dot(a_ref[...], w_ref[...], preferred_element_type=F32)
    r = _rope_half(acc * tab_ref[...])
    lane = lax.broadcasted_iota(I32, r.shape, 1)
    o32_ref[...] = r[:, :QK_ROPE]
    o16_ref[...] = jnp.where(lane < QK_ROPE, r, 0.0).astype(o16_ref.dtype)


def _mm_rope_q_body(a_ref, w_ref, tab_ref, o_ref, *, heads):
    acc = jnp.dot(a_ref[...], w_ref[...], preferred_element_type=F32)
    tab = tab_ref[...]
    lane = lax.broadcasted_iota(I32, tab.shape, 1)
    for h in range(heads):
        base = h * 2 * LANES
        o_ref[:, base:base + LANES] = acc[:, base:base + LANES].astype(o_ref.dtype)
        r = _rope_half(acc[:, base + LANES:base + 2 * LANES] * tab)
        o_ref[:, base + LANES:base + 2 * LANES] = jnp.where(lane < QK_ROPE, r, 0.0).astype(o_ref.dtype)


def _mm_call(name, body, a, w, extra, extra_specs, out_cols, out_dtypes, tm, tn):
    m, k = a.shape
    n = w.shape[1]
    tm = math.gcd(tm, m)
    tn = min(tn, n)
    assert tm % SUBLANES == 0 and n % tn == 0, (m, n, tm, tn)
    oc = [tn if c is None else c for c in out_cols]
    return pl.pallas_call(
        body,
        grid=(m // tm, n // tn),
        in_specs=[pl.BlockSpec((tm, k), lambda i, j: (i, 0)),
                  pl.BlockSpec((k, tn), lambda i, j: (0, j))] + extra_specs(tm, tn),
        out_specs=[pl.BlockSpec((tm, c), lambda i, j: (i, j)) for c in oc],
        out_shape=[jax.ShapeDtypeStruct((m, (n // tn) * c), d) for c, d in zip(oc, out_dtypes)],
        compiler_params=_params(("parallel", "parallel")),
        name=name,
    )(a, w, *extra)


def _mm(name, a, w, outs, tm=1024, tn=512):
    m, k = a.shape
    n = w.shape[1]
    tm = math.gcd(tm, m, *[o[1] for o in outs if isinstance(o, tuple) and o[0] in ("vt", "heads")])
    tn = min(tn, n)
    assert tm % SUBLANES == 0 and n % tn == 0, (m, n, tm, tn)
    kinds, specs, shapes = [], [], []
    for o in outs:
        o = o if isinstance(o, tuple) else ("plain", o)
        if o[0] in ("plain", "silu"):
            kinds.append((o[0],))
            specs.append(pl.BlockSpec((tm, tn), lambda i, j: (i, j)))
            shapes.append(jax.ShapeDtypeStruct((m, n), o[1]))
        elif o[0] == "acopy":
            kinds.append(("acopy",))
            specs.append(pl.BlockSpec((tm, k), lambda i, j: (i, 0)))
            shapes.append(jax.ShapeDtypeStruct((m, k), w.dtype))
        elif o[0] == "vt":
            _, t, tb, dt = o
            assert t % tm == 0 and tm % tb == 0 and tn % LANES == 0
            per = t // tm
            kinds.append(("vt", t, tb))
            specs.append(pl.BlockSpec((1, tn // LANES, tm // tb, VT_ROWS, tb),
                                      lambda i, j: (i // per, j, i % per, 0, 0)))
            shapes.append(jax.ShapeDtypeStruct((m // t, n // LANES, t // tb, VT_ROWS, tb), dt))
        elif o[0] == "heads":
            _, t, hd, dt = o
            assert t % tm == 0 and tn % hd == 0
            per = t // tm
            kinds.append(("heads", t, hd))
            specs.append(pl.BlockSpec((1, tn // hd, tm, hd), lambda i, j: (i // per, j, i % per, 0)))
            shapes.append(jax.ShapeDtypeStruct((m // t, n // hd, t, hd), dt))
        else:
            raise ValueError(o)
    return pl.pallas_call(
        functools.partial(_mm_body, kinds=tuple(kinds)),
        grid=(m // tm, n // tn),
        in_specs=[pl.BlockSpec((tm, k), lambda i, j: (i, 0)), pl.BlockSpec((k, tn), lambda i, j: (0, j))],
        out_specs=specs,
        out_shape=shapes,
        compiler_params=_params(("parallel", "arbitrary")),
        name=name,
    )(a, w)


def _mm_conv_body(a_ref, w_ref, st_ref, cw_ref, cb_ref, o_ref, tail_ref, ext_scr, prev_scr, *, tiles_per_seq):
    i, j = pl.program_id(0), pl.program_id(1)
    tm, tn = o_ref.shape

    @pl.when(i % tiles_per_seq == 0)
    def _():
        ext_scr[0:SUBLANES, :] = st_ref[0]

    @pl.when(i % tiles_per_seq != 0)
    def _():
        ext_scr[0:SUBLANES, :] = prev_scr[j]

    cw_ = min(tn, 2 * LANES)
    for c in range(tn // cw_):
        cs = slice(c * cw_, (c + 1) * cw_)
        ext_scr[SUBLANES:SUBLANES + tm, cs] = jnp.dot(a_ref[...], w_ref[:, cs], preferred_element_type=F32)
        conv = cb_ref[:, cs]
        for tap in range(CONV_W):
            lo = SUBLANES - (CONV_W - 1) + tap
            conv = conv + ext_scr[lo:lo + tm, cs] * cw_ref[tap:tap + 1, cs]
        o_ref[:, cs] = _silu(conv).astype(o_ref.dtype)
    tail = ext_scr[tm:tm + SUBLANES, :]
    prev_scr[j] = tail
    tail_ref[0] = tail


def _mm_conv(name, a, w, state, conv_w, conv_b, t, tm=1024, tn=512):
    m, k = a.shape
    n = w.shape[1]
    tm = math.gcd(tm, t)
    tn = min(tn, n)
    assert n % tn == 0 and tm % SUBLANES == 0
    per = t // tm
    return pl.pallas_call(
        functools.partial(_mm_conv_body, tiles_per_seq=per),
        grid=(m // tm, n // tn),
        in_specs=[pl.BlockSpec((tm, k), lambda i, j: (i, 0)),
                  pl.BlockSpec((k, tn), lambda i, j: (0, j)),
                  pl.BlockSpec((1, SUBLANES, tn), lambda i, j: (i // per, 0, j)),
                  pl.BlockSpec((CONV_W, tn), lambda i, j: (0, j)),
                  pl.BlockSpec((1, tn), lambda i, j: (0, j))],
        out_specs=[pl.BlockSpec((tm, tn), lambda i, j: (i, j)),
                   pl.BlockSpec((1, SUBLANES, tn), lambda i, j: (i, 0, j))],
        out_shape=[jax.ShapeDtypeStruct((m, n), F32), jax.ShapeDtypeStruct((m // tm, SUBLANES, n), F32)],
        scratch_shapes=[pltpu.VMEM((tm + SUBLANES, tn), F32), pltpu.VMEM((n // tn, SUBLANES, tn), F32)],
        compiler_params=_params(("arbitrary", "arbitrary")),
        name=name,
    )(a, w, state, conv_w, conv_b.reshape(1, n))


def _mm_rms(name, a, w, g, out_dtypes, tm=1024):
    n = w.shape[1]
    return _mm_call(name, _mm_rms_body, a, w, [g.reshape(1, n)],
                    lambda tm_, tn_: [pl.BlockSpec((1, n), lambda i, j: (0, 0))],
                    [None] * len(out_dtypes), out_dtypes, tm, n)


def _tab_spec(t_rows):
    def spec(tm, tn):
        nt = t_rows // tm
        return [pl.BlockSpec((tm, LANES), lambda i, j: (i % nt, 0))]
    return spec


def _mm_rope_k(name, a, w, tab, tm=1024):
    tm = min(tm, tab.shape[0])
    return _mm_call(name, _mm_rope_k_body, a, w, [tab], _tab_spec(tab.shape[0]), [QK_ROPE, LANES], [F32, BF16],
                    tm, LANES)


def _mm_rope_q(name, a, w, tab, tm=1024, heads_per_block=8):
    tm = min(tm, tab.shape[0])
    body = functools.partial(_mm_rope_q_body, heads=heads_per_block)
    return _mm_call(name, body, a, w, [tab], _tab_spec(tab.shape[0]), [None], [BF16], tm,
                    heads_per_block * 2 * LANES)[0]


def _oproj_body(*refs, n_parts):
    parts = refs[:n_parts]
    ws = refs[n_parts:2 * n_parts]
    x_ref, g_ref, b_ref, o32_ref, o16_ref = refs[2 * n_parts:]
    acc = ALPHA * x_ref[...]
    for p, w in zip(parts, ws):
        acc = acc + jnp.dot(p[...], w[...], preferred_element_type=F32)
    mu = jnp.mean(acc, axis=-1, keepdims=True)
    d = acc - mu
    var = jnp.mean(d * d, axis=-1, keepdims=True)
    y = d * lax.rsqrt(var + EPS) * g_ref[...] + b_ref[...]
    o32_ref[...] = y
    o16_ref[...] = y.astype(o16_ref.dtype)


def _oproj_ln(name, parts, ws, x, g, b, tm=256):
    m, n = x.shape
    tm = min(tm, m)
    np_ = len(parts)
    const = lambda i: (0, 0)
    return pl.pallas_call(
        functools.partial(_oproj_body, n_parts=np_),
        grid=(m // tm,),
        in_specs=[pl.BlockSpec((tm, p.shape[1]), lambda i: (i, 0)) for p in parts]
        + [pl.BlockSpec(w.shape, const, pipeline_mode=pl.Buffered(1)) for w in ws]
        + [pl.BlockSpec((tm, n), lambda i: (i, 0)),
           pl.BlockSpec((1, n), const), pl.BlockSpec((1, n), const)],
        out_specs=[pl.BlockSpec((tm, n), lambda i: (i, 0))] * 2,
        out_shape=[jax.ShapeDtypeStruct((m, n), F32), jax.ShapeDtypeStruct((m, n), BF16)],
        compiler_params=_params(("parallel",)),
        name=name,
    )(*parts, *ws, x, g.reshape(1, n), b.reshape(1, n))


def _ones_rows(tb):
    return jnp.ones((VT_ROWS - LANES, tb), BF16)


def _to_vt(v, tb):
    b, l, hd = v.shape
    h = hd // LANES
    vt = jnp.transpose(v.reshape(b, l // tb, tb, h, LANES), (0, 3, 1, 4, 2))
    return jnp.concatenate([vt, jnp.ones(vt.shape[:3] + (VT_ROWS - LANES, tb), vt.dtype)], axis=3)


def _softmax_init(m_scr, acc_scr, heads, tb):
    for h in range(heads):
        m_scr[h] = jnp.full((1, tb), NEG_BIG, F32)
        acc_scr[h] = jnp.zeros(acc_scr.shape[1:], F32)


def _softmax_step(s, vts, m_scr, acc_scr, h):
    m_prev = m_scr[h]
    m_new = jnp.maximum(m_prev, jnp.max(s, axis=0, keepdims=True))
    alpha = jnp.exp2(m_prev - m_new)
    p = jnp.exp2(s - m_new)
    pb = p.astype(BF16)
    tk = s.shape[0] // len(vts)
    acc = alpha * acc_scr[h]
    for i, vt in enumerate(vts):
        acc = acc + jnp.dot(vt, pb[i * tk:(i + 1) * tk], preferred_element_type=F32)
    acc_scr[h] = acc
    m_scr[h] = m_new


def _pipelined_blocks(n_blocks, logits_fn, consume_fn, buf_a, buf_b):
    @pl.when(n_blocks > 0)
    def _():
        logits_fn(0, buf_a)

    def two_blocks(u, carry):
        i = 2 * u
        logits_fn(i + 1, buf_b)
        consume_fn(i, buf_a)
        logits_fn(jnp.minimum(i + 2, n_blocks - 1), buf_a)
        consume_fn(i + 1, buf_b)
        return carry

    lax.fori_loop(0, n_blocks // 2, two_blocks, 0)

    @pl.when(n_blocks % 2 == 1)
    def _():
        consume_fn(n_blocks - 1, buf_a)


def _pipelined_blocks_final(n_plain, logits_fn, consume_fn, final_fn, buf_a, buf_b):
    logits_fn(0, buf_a)

    def two_blocks(u, carry):
        i = 2 * u
        logits_fn(i + 1, buf_b)
        consume_fn(i, buf_a)
        logits_fn(i + 2, buf_a)
        consume_fn(i + 1, buf_b)
        return carry

    lax.fori_loop(0, n_plain // 2, two_blocks, 0)

    @pl.when(n_plain % 2 == 0)
    def _():
        final_fn(n_plain, buf_a)

    @pl.when(n_plain % 2 == 1)
    def _():
        logits_fn(n_plain, buf_b)
        consume_fn(n_plain - 1, buf_a)
        final_fn(n_plain, buf_b)


def _softmax_finish(acc_scr, g_ref, o_ref, heads):
    for h in range(heads):
        hs = slice(h * LANES, (h + 1) * LANES)
        o = (acc_scr[h, 0:LANES, :] / acc_scr[h, LANES:LANES + 1, :]).T
        o_ref[0, :, hs] = (o * g_ref[0, :, hs]).astype(o_ref.dtype)


def _t5_bucket(rel):
    half = N_BUCKETS // 2
    max_exact = half // 2
    ret = jnp.where(rel < 0, half, 0)
    n = jnp.abs(rel)
    nf = jnp.maximum(n, 1).astype(F32)
    large = max_exact + (jnp.log(nf / max_exact) / math.log(MAX_DISTANCE / max_exact) * (half - max_exact)).astype(jnp.int32)
    large = jnp.minimum(large, half - 1)
    return ret + jnp.where(n < max_exact, n, large)


def _num_special_tiles(tb):
    return (MAX_DISTANCE - 2 + 2 * tb) // tb


def _bias_tables(t5_bias, tb):
    ns = _num_special_tiles(tb)
    s = jnp.arange(tb, dtype=I32)[:, None]
    t = jnp.arange(tb, dtype=I32)[None, :]
    bucket = jnp.stack([_t5_bucket(tb * d + t - s) for d in range(ns)])
    far = t5_bias[_t5_bucket(jnp.int32(tb * ns))]
    out = jnp.zeros((ns, A_HEADS, tb, tb), F32)
    for k in range(N_BUCKETS):
        out = jnp.where(bucket[:, None] == k, t5_bias[k][None, :, None, None], out)
    return (out - far[None, :, None, None]) * LOG2E


def _dsa_body(iq_ref, iwt_ref, ik_ref, q_ref, k_ref, vt_ref, g_ref, bias_ref, o_ref,
              key_scr, hi_scr, lo_scr, mb_scr, x_scr, m_scr, acc_scr, qka_scr, qkb_scr, *,
              tb, qt0, ns, n_sel, idx_bits):
    qt = pl.program_id(1) + qt0
    nkv = qt + 1
    krow = lax.broadcasted_iota(I32, (tb, tb), 0)
    qcol = lax.broadcasted_iota(I32, (tb, tb), 1)
    diag_vis = (krow // CHUNK) <= (qcol // CHUNK)

    wt = iwt_ref[0] * IDX_W_SCALE

    def idx_tile(j, carry):
        kt = ik_ref[0, pl.ds(pl.multiple_of(j * tb, tb), tb), :]
        acc = jnp.zeros((tb, tb), F32)
        for h in range(IDX_HEADS):
            acc = acc + jnp.maximum(_dot_nt(kt, iq_ref[0, h]), 0.0) * wt[h:h + 1, :]
        acc = jnp.where(acc == 0.0, 0.0, acc)
        s = jnp.where(jnp.logical_or(diag_vis, j < qt), acc, -jnp.inf)
        bits = pltpu.bitcast(s, I32)
        key = bits ^ ((bits >> 31) & 0x7FFFFFFF)
        key_scr[j] = key
        hi_scr[j] = (key >> HALF_BITS).astype(I16)
        lo_scr[j] = ((key & HALF_MASK) - HALF_OFFSET).astype(I16)
        return carry

    lax.fori_loop(0, nkv, idx_tile, 0)

    def count(pred):
        def body(j, c):
            f = jnp.where(pred(key_scr[j], j), 1.0, 0.0)
            return c + jnp.sum(f.reshape(tb // SUBLANES, SUBLANES, tb), axis=0)
        c = lax.fori_loop(0, nkv, body, jnp.zeros((SUBLANES, tb), F32))
        return jnp.sum(c, axis=0, keepdims=True)

    pack = 2 * SUBLANES
    one16, zero16 = jnp.ones((), I16), jnp.zeros((), I16)

    def count16(ref, cand, strict):
        c16 = cand.astype(I16)

        def tile_count(j):
            v = ref[j]
            f = jnp.where(v > c16 if strict else v >= c16, one16, zero16).reshape(tb // pack, pack, tb)
            part = f[0]
            for r in range(1, tb // pack):
                part = part + f[r]
            return part

        def two_tiles(u, c):
            return c + (tile_count(2 * u) + tile_count(2 * u + 1)).astype(I32)

        def one_tile(j, c):
            return c + tile_count(j).astype(I32)

        c = lax.fori_loop(0, nkv // 2, two_tiles, jnp.zeros((pack, tb), I32))
        c = lax.fori_loop(2 * (nkv // 2), nkv, one_tile, c)
        return jnp.sum(c, axis=0, keepdims=True)

    def search16(ref, target):
        v0 = jnp.where(count16(ref, jnp.zeros((1, tb), I32), False) >= target, 0, HALF_MIN).astype(I32)

        def bit_body(i, v):
            cand = v + jnp.left_shift(jnp.int32(1), HALF_BITS - 2 - i)
            return jnp.where(count16(ref, cand, False) >= target, cand, v)
        return lax.fori_loop(0, HALF_BITS - 1, bit_body, v0)

    thr_hi = search16(hi_scr, n_sel)
    need_lo = n_sel - count16(hi_scr, thr_hi, True)
    hi16 = thr_hi.astype(I16)

    def band_tile(j, carry):
        hi_scr[j] = jnp.where(hi_scr[j] == hi16, lo_scr[j], jnp.full((), HALF_MIN, I16))
        return carry

    lax.fori_loop(0, nkv, band_tile, 0)
    thr_lo = search16(hi_scr, need_lo)
    thr = thr_hi * (HALF_MASK + 1) + (thr_lo + HALF_OFFSET)
    nsel = float(n_sel)

    need = nsel - count(lambda kt, j: kt > thr)
    excess = jnp.logical_and(count(lambda kt, j: kt == thr) > need, thr > KEY_NEG_INF)
    x_scr[...] = jnp.full((1, tb), INT_MAX, I32)

    @pl.when(jnp.max(jnp.where(excess, 1.0, 0.0)) > 0.0)
    def _():
        def xbit(i, x):
            cand = x + jnp.left_shift(jnp.int32(1), idx_bits - 1 - i)
            c = count(lambda kt, j: jnp.logical_and(kt == thr, krow + j * tb < cand))
            return jnp.where(c < need, cand, x)
        x = lax.fori_loop(0, idx_bits, xbit, jnp.zeros((1, tb), I32))
        x_scr[...] = jnp.where(excess, x, INT_MAX)

    xcut = x_scr[...]

    def mask_tile(j, carry):
        kt = key_scr[j]
        tie = jnp.logical_and(kt == thr, krow + j * tb <= xcut)
        sel = jnp.logical_and(jnp.logical_or(kt > thr, tie), kt != KEY_NEG_INF)
        mb_scr[j] = jnp.where(sel, 0.0, -jnp.inf)
        return carry

    lax.fori_loop(0, nkv, mask_tile, 0)

    _softmax_init(m_scr, acc_scr, A_HEADS, tb)

    def attn_tiles(j, nt, biased):
        rows = pl.ds(pl.multiple_of(j * tb, tb), nt * tb)
        mb = mb_scr[pl.ds(j, nt)].reshape(nt * tb, tb)
        qk = [_dot_nt(k_ref[0, rows, h * A_HEAD_DIM:(h + 1) * A_HEAD_DIM],
                      q_ref[0, :, h * A_HEAD_DIM:(h + 1) * A_HEAD_DIM]) for h in range(A_HEADS)]
        for h in range(A_HEADS):
            s = mb + qk[h]
            if biased:
                s = bias_ref[qt - j, h] + s
            _softmax_step(s, [vt_ref[0, h, j + i] for i in range(nt)], m_scr, acc_scr, h)

    first = (qt + 1) % 2 if ns == 2 else 0

    def pair_logits(p, buf):
        rows = pl.ds(pl.multiple_of((first + 2 * p) * tb, tb), 2 * tb)
        for h in range(A_HEADS):
            hs = slice(h * A_HEAD_DIM, (h + 1) * A_HEAD_DIM)
            buf[h] = _dot_nt(k_ref[0, rows, hs], q_ref[0, :, hs])

    def pair_softmax(p, buf, biased=False):
        j = first + 2 * p
        mb = mb_scr[pl.ds(j, 2)].reshape(2 * tb, tb)
        for h in range(A_HEADS):
            s = mb + buf[h]
            if biased:
                s = jnp.concatenate([bias_ref[1, h], bias_ref[0, h]], axis=0) + s
            _softmax_step(s, [vt_ref[0, h, j], vt_ref[0, h, j + 1]], m_scr, acc_scr, h)

    def far_tile(j, carry):
        attn_tiles(j, 1, False)
        return carry

    def near_tile(j, carry):
        attn_tiles(j, 1, True)
        return carry

    if ns == 2:
        @pl.when(qt == 0)
        def _():
            attn_tiles(0, 1, True)

        @pl.when(jnp.logical_and(qt > 0, first == 1))
        def _():
            attn_tiles(0, 1, False)

        @pl.when(qt > 0)
        def _():
            _pipelined_blocks_final((qt + 1 - first) // 2 - 1, pair_logits, pair_softmax,
                                    functools.partial(pair_softmax, biased=True), qka_scr, qkb_scr)
    else:
        n_far = jnp.maximum(qt - (ns - 1), 0)
        n_pair = n_far // 2
        _pipelined_blocks(n_pair, pair_logits, pair_softmax, qka_scr, qkb_scr)
        lax.fori_loop(2 * n_pair, n_far, far_tile, 0)
        lax.fori_loop(n_far, nkv, near_tile, 0)
    _softmax_finish(acc_scr, g_ref, o_ref, A_HEADS)


def _dsa(name, iq, iwt, ik, q, k, vt, gate, bias_tabs, tb, past_len):
    b, t, aw = q.shape
    l = k.shape[1]
    assert t % tb == 0 and l % tb == 0 and past_len % tb == 0 and l == past_len + t
    nq, nkv = t // tb, l // tb
    ns = _num_special_tiles(tb)
    n_sel = min(TOPK_MAX, l // 4)
    body = functools.partial(_dsa_body, tb=tb, qt0=past_len // tb, ns=ns, n_sel=n_sel,
                             idx_bits=max(1, (l - 1).bit_length()))
    once = pl.Buffered(1)
    return pl.pallas_call(
        body,
        grid=(b, nq),
        in_specs=[
            pl.BlockSpec((1, IDX_HEADS, tb, IDX_DIM), lambda bi, qi: (bi, 0, qi, 0)),
            pl.BlockSpec((1, IDX_HEADS, tb), lambda bi, qi: (bi, 0, qi)),
            pl.BlockSpec((1, l, IDX_DIM), lambda bi, qi: (bi, 0, 0), pipeline_mode=once),
            pl.BlockSpec((1, tb, aw), lambda bi, qi: (bi, qi, 0)),
            pl.BlockSpec((1, l, aw), lambda bi, qi: (bi, 0, 0), pipeline_mode=once),
            pl.BlockSpec((1, A_HEADS, nkv, VT_ROWS, tb), lambda bi, qi: (bi, 0, 0, 0, 0), pipeline_mode=once),
            pl.BlockSpec((1, tb, aw), lambda bi, qi: (bi, qi, 0)),
            pl.BlockSpec(bias_tabs.shape, lambda bi, qi: (0, 0, 0, 0), pipeline_mode=once),
        ],
        out_specs=pl.BlockSpec((1, tb, aw), lambda bi, qi: (bi, qi, 0)),
        out_shape=jax.ShapeDtypeStruct((b, t, aw), BF16),
        scratch_shapes=[
            pltpu.VMEM((nkv, tb, tb), I32),
            pltpu.VMEM((nkv, tb, tb), I16),
            pltpu.VMEM((nkv, tb, tb), I16),
            pltpu.VMEM((nkv, tb, tb), F32),
            pltpu.VMEM((1, tb), I32),
            pltpu.VMEM((A_HEADS, 1, tb), F32),
            pltpu.VMEM((A_HEADS, VT_ROWS, tb), F32),
            pltpu.VMEM((A_HEADS, 2 * tb, tb), F32),
            pltpu.VMEM((A_HEADS, 2 * tb, tb), F32),
        ],
        compiler_params=_params(("parallel", "parallel")),
        name=name,
    )(iq, iwt, ik, q, k, vt, gate, bias_tabs)


def _ssd_body(zg_ref, xbc_ref, dt_ref, dtt_ref, dtb_ref, dtbt_ref, alog_ref, alogt_ref,
              dsk_ref, nw_ref, exp_ref, h0_ref, y_ref, hout_ref, h_scr, yi_scr):
    c = pl.program_id(1)
    l = CHUNK
    gw = B_WIDTH // B_GROUPS

    @pl.when(c == 0)
    def _():
        h_scr[...] = h0_ref[0]

    xbc = xbc_ref[0]
    xs = xbc[:, :B_WIDTH]

    dt = jax.nn.softplus(dt_ref[0, 0] + dtb_ref[...])
    dtt = jax.nn.softplus(dtt_ref[0, 0] + dtbt_ref[...])
    a = -jnp.exp(alog_ref[...])
    at = -jnp.exp(alogt_ref[...])
    ti = lax.broadcasted_iota(I32, (l, l), 0)
    si = lax.broadcasted_iota(I32, (l, l), 1)
    causal = si <= ti
    lower = jnp.where(causal, 1.0, 0.0).astype(BF16)
    upper = jnp.where(ti <= si, 1.0, 0.0).astype(BF16)
    acum = _dot01(lower, dt * a, split_rhs=True)
    acumt = _dot01(dtt * at, upper, split_rhs=False)
    a_last = acum[l - 1:l, :]
    expand = exp_ref[...]
    e_full = _dot01(jnp.exp(acum), expand, split_rhs=False)
    tail_full = _dot01(jnp.exp(a_last - acum) * dt, expand, split_rhs=False)
    xt = (xs * tail_full).astype(BF16)
    xs16 = xs.astype(BF16)
    lane = lax.broadcasted_iota(I32, (l, 2 * B_HEAD_DIM), 1)

    for g in range(B_GROUPS):
        bm = xbc[:, B_WIDTH + g * B_STATE:B_WIDTH + (g + 1) * B_STATE].astype(BF16)
        cm = xbc[:, B_WIDTH + (B_GROUPS + g) * B_STATE:B_WIDTH + (B_GROUPS + g + 1) * B_STATE].astype(BF16)
        cb = _dot_nt(cm, bm)
        gs = slice(g * gw, (g + 1) * gw)
        hg = h_scr[g]
        y_state = jnp.dot(cm, hg.astype(BF16), preferred_element_type=F32) * e_full[:, gs]
        for pr in range(B_HPG // 2):
            ws = []
            for r in (g * B_HPG + 2 * pr, g * B_HPG + 2 * pr + 1):
                seg = acum[:, r:r + 1] - acumt[r:r + 1, :]
                decay = jnp.exp(jnp.where(causal, seg, -jnp.inf))
                ws.append((cb * decay * dtt[r:r + 1, :]).astype(BF16))
            c0 = g * gw + pr * 2 * B_HEAD_DIM
            xp = xs16[:, c0:c0 + 2 * B_HEAD_DIM]
            y0 = jnp.dot(ws[0], xp, preferred_element_type=F32)
            y1 = jnp.dot(ws[1], xp, preferred_element_type=F32)
            yi_scr[:, c0:c0 + 2 * B_HEAD_DIM] = jnp.where(lane < B_HEAD_DIM, y0, y1)
        yi_scr[:, gs] = yi_scr[:, gs] + y_state
        upd = lax.dot_general(bm, xt[:, gs], (((0,), (0,)), ((), ())), preferred_element_type=F32)
        h_scr[g] = hg * e_full[l - 1:l, gs] + upd

    y = (yi_scr[...] + dsk_ref[...] * xs) * zg_ref[0]
    for g in range(B_GROUPS):
        gs = slice(g * gw, (g + 1) * gw)
        yg = y[:, gs]
        yg = yg * lax.rsqrt(jnp.mean(yg * yg, axis=-1, keepdims=True) + EPS)
        y_ref[0, :, gs] = (yg * nw_ref[:, gs]).astype(y_ref.dtype)

    @pl.when(c == pl.num_programs(1) - 1)
    def _():
        hout_ref[0] = h_scr[...]


def _ssd(name, zg, xbc, dt_raw, dt_bias, a_log, d_skip, norm_w, h0):
    b, t, _ = zg.shape
    nc = t // CHUNK
    gw = B_WIDTH // B_GROUPS
    dt4 = dt_raw.reshape(b, nc, CHUNK, B_HEADS)
    dtt4 = jnp.swapaxes(dt4, 2, 3)
    h0t = jnp.transpose(h0.reshape(b, B_GROUPS, B_HPG, B_HEAD_DIM, B_STATE), (0, 1, 4, 2, 3)).reshape(b, B_GROUPS, B_STATE, gw)
    expand = jnp.repeat(jnp.eye(B_HEADS, dtype=BF16), B_HEAD_DIM, axis=1)
    dsk = jnp.repeat(d_skip, B_HEAD_DIM).reshape(1, B_WIDTH)
    row = lambda v: v.reshape(1, -1)
    colv = lambda v: v.reshape(-1, 1)
    const2 = lambda bi, ci: (0, 0)
    y, hout = pl.pallas_call(
        _ssd_body,
        grid=(b, nc),
        in_specs=[
            pl.BlockSpec((1, CHUNK, B_WIDTH), lambda bi, ci: (bi, ci, 0)),
            pl.BlockSpec((1, CHUNK, B_CONV_DIM), lambda bi, ci: (bi, ci, 0)),
            pl.BlockSpec((1, 1, CHUNK, B_HEADS), lambda bi, ci: (bi, ci, 0, 0)),
            pl.BlockSpec((1, 1, B_HEADS, CHUNK), lambda bi, ci: (bi, ci, 0, 0)),
            pl.BlockSpec((1, B_HEADS), const2),
            pl.BlockSpec((B_HEADS, 1), const2),
            pl.BlockSpec((1, B_HEADS), const2),
            pl.BlockSpec((B_HEADS, 1), const2),
            pl.BlockSpec((1, B_WIDTH), const2),
            pl.BlockSpec((1, B_WIDTH), const2),
            pl.BlockSpec((B_HEADS, B_WIDTH), const2),
            pl.BlockSpec((1, B_GROUPS, B_STATE, gw), lambda bi, ci: (bi, 0, 0, 0)),
        ],
        out_specs=[pl.BlockSpec((1, CHUNK, B_WIDTH), lambda bi, ci: (bi, ci, 0)),
                   pl.BlockSpec((1, B_GROUPS, B_STATE, gw), lambda bi, ci: (bi, 0, 0, 0))],
        out_shape=[jax.ShapeDtypeStruct((b, t, B_WIDTH), BF16),
                   jax.ShapeDtypeStruct((b, B_GROUPS, B_STATE, gw), F32)],
        scratch_shapes=[pltpu.VMEM((B_GROUPS, B_STATE, gw), F32),
                        pltpu.VMEM((CHUNK, B_WIDTH), F32)],
        compiler_params=_params(("parallel", "arbitrary")),
        name=name,
    )(zg, xbc, dt4, dtt4, row(dt_bias), colv(dt_bias), row(a_log), colv(a_log),
      dsk, row(norm_w), expand, h0t)
    hnew = jnp.transpose(hout.reshape(b, B_GROUPS, B_STATE, B_HPG, B_HEAD_DIM), (0, 1, 3, 4, 2))
    return y, hnew.reshape(b, B_HEADS, B_HEAD_DIM, B_STATE)


def _mla_body(q_ref, kn_ref, kr_ref, vt_ref, g_ref, o_ref, m_scr, acc_scr, qka_scr, qkb_scr, *,
              tb, qt0, heads, pair_blocks):
    qt = pl.program_id(2) + qt0
    _softmax_init(m_scr, acc_scr, heads, tb)
    krow = lax.broadcasted_iota(I32, (tb, tb), 0)
    qcol = lax.broadcasted_iota(I32, (tb, tb), 1)
    diag_vis = (krow // CHUNK) <= (qcol // CHUNK)

    def tiles(j, nt, masked):
        rows = pl.ds(pl.multiple_of(j * tb, tb), nt * tb)
        kr = kr_ref[0, rows, :]
        qk = [_dot_nt(jnp.concatenate([kn_ref[0, rows, h * QK_NOPE:(h + 1) * QK_NOPE], kr], axis=1),
                      q_ref[0, :, h * 2 * LANES:(h + 1) * 2 * LANES]) for h in range(heads)]
        for h in range(heads):
            s = jnp.where(diag_vis, qk[h], -jnp.inf) if masked else qk[h]
            _softmax_step(s, [vt_ref[0, h, j + i] for i in range(nt)], m_scr, acc_scr, h)

    def pair_logits(p, buf):
        rows = pl.ds(pl.multiple_of(p * 2 * tb, 2 * tb), 2 * tb)
        kr = kr_ref[0, rows, :]
        for h in range(heads):
            kc = jnp.concatenate([kn_ref[0, rows, h * QK_NOPE:(h + 1) * QK_NOPE], kr], axis=1)
            buf[h] = _dot_nt(kc, q_ref[0, :, h * 2 * LANES:(h + 1) * 2 * LANES])

    def pair_softmax(p, buf):
        for h in range(heads):
            _softmax_step(buf[h], [vt_ref[0, h, 2 * p], vt_ref[0, h, 2 * p + 1]], m_scr, acc_scr, h)

    def last_pair_softmax(p, buf):
        kchunk = (p * 2 * tb + lax.broadcasted_iota(I32, (2 * tb, tb), 0)) // CHUNK
        vis = kchunk <= (qt * tb + lax.broadcasted_iota(I32, (2 * tb, tb), 1)) // CHUNK
        for h in range(heads):
            _softmax_step(jnp.where(vis, buf[h], -jnp.inf), [vt_ref[0, h, 2 * p], vt_ref[0, h, 2 * p + 1]],
                          m_scr, acc_scr, h)

    def full_tile(j, carry):
        tiles(j, 1, False)
        return carry

    if pair_blocks:
        _pipelined_blocks_final(qt // 2, pair_logits, pair_softmax, last_pair_softmax, qka_scr, qkb_scr)
    else:
        lax.fori_loop(0, qt, full_tile, 0)
        tiles(qt, 1, True)
    _softmax_finish(acc_scr, g_ref, o_ref, heads)


def _mla(name, q, kn, kr, vt, gate, tb, past_len, heads=8):
    b, t, _ = q.shape
    l = kn.shape[1]
    assert t % tb == 0 and past_len % tb == 0 and l == past_len + t and C_HEADS % heads == 0
    nkv = l // tb
    body = functools.partial(_mla_body, tb=tb, qt0=past_len // tb, heads=heads, pair_blocks=nkv % 2 == 0)
    return pl.pallas_call(
        body,
        grid=(b, C_HEADS // heads, t // tb),
        in_specs=[
            pl.BlockSpec((1, tb, heads * 2 * LANES), lambda bi, h, qi: (bi, qi, h)),
            pl.BlockSpec((1, l, heads * QK_NOPE), lambda bi, h, qi: (bi, 0, h)),
            pl.BlockSpec((1, l, LANES), lambda bi, h, qi: (bi, 0, 0)),
            pl.BlockSpec((1, heads, nkv, VT_ROWS, tb), lambda bi, h, qi: (bi, h, 0, 0, 0)),
            pl.BlockSpec((1, tb, heads * V_DIM), lambda bi, h, qi: (bi, qi, h)),
        ],
        out_specs=pl.BlockSpec((1, tb, heads * V_DIM), lambda bi, h, qi: (bi, qi, h)),
        out_shape=jax.ShapeDtypeStruct((b, t, C_WIDTH), BF16),
        scratch_shapes=[pltpu.VMEM((heads, 1, tb), F32), pltpu.VMEM((heads, VT_ROWS, tb), F32),
                        pltpu.VMEM((heads, 2 * tb, tb), F32), pltpu.VMEM((heads, 2 * tb, tb), F32)],
        compiler_params=_params(("parallel", "parallel", "parallel")),
        name=name,
    )(q, kn, kr, vt, gate)


def _even_weights(w_in):
    offs = [0]
    for s in (A_WIDTH, A_WIDTH, A_WIDTH, A_WIDTH, IDX_HEADS * IDX_DIM, IDX_DIM, IDX_HEADS, B_WIDTH, B_CONV_DIM, B_HEADS):
        offs.append(offs[-1] + s)
    cols = [w_in[:, offs[i]:offs[i + 1]] for i in range(10)]
    aq, ak, av, ag, iq, ik, iw, bz, bxbc, bdt = cols
    aq = aq * (A_HEAD_DIM ** -0.5 * LOG2E)
    pad = jnp.zeros((w_in.shape[0], LANES - IDX_DIM - IDX_HEADS - B_HEADS), w_in.dtype)
    small = jnp.concatenate([ik, iw, bdt, pad], axis=1)
    return [c.astype(BF16) for c in (aq, ak, av, ag, iq, small, bz, bxbc)]


def _even_layer(tag, x, past, wts, w_out, conv_w, conv_b, dt_bias, a_log, d_skip, norm_w, ln_g, ln_b, t5_bias, tb):
    b, t, _ = x.shape
    m = b * t
    x2 = x.reshape(m, D_MODEL)
    w_aq, w_ak, w_av, w_ag, w_iq, w_small, w_bz, w_bxbc = wts
    aq, xb = _mm(tag + "_in_aq", x2, w_aq, [BF16, ("acopy",)])
    ak, ak16 = _mm(tag + "_in_ak", xb, w_ak, [F32, BF16])
    (ag,) = _mm(tag + "_in_ag", xb, w_ag, [("silu", F32)])
    (iqt,) = _mm(tag + "_in_iq", xb, w_iq, [("heads", t, IDX_DIM, BF16)])
    (small,) = _mm(tag + "_in_small", xb, w_small, [F32])
    (bzg,) = _mm(tag + "_in_bz", xb, w_bz, [("silu", F32)])
    ik = small[:, :IDX_DIM].reshape(b, t, IDX_DIM)
    iwt = jnp.swapaxes(small[:, IDX_DIM:IDX_DIM + IDX_HEADS].reshape(b, t, IDX_HEADS), 1, 2)
    bdt = small[:, IDX_DIM + IDX_HEADS:IDX_DIM + IDX_HEADS + B_HEADS].reshape(b, t, B_HEADS)
    k16 = ak16.reshape(b, t, A_WIDTH)
    ik16 = ik.astype(BF16)
    if past is None:
        p_len = 0
        av, vt = _mm(tag + "_in_av", xb, w_av, [F32, ("vt", t, tb, BF16)])
        conv0 = jnp.zeros((b, SUBLANES, B_CONV_DIM), F32)
        h0 = jnp.zeros((b, B_HEADS, B_HEAD_DIM, B_STATE), F32)
    else:
        pk, pv, pki, pconv, pssm = past
        p_len = pk.shape[1]
        av, av16 = _mm(tag + "_in_av", xb, w_av, [F32, BF16])
        k16 = jnp.concatenate([pk.reshape(b, p_len, A_WIDTH).astype(BF16), k16], axis=1)
        vt = _to_vt(jnp.concatenate([pv.reshape(b, p_len, A_WIDTH).astype(BF16), av16.reshape(b, t, A_WIDTH)], axis=1), tb)
        ik16 = jnp.concatenate([pki.astype(BF16), ik16], axis=1)
        conv0 = jnp.pad(pconv, ((0, 0), (SUBLANES - (CONV_W - 1), 0), (0, 0)))
        h0 = pssm
    xbc, tails = _mm_conv(tag + "_in_bxbc", xb, w_bxbc, conv0, conv_w, conv_b, t)
    conv_new = tails.reshape(b, -1, SUBLANES, B_CONV_DIM)[:, -1, SUBLANES - (CONV_W - 1):]
    a_out = _dsa(tag + "_dsa", iqt, iwt, ik16, aq.reshape(b, t, A_WIDTH), k16, vt,
                 ag.reshape(b, t, A_WIDTH), _bias_tables(t5_bias, tb), tb, p_len)
    b_out, ssm_new = _ssd(tag + "_ssd", bzg.reshape(b, t, B_WIDTH), xbc.reshape(b, t, B_CONV_DIM), bdt, dt_bias, a_log,
                          d_skip, norm_w, h0)
    wo = w_out.astype(BF16)
    y, y16 = _oproj_ln(tag + "_out0", [a_out.reshape(m, A_WIDTH), b_out.reshape(m, B_WIDTH)],
                       [wo[:A_WIDTH], wo[A_WIDTH:]], x2, ln_g, ln_b)
    state = (ak.reshape(b, t, A_HEADS, A_HEAD_DIM), av.reshape(b, t, A_HEADS, A_HEAD_DIM), ik, conv_new, ssm_new)
    return y.reshape(b, t, D_MODEL), y16, state


def _rope_rot_cols(w):
    half = QK_ROPE // 2
    return jnp.concatenate([-w[..., half:], w[..., :half]], axis=-1)


def _odd_weights(w_in, w_uq, w_ukv):
    w_cq = w_in[:, :Q_LORA]
    w_ckv = w_in[:, Q_LORA:Q_LORA + KV_LORA]
    w_kr = w_in[:, Q_LORA + KV_LORA:Q_LORA + KV_LORA + QK_ROPE]
    w_gate = w_in[:, Q_LORA + KV_LORA + QK_ROPE:]
    w_kr2 = jnp.concatenate([w_kr, _rope_rot_cols(w_kr)], axis=1)
    uq = w_uq.reshape(Q_LORA, C_HEADS, QK_NOPE + QK_ROPE) * (MLA_SCALE * LOG2E)
    uq_rope = uq[..., QK_NOPE:]
    uq2 = jnp.concatenate([uq[..., :QK_NOPE], uq_rope, _rope_rot_cols(uq_rope)], axis=-1).reshape(Q_LORA, C_HEADS * 2 * LANES)
    ukv = w_ukv.reshape(KV_LORA, C_HEADS, QK_NOPE + V_DIM)
    w_uk = ukv[..., :QK_NOPE].reshape(KV_LORA, C_HEADS * QK_NOPE)
    w_uv = ukv[..., QK_NOPE:].reshape(KV_LORA, C_HEADS * V_DIM)
    return [c.astype(BF16) for c in (w_cq, w_ckv, w_kr2, w_gate, uq2, w_uk, w_uv)]


def _rope_table(pos):
    half = QK_ROPE // 2
    inv = ROPE_THETA ** (-jnp.arange(half, dtype=F32) / half)
    ang = pos.astype(F32)[:, None] * inv[None, :]
    cos, sin = jnp.cos(ang), jnp.sin(ang)
    return jnp.concatenate([cos, cos, sin, sin], axis=1)


def _odd_layer(tag, x, x16, past, wts, q_norm_w, kv_norm_w, w_out, ln_g, ln_b, tb):
    b, t, _ = x.shape
    m = b * t
    w_cq, w_ckv, w_kr2, w_gate, w_uq2, w_uk, w_uv = wts
    p_len = 0 if past is None else past[0].shape[1]
    tab = _rope_table(p_len + jnp.arange(t, dtype=I32))
    (cq16,) = _mm_rms(tag + "_in_cq", x16, w_cq, q_norm_w, [BF16])
    ckv, ckv16 = _mm_rms(tag + "_in_ckv", x16, w_ckv, kv_norm_w, [F32, BF16])
    kr, kr16 = _mm_rope_k(tag + "_in_kr", x16, w_kr2, tab)
    (gate,) = _mm(tag + "_in_gate", x16, w_gate, [("silu", F32)])
    q = _mm_rope_q(tag + "_uq", cq16, w_uq2, tab)
    lat16 = ckv16.reshape(b, t, KV_LORA)
    kr16 = kr16.reshape(b, t, LANES)
    if past is not None:
        lat16 = jnp.concatenate([past[0].astype(BF16), lat16], axis=1)
        kr_past = jnp.pad(past[1], ((0, 0), (0, 0), (0, LANES - QK_ROPE))).astype(BF16)
        kr16 = jnp.concatenate([kr_past, kr16], axis=1)
    l = p_len + t
    lat2 = lat16.reshape(b * l, KV_LORA)
    wide = C_HEADS * QK_NOPE
    (kn,) = _mm(tag + "_uk", lat2, w_uk, [BF16], tn=wide)
    if tb % LANES == 0:
        (vt,) = _mm(tag + "_uv", lat2, w_uv, [("vt", l, tb, BF16)], tn=wide)
    else:
        (v,) = _mm(tag + "_uv", lat2, w_uv, [BF16], tn=wide)
        vt = _to_vt(v.reshape(b, l, C_WIDTH), tb)
    o = _mla(tag + "_mla", q.reshape(b, t, C_HEADS * 2 * LANES), kn.reshape(b, l, C_HEADS * QK_NOPE), kr16,
             vt, gate.reshape(b, t, C_WIDTH), tb, p_len)
    y, _ = _oproj_ln(tag + "_out1", [o.reshape(m, C_WIDTH)], [w_out.astype(BF16)], x.reshape(m, D_MODEL), ln_g, ln_b)
    return y.reshape(b, t, D_MODEL), (ckv.reshape(b, t, KV_LORA), kr.reshape(b, t, QK_ROPE))


def kernel(x_prompt, x_sample, cache_a_k, cache_a_v, cache_a_kidx, state_b_conv, state_b_ssm, cache_c_latent, cache_c_krope, t5_bias, w_in0, w_out0, conv_w, conv_b, dt_bias, a_log, d_skip, ssm_norm_w, ln0_g, ln0_b, w_in1, q_norm_w, kv_norm_w, w_uq, w_ukv, w_out1, ln1_g, ln1_b):
    tb_prompt = 256
    tb_sample = CHUNK
    ew = _even_weights(w_in0[0])
    eprm = (w_out0[0], conv_w[0], conv_b[0], dt_bias[0], a_log[0], d_skip[0], ssm_norm_w[0], ln0_g[0], ln0_b[0], t5_bias)
    yp, yp16, st_p = _even_layer("p0", x_prompt, None, ew, *eprm, tb_prompt)
    past = (cache_a_k[0], cache_a_v[0], cache_a_kidx[0], state_b_conv[0], state_b_ssm[0])
    ys, ys16, st_s = _even_layer("s0", x_sample, past, ew, *eprm, tb_sample)
    ow = _odd_weights(w_in1[0], w_uq[0], w_ukv[0])
    oprm = (q_norm_w[0], kv_norm_w[0], w_out1[0], ln1_g[0], ln1_b[0])
    yp, od_p = _odd_layer("p1", yp, yp16, None, ow, *oprm, tb_prompt)
    ys, od_s = _odd_layer("s1", ys, ys16, (cache_c_latent[0], cache_c_krope[0]), ow, *oprm, tb_sample)
    e = lambda a: a[None]
    return (yp, ys, e(st_p[0]), e(st_s[0]), e(st_p[1]), e(st_s[1]), e(st_p[2]), e(st_s[2]),
            e(st_p[3]), e(st_s[3]), e(st_p[4]), e(st_s[4]), e(od_p[0]), e(od_s[0]), e(od_p[1]), e(od_s[1]))
```

```python
import functools
import math

import jax
import jax.numpy as jnp
from jax import lax
from jax.experimental import pallas as pl
from jax.experimental.pallas import tpu as pltpu

F32 = jnp.float32
BF16 = jnp.bfloat16
I32 = jnp.int32
I16 = jnp.int16

D_MODEL = 2048
CHUNK = 64
A_HEADS = 8
A_HEAD_DIM = 128
A_WIDTH = A_HEADS * A_HEAD_DIM
IDX_HEADS = 16
IDX_DIM = 64
IDX_W_SCALE = (IDX_HEADS * IDX_DIM) ** -0.5
TOPK_MAX = 256
N_BUCKETS = 32
MAX_DISTANCE = 128
B_HEAD_DIM = 64
B_WIDTH = D_MODEL
B_HEADS = B_WIDTH // B_HEAD_DIM
B_GROUPS = 4
B_HPG = B_HEADS // B_GROUPS
B_STATE = 128
CONV_W = 4
B_CONV_DIM = B_WIDTH + 2 * B_GROUPS * B_STATE
C_HEADS = 16
Q_LORA = 512
KV_LORA = 512
QK_NOPE = 128
QK_ROPE = 64
V_DIM = 128
C_WIDTH = C_HEADS * V_DIM
ROPE_THETA = 10000.0
MLA_SCALE = (QK_NOPE + QK_ROPE) ** -0.5
DEPTH = 2
ALPHA = (2 * DEPTH) ** 0.25
EPS = 1e-5

LANES = 128
SUBLANES = 8
VT_ROWS = LANES + 2 * SUBLANES
VMEM_LIMIT = 56 * 1024 * 1024
LOG2E = math.log2(math.e)
INT_MIN = -(2 ** 31)
INT_MAX = 2 ** 31 - 1
KEY_NEG_INF = (0xFF800000 ^ 0x7FFFFFFF) - (1 << 32)
HALF_BITS = 16
HALF_MASK = (1 << HALF_BITS) - 1
HALF_OFFSET = 1 << (HALF_BITS - 1)
HALF_MIN = -HALF_OFFSET
NEG_BIG = -1e30


def _params(sem):
    return pltpu.CompilerParams(dimension_semantics=sem, vmem_limit_bytes=VMEM_LIMIT)


def _dot_nt(a, b):
    return lax.dot_general(a, b, (((1,), (1,)), ((), ())), preferred_element_type=F32)


def _silu(x):
    return x * (1.0 / (1.0 + jnp.exp(-x)))


def _split3(v):
    hi = v.astype(BF16)
    r = v - hi.astype(F32)
    mid = r.astype(BF16)
    return hi, mid, (r - mid.astype(F32)).astype(BF16)


def _dot01(a, b, split_rhs):
    if split_rhs:
        parts = [jnp.dot(a, t, preferred_element_type=F32) for t in _split3(b)]
    else:
        parts = [jnp.dot(t, b, preferred_element_type=F32) for t in _split3(a)]
    return parts[0] + parts[1] + parts[2]


def _mm_body(a_ref, w_ref, *refs, kinds, sub):
    o_refs, bufs = refs[:len(kinds)], refs[len(kinds):]
    tm, tn = a_ref.shape[0], w_ref.shape[1]

    def lhs(rows):
        a = a_ref[rows, :]
        return a if a.dtype == w_ref.dtype else a.astype(w_ref.dtype)

    def write(acc, r0, nr):
        rows = slice(r0, r0 + nr)
        for kind, o in zip(kinds, o_refs):
            if kind[0] == "plain":
                o[rows, :] = acc.astype(o.dtype)
            elif kind[0] == "silu":
                o[rows, :] = _silu(acc).astype(o.dtype)
            elif kind[0] == "acopy":
                @pl.when(pl.program_id(1) == 0)
                def _():
                    o[rows, :] = lhs(rows)
            elif kind[0] == "vt":
                tb = kind[2]
                for hh in range(tn // LANES):
                    for kt in range(nr // tb):
                        tile = acc[kt * tb:(kt + 1) * tb, hh * LANES:(hh + 1) * LANES]
                        o[0, hh, r0 // tb + kt, 0:LANES, :] = tile.T.astype(o.dtype)
                        o[0, hh, r0 // tb + kt, LANES:VT_ROWS, :] = _ones_rows(tb)
            elif kind[0] == "heads":
                hd = kind[2]
                for hh in range(tn // hd):
                    o[0, hh, rows, :] = acc[:, hh * hd:(hh + 1) * hd].astype(o.dtype)

    if sub == tm:
        write(jnp.dot(lhs(slice(0, tm)), w_ref[...], preferred_element_type=F32), 0, tm)
        return
    n_sub = tm // sub

    def matmul(r):
        bufs[r % 2][...] = jnp.dot(lhs(slice(r * sub, (r + 1) * sub)), w_ref[...], preferred_element_type=F32)

    matmul(0)
    for r in range(n_sub):
        if r + 1 < n_sub:
            matmul(r + 1)
        write(bufs[r % 2][...], r * sub, sub)


def _mm_rms_body(a_ref, w_ref, g_ref, *o_refs):
    acc = jnp.dot(a_ref[...], w_ref[...], preferred_element_type=F32)
    y = acc * lax.rsqrt(jnp.mean(acc * acc, axis=-1, keepdims=True) + EPS) * g_ref[...]
    for o in o_refs:
        o[...] = y.astype(o.dtype)


def _rope_half(t):
    return t + pltpu.roll(t, QK_ROPE, 1)


def _mm_rope_k_body(a_ref, w_ref, tab_ref, o32_ref, o16_ref):
    acc = jnp.dot(a_ref[...], w_ref[...], preferred_element_type=F32)
    r = _rope_half(acc * tab_ref[...])
    lane = lax.broadcasted_iota(I32, r.shape, 1)
    o32_ref[...] = r[:, :QK_ROPE]
    o16_ref[...] = jnp.where(lane < QK_ROPE, r, 0.0).astype(o16_ref.dtype)


def _mm_rope_q_body(a_ref, w_ref, tab_ref, o_ref, *, heads):
    acc = jnp.dot(a_ref[...], w_ref[...], preferred_element_type=F32)
    tab = tab_ref[...]
    lane = lax.broadcasted_iota(I32, tab.shape, 1)
    for h in range(heads):
        base = h * 2 * LANES
        o_ref[:, base:base + LANES] = acc[:, base:base + LANES].astype(o_ref.dtype)
        r = _rope_half(acc[:, base + LANES:base + 2 * LANES] * tab)
        o_ref[:, base + LANES:base + 2 * LANES] = jnp.where(lane < QK_ROPE, r, 0.0).astype(o_ref.dtype)


def _mm_call(name, body, a, w, extra, extra_specs, out_cols, out_dtypes, tm, tn):
    m, k = a.shape
    n = w.shape[1]
    tm = math.gcd(tm, m)
    tn = min(tn, n)
    assert tm % SUBLANES == 0 and n % tn == 0, (m, n, tm, tn)
    oc = [tn if c is None else c for c in out_cols]
    return pl.pallas_call(
        body,
        grid=(m // tm, n // tn),
        in_specs=[pl.BlockSpec((tm, k), lambda i, j: (i, 0)),
                  pl.BlockSpec((k, tn), lambda i, j: (0, j))] + extra_specs(tm, tn),
        out_specs=[pl.BlockSpec((tm, c), lambda i, j: (i, j)) for c in oc],
        out_shape=[jax.ShapeDtypeStruct((m, (n // tn) * c), d) for c, d in zip(oc, out_dtypes)],
        compiler_params=_params(("parallel", "parallel")),
        name=name,
    )(a, w, *extra)


def _mm(name, a, w, outs, tm=1024, tn=512):
    m, k = a.shape
    n = w.shape[1]
    tm = math.gcd(tm, m, *[o[1] for o in outs if isinstance(o, tuple) and o[0] in ("vt", "heads")])
    tn = min(tn, n)
    assert tm % SUBLANES == 0 and n % tn == 0, (m, n, tm, tn)
    kinds, specs, shapes = [], [], []
    for o in outs:
        o = o if isinstance(o, tuple) else ("plain", o)
        if o[0] in ("plain", "silu"):
            kinds.append((o[0],))
            specs.append(pl.BlockSpec((tm, tn), lambda i, j: (i, j)))
            shapes.append(jax.ShapeDtypeStruct((m, n), o[1]))
        elif o[0] == "acopy":
            kinds.append(("acopy",))
            specs.append(pl.BlockSpec((tm, k), lambda i, j: (i, 0)))
            shapes.append(jax.ShapeDtypeStruct((m, k), w.dtype))
        elif o[0] == "vt":
            _, t, tb, dt = o
            assert t % tm == 0 and tm % tb == 0 and tn % LANES == 0
            per = t // tm
            kinds.append(("vt", t, tb))
            specs.append(pl.BlockSpec((1, tn // LANES, tm // tb, VT_ROWS, tb),
                                      lambda i, j: (i // per, j, i % per, 0, 0)))
            shapes.append(jax.ShapeDtypeStruct((m // t, n // LANES, t // tb, VT_ROWS, tb), dt))
        elif o[0] == "heads":
            _, t, hd, dt = o
            assert t % tm == 0 and tn % hd == 0
            per = t // tm
            kinds.append(("heads", t, hd))
            specs.append(pl.BlockSpec((1, tn // hd, tm, hd), lambda i, j: (i // per, j, i % per, 0)))
            shapes.append(jax.ShapeDtypeStruct((m // t, n // hd, t, hd), dt))
        else:
            raise ValueError(o)
    busy = any(kd[0] in ("silu", "vt", "heads") for kd in kinds)
    sub = math.gcd(tm, 2 * LANES) if busy else tm
    for kd in kinds:
        if kd[0] == "vt":
            sub = max(sub, kd[2])
    sub = sub if tm % sub == 0 else tm
    return pl.pallas_call(
        functools.partial(_mm_body, kinds=tuple(kinds), sub=sub),
        grid=(m // tm, n // tn),
        in_specs=[pl.BlockSpec((tm, k), lambda i, j: (i, 0)), pl.BlockSpec((k, tn), lambda i, j: (0, j))],
        out_specs=specs,
        out_shape=shapes,
        scratch_shapes=[pltpu.VMEM((sub, tn), F32)] * (2 if sub < tm else 0),
        compiler_params=_params(("parallel", "arbitrary")),
        name=name,
    )(a, w)


def _mm_conv_body(a_ref, w_ref, st_ref, cw_ref, cb_ref, o_ref, tail_ref, raw_a, raw_b, prev_scr, *,
                  tiles_per_seq, sub):
    i, j = pl.program_id(0), pl.program_id(1)
    tm, tn = o_ref.shape
    bufs = (raw_a, raw_b)

    @pl.when(i % tiles_per_seq == 0)
    def _():
        raw_a[0:SUBLANES, :] = st_ref[0]

    @pl.when(i % tiles_per_seq != 0)
    def _():
        raw_a[0:SUBLANES, :] = prev_scr[j]

    def matmul(r):
        bufs[r % 2][SUBLANES:SUBLANES + sub, :] = jnp.dot(a_ref[r * sub:(r + 1) * sub, :], w_ref[...],
                                                          preferred_element_type=F32)

    def epilogue(r):
        buf, nxt = bufs[r % 2], bufs[(r + 1) % 2]
        conv = cb_ref[...]
        for tap in range(CONV_W):
            lo = SUBLANES - (CONV_W - 1) + tap
            conv = conv + buf[lo:lo + sub, :] * cw_ref[tap:tap + 1, :]
        o_ref[r * sub:(r + 1) * sub, :] = _silu(conv).astype(o_ref.dtype)
        nxt[0:SUBLANES, :] = buf[sub:sub + SUBLANES, :]

    n_sub = tm // sub
    matmul(0)
    for r in range(n_sub):
        if r + 1 < n_sub:
            matmul(r + 1)
        epilogue(r)
    tail = bufs[n_sub % 2][0:SUBLANES, :]
    prev_scr[j] = tail
    tail_ref[0] = tail


def _mm_conv(name, a, w, state, conv_w, conv_b, t, tm=1024, tn=512):
    m, k = a.shape
    n = w.shape[1]
    tm = math.gcd(tm, t)
    tn = min(tn, n)
    assert n % tn == 0 and tm % SUBLANES == 0
    per = t // tm
    sub = math.gcd(tm, 2 * LANES)
    return pl.pallas_call(
        functools.partial(_mm_conv_body, tiles_per_seq=per, sub=sub),
        grid=(m // tm, n // tn),
        in_specs=[pl.BlockSpec((tm, k), lambda i, j: (i, 0)),
                  pl.BlockSpec((k, tn), lambda i, j: (0, j)),
                  pl.BlockSpec((1, SUBLANES, tn), lambda i, j: (i // per, 0, j)),
                  pl.BlockSpec((CONV_W, tn), lambda i, j: (0, j)),
                  pl.BlockSpec((1, tn), lambda i, j: (0, j))],
        out_specs=[pl.BlockSpec((tm, tn), lambda i, j: (i, j)),
                   pl.BlockSpec((1, SUBLANES, tn), lambda i, j: (i, 0, j))],
        out_shape=[jax.ShapeDtypeStruct((m, n), F32), jax.ShapeDtypeStruct((m // tm, SUBLANES, n), F32)],
        scratch_shapes=[pltpu.VMEM((sub + SUBLANES, tn), F32), pltpu.VMEM((sub + SUBLANES, tn), F32),
                        pltpu.VMEM((n // tn, SUBLANES, tn), F32)],
        compiler_params=_params(("arbitrary", "arbitrary")),
        name=name,
    )(a, w, state, conv_w, conv_b.reshape(1, n))


def _mm_rms(name, a, w, g, out_dtypes, tm=1024):
    n = w.shape[1]
    return _mm_call(name, _mm_rms_body, a, w, [g.reshape(1, n)],
                    lambda tm_, tn_: [pl.BlockSpec((1, n), lambda i, j: (0, 0))],
                    [None] * len(out_dtypes), out_dtypes, tm, n)


def _tab_spec(t_rows):
    def spec(tm, tn):
        nt = t_rows // tm
        return [pl.BlockSpec((tm, LANES), lambda i, j: (i % nt, 0))]
    return spec


def _mm_rope_k(name, a, w, tab, tm=1024):
    tm = min(tm, tab.shape[0])
    return _mm_call(name, _mm_rope_k_body, a, w, [tab], _tab_spec(tab.shape[0]), [QK_ROPE, LANES], [F32, BF16],
                    tm, LANES)


def _mm_rope_q(name, a, w, tab, tm=1024, heads_per_block=8):
    tm = min(tm, tab.shape[0])
    body = functools.partial(_mm_rope_q_body, heads=heads_per_block)
    return _mm_call(name, body, a, w, [tab], _tab_spec(tab.shape[0]), [None], [BF16], tm,
                    heads_per_block * 2 * LANES)[0]


def _oproj_body(*refs, n_parts, sub):
    parts = refs[:n_parts]
    ws = refs[n_parts:2 * n_parts]
    x_ref, g_ref, b_ref, o32_ref, o16_ref, buf_a, buf_b = refs[2 * n_parts:]
    bufs = (buf_a, buf_b)
    n_sub = x_ref.shape[0] // sub

    def matmul(r):
        rows = slice(r * sub, (r + 1) * sub)
        acc = ALPHA * x_ref[rows, :]
        for p, w in zip(parts, ws):
            acc = acc + jnp.dot(p[rows, :], w[...], preferred_element_type=F32)
        bufs[r % 2][...] = acc

    def layer_norm(r):
        rows = slice(r * sub, (r + 1) * sub)
        acc = bufs[r % 2][...]
        mu = jnp.mean(acc, axis=-1, keepdims=True)
        d = acc - mu
        var = jnp.mean(d * d, axis=-1, keepdims=True)
        y = d * lax.rsqrt(var + EPS) * g_ref[...] + b_ref[...]
        o32_ref[rows, :] = y
        o16_ref[rows, :] = y.astype(o16_ref.dtype)

    matmul(0)
    for r in range(n_sub):
        if r + 1 < n_sub:
            matmul(r + 1)
        layer_norm(r)


def _oproj_ln(name, parts, ws, x, g, b, tm=512):
    m, n = x.shape
    tm = min(tm, m)
    sub = math.gcd(tm, 2 * LANES)
    np_ = len(parts)
    const = lambda i: (0, 0)
    return pl.pallas_call(
        functools.partial(_oproj_body, n_parts=np_, sub=sub),
        scratch_shapes=[pltpu.VMEM((sub, n), F32)] * 2,
        grid=(m // tm,),
        in_specs=[pl.BlockSpec((tm, p.shape[1]), lambda i: (i, 0)) for p in parts]
        + [pl.BlockSpec(w.shape, const, pipeline_mode=pl.Buffered(1)) for w in ws]
        + [pl.BlockSpec((tm, n), lambda i: (i, 0)),
           pl.BlockSpec((1, n), const), pl.BlockSpec((1, n), const)],
        out_specs=[pl.BlockSpec((tm, n), lambda i: (i, 0))] * 2,
        out_shape=[jax.ShapeDtypeStruct((m, n), F32), jax.ShapeDtypeStruct((m, n), BF16)],
        compiler_params=_params(("parallel",)),
        name=name,
    )(*parts, *ws, x, g.reshape(1, n), b.reshape(1, n))


def _ones_rows(tb):
    return jnp.ones((VT_ROWS - LANES, tb), BF16)


def _to_vt(v, tb):
    b, l, hd = v.shape
    h = hd // LANES
    vt = jnp.transpose(v.reshape(b, l // tb, tb, h, LANES), (0, 3, 1, 4, 2))
    return jnp.concatenate([vt, jnp.ones(vt.shape[:3] + (VT_ROWS - LANES, tb), vt.dtype)], axis=3)


def _softmax_init(m_scr, acc_scr, heads, tb):
    for h in range(heads):
        m_scr[h] = jnp.full((1, tb), NEG_BIG, F32)
        acc_scr[h] = jnp.zeros(acc_scr.shape[1:], F32)


def _softmax_step(s, vts, m_scr, acc_scr, h):
    m_prev = m_scr[h]
    m_new = jnp.maximum(m_prev, jnp.max(s, axis=0, keepdims=True))
    alpha = jnp.exp2(m_prev - m_new)
    p = jnp.exp2(s - m_new)
    pb = p.astype(BF16)
    tk = s.shape[0] // len(vts)
    acc = alpha * acc_scr[h]
    for i, vt in enumerate(vts):
        acc = acc + jnp.dot(vt, pb[i * tk:(i + 1) * tk], preferred_element_type=F32)
    acc_scr[h] = acc
    m_scr[h] = m_new


def _pipelined_blocks(n_blocks, logits_fn, consume_fn, buf_a, buf_b):
    @pl.when(n_blocks > 0)
    def _():
        logits_fn(0, buf_a)

    def two_blocks(u, carry):
        i = 2 * u
        logits_fn(i + 1, buf_b)
        consume_fn(i, buf_a)
        logits_fn(jnp.minimum(i + 2, n_blocks - 1), buf_a)
        consume_fn(i + 1, buf_b)
        return carry

    lax.fori_loop(0, n_blocks // 2, two_blocks, 0)

    @pl.when(n_blocks % 2 == 1)
    def _():
        consume_fn(n_blocks - 1, buf_a)


def _pipelined_blocks_final(n_plain, logits_fn, consume_fn, final_fn, buf_a, buf_b):
    logits_fn(0, buf_a)

    def two_blocks(u, carry):
        i = 2 * u
        logits_fn(i + 1, buf_b)
        consume_fn(i, buf_a)
        logits_fn(i + 2, buf_a)
        consume_fn(i + 1, buf_b)
        return carry

    lax.fori_loop(0, n_plain // 2, two_blocks, 0)

    @pl.when(n_plain % 2 == 0)
    def _():
        final_fn(n_plain, buf_a)

    @pl.when(n_plain % 2 == 1)
    def _():
        logits_fn(n_plain, buf_b)
        consume_fn(n_plain - 1, buf_a)
        final_fn(n_plain, buf_b)


def _softmax_finish(acc_scr, g_ref, o_ref, heads):
    for h in range(heads):
        hs = slice(h * LANES, (h + 1) * LANES)
        o = (acc_scr[h, 0:LANES, :] / acc_scr[h, LANES:LANES + 1, :]).T
        o_ref[0, :, hs] = (o * g_ref[0, :, hs]).astype(o_ref.dtype)


def _t5_bucket(rel):
    half = N_BUCKETS // 2
    max_exact = half // 2
    ret = jnp.where(rel < 0, half, 0)
    n = jnp.abs(rel)
    nf = jnp.maximum(n, 1).astype(F32)
    large = max_exact + (jnp.log(nf / max_exact) / math.log(MAX_DISTANCE / max_exact) * (half - max_exact)).astype(jnp.int32)
    large = jnp.minimum(large, half - 1)
    return ret + jnp.where(n < max_exact, n, large)


def _num_special_tiles(tb):
    return (MAX_DISTANCE - 2 + 2 * tb) // tb


def _bias_tables(t5_bias, tb):
    ns = _num_special_tiles(tb)
    s = jnp.arange(tb, dtype=I32)[:, None]
    t = jnp.arange(tb, dtype=I32)[None, :]
    bucket = jnp.stack([_t5_bucket(tb * d + t - s) for d in range(ns)])
    far = t5_bias[_t5_bucket(jnp.int32(tb * ns))]
    out = jnp.zeros((ns, A_HEADS, tb, tb), F32)
    for k in range(N_BUCKETS):
        out = jnp.where(bucket[:, None] == k, t5_bias[k][None, :, None, None], out)
    return (out - far[None, :, None, None]) * LOG2E


def _dsa_body(iq_ref, iwt_ref, ik_ref, q_ref, k_ref, vt_ref, g_ref, bias_ref, o_ref,
              key_scr, hi_scr, lo_scr, mb_scr, x_scr, m_scr, acc_scr, qka_scr, qkb_scr, *,
              tb, qt0, ns, n_sel, idx_bits):
    qt = pl.program_id(1) + qt0
    nkv = qt + 1
    krow = lax.broadcasted_iota(I32, (tb, tb), 0)
    qcol = lax.broadcasted_iota(I32, (tb, tb), 1)
    diag_vis = (krow // CHUNK) <= (qcol // CHUNK)

    wt = iwt_ref[0] * IDX_W_SCALE

    def idx_tile(j, carry):
        kt = ik_ref[0, pl.ds(pl.multiple_of(j * tb, tb), tb), :]
        acc = jnp.zeros((tb, tb), F32)
        for h in range(IDX_HEADS):
            acc = acc + jnp.maximum(_dot_nt(kt, iq_ref[0, h]), 0.0) * wt[h:h + 1, :]
        acc = jnp.where(acc == 0.0, 0.0, acc)
        s = jnp.where(jnp.logical_or(diag_vis, j < qt), acc, -jnp.inf)
        bits = pltpu.bitcast(s, I32)
        key = bits ^ ((bits >> 31) & 0x7FFFFFFF)
        key_scr[j] = key
        hi_scr[j] = (key >> HALF_BITS).astype(I16)
        lo_scr[j] = ((key & HALF_MASK) - HALF_OFFSET).astype(I16)
        return carry

    lax.fori_loop(0, nkv, idx_tile, 0)

    def count(pred):
        def body(j, c):
            f = jnp.where(pred(key_scr[j], j), 1.0, 0.0)
            return c + jnp.sum(f.reshape(tb // SUBLANES, SUBLANES, tb), axis=0)
        c = lax.fori_loop(0, nkv, body, jnp.zeros((SUBLANES, tb), F32))
        return jnp.sum(c, axis=0, keepdims=True)

    pack = 2 * SUBLANES
    one16, zero16 = jnp.ones((), I16), jnp.zeros((), I16)

    def count16(ref, cand, strict):
        c16 = cand.astype(I16)

        def tile_count(j):
            v = ref[j]
            f = jnp.where(v > c16 if strict else v >= c16, one16, zero16).reshape(tb // pack, pack, tb)
            part = f[0]
            for r in range(1, tb // pack):
                part = part + f[r]
            return part

        def two_tiles(u, c):
            return c + (tile_count(2 * u) + tile_count(2 * u + 1)).astype(I32)

        def one_tile(j, c):
            return c + tile_count(j).astype(I32)

        c = lax.fori_loop(0, nkv // 2, two_tiles, jnp.zeros((pack, tb), I32))
        c = lax.fori_loop(2 * (nkv // 2), nkv, one_tile, c)
        return jnp.sum(c, axis=0, keepdims=True)

    def search16(ref, target):
        v0 = jnp.where(count16(ref, jnp.zeros((1, tb), I32), False) >= target, 0, HALF_MIN).astype(I32)

        def bit_body(i, v):
            cand = v + jnp.left_shift(jnp.int32(1), HALF_BITS - 2 - i)
            return jnp.where(count16(ref, cand, False) >= target, cand, v)
        return lax.fori_loop(0, HALF_BITS - 1, bit_body, v0)

    thr_hi = search16(hi_scr, n_sel)
    need_lo = n_sel - count16(hi_scr, thr_hi, True)
    hi16 = thr_hi.astype(I16)

    def band_tile(j, carry):
        hi_scr[j] = jnp.where(hi_scr[j] == hi16, lo_scr[j], jnp.full((), HALF_MIN, I16))
        return carry

    lax.fori_loop(0, nkv, band_tile, 0)
    thr_lo = search16(hi_scr, need_lo)
    thr = thr_hi * (HALF_MASK + 1) + (thr_lo + HALF_OFFSET)
    nsel = float(n_sel)

    need = nsel - count(lambda kt, j: kt > thr)
    excess = jnp.logical_and(count(lambda kt, j: kt == thr) > need, thr > KEY_NEG_INF)
    x_scr[...] = jnp.full((1, tb), INT_MAX, I32)

    @pl.when(jnp.max(jnp.where(excess, 1.0, 0.0)) > 0.0)
    def _():
        def xbit(i, x):
            cand = x + jnp.left_shift(jnp.int32(1), idx_bits - 1 - i)
            c = count(lambda kt, j: jnp.logical_and(kt == thr, krow + j * tb < cand))
            return jnp.where(c < need, cand, x)
        x = lax.fori_loop(0, idx_bits, xbit, jnp.zeros((1, tb), I32))
        x_scr[...] = jnp.where(excess, x, INT_MAX)

    xcut = x_scr[...]

    def mask_tile(j, carry):
        kt = key_scr[j]
        tie = jnp.logical_and(kt == thr, krow + j * tb <= xcut)
        sel = jnp.logical_and(jnp.logical_or(kt > thr, tie), kt != KEY_NEG_INF)
        mb_scr[j] = jnp.where(sel, 0.0, -jnp.inf)
        return carry

    lax.fori_loop(0, nkv, mask_tile, 0)

    _softmax_init(m_scr, acc_scr, A_HEADS, tb)

    def attn_tiles(j, nt, biased):
        rows = pl.ds(pl.multiple_of(j * tb, tb), nt * tb)
        mb = mb_scr[pl.ds(j, nt)].reshape(nt * tb, tb)
        qk = [_dot_nt(k_ref[0, rows, h * A_HEAD_DIM:(h + 1) * A_HEAD_DIM],
                      q_ref[0, :, h * A_HEAD_DIM:(h + 1) * A_HEAD_DIM]) for h in range(A_HEADS)]
        for h in range(A_HEADS):
            s = mb + qk[h]
            if biased:
                s = bias_ref[qt - j, h] + s
            _softmax_step(s, [vt_ref[0, h, j + i] for i in range(nt)], m_scr, acc_scr, h)

    first = (qt + 1) % 2 if ns == 2 else 0

    def pair_logits(p, buf):
        rows = pl.ds(pl.multiple_of((first + 2 * p) * tb, tb), 2 * tb)
        for h in range(A_HEADS):
            hs = slice(h * A_HEAD_DIM, (h + 1) * A_HEAD_DIM)
            buf[h] = _dot_nt(k_ref[0, rows, hs], q_ref[0, :, hs])

    def pair_softmax(p, buf, biased=False):
        j = first + 2 * p
        mb = mb_scr[pl.ds(j, 2)].reshape(2 * tb, tb)
        for h in range(A_HEADS):
            s = mb + buf[h]
            if biased:
                s = jnp.concatenate([bias_ref[1, h], bias_ref[0, h]], axis=0) + s
            _softmax_step(s, [vt_ref[0, h, j], vt_ref[0, h, j + 1]], m_scr, acc_scr, h)

    def far_tile(j, carry):
        attn_tiles(j, 1, False)
        return carry

    def near_tile(j, carry):
        attn_tiles(j, 1, True)
        return carry

    if ns == 2:
        @pl.when(qt == 0)
        def _():
            attn_tiles(0, 1, True)

        @pl.when(jnp.logical_and(qt > 0, first == 1))
        def _():
            attn_tiles(0, 1, False)

        @pl.when(qt > 0)
        def _():
            _pipelined_blocks_final((qt + 1 - first) // 2 - 1, pair_logits, pair_softmax,
                                    functools.partial(pair_softmax, biased=True), qka_scr, qkb_scr)
    else:
        n_far = jnp.maximum(qt - (ns - 1), 0)
        n_pair = n_far // 2
        _pipelined_blocks(n_pair, pair_logits, pair_softmax, qka_scr, qkb_scr)
        lax.fori_loop(2 * n_pair, n_far, far_tile, 0)
        lax.fori_loop(n_far, nkv, near_tile, 0)
    _softmax_finish(acc_scr, g_ref, o_ref, A_HEADS)


def _dsa(name, iq, iwt, ik, q, k, vt, gate, bias_tabs, tb, past_len):
    b, t, aw = q.shape
    l = k.shape[1]
    assert t % tb == 0 and l % tb == 0 and past_len % tb == 0 and l == past_len + t
    nq, nkv = t // tb, l // tb
    ns = _num_special_tiles(tb)
    n_sel = min(TOPK_MAX, l // 4)
    body = functools.partial(_dsa_body, tb=tb, qt0=past_len // tb, ns=ns, n_sel=n_sel,
                             idx_bits=max(1, (l - 1).bit_length()))
    once = pl.Buffered(1)
    return pl.pallas_call(
        body,
        grid=(b, nq),
        in_specs=[
            pl.BlockSpec((1, IDX_HEADS, tb, IDX_DIM), lambda bi, qi: (bi, 0, qi, 0)),
            pl.BlockSpec((1, IDX_HEADS, tb), lambda bi, qi: (bi, 0, qi)),
            pl.BlockSpec((1, l, IDX_DIM), lambda bi, qi: (bi, 0, 0), pipeline_mode=once),
            pl.BlockSpec((1, tb, aw), lambda bi, qi: (bi, qi, 0)),
            pl.BlockSpec((1, l, aw), lambda bi, qi: (bi, 0, 0), pipeline_mode=once),
            pl.BlockSpec((1, A_HEADS, nkv, VT_ROWS, tb), lambda bi, qi: (bi, 0, 0, 0, 0), pipeline_mode=once),
            pl.BlockSpec((1, tb, aw), lambda bi, qi: (bi, qi, 0)),
            pl.BlockSpec(bias_tabs.shape, lambda bi, qi: (0, 0, 0, 0), pipeline_mode=once),
        ],
        out_specs=pl.BlockSpec((1, tb, aw), lambda bi, qi: (bi, qi, 0)),
        out_shape=jax.ShapeDtypeStruct((b, t, aw), BF16),
        scratch_shapes=[
            pltpu.VMEM((nkv, tb, tb), I32),
            pltpu.VMEM((nkv, tb, tb), I16),
            pltpu.VMEM((nkv, tb, tb), I16),
            pltpu.VMEM((nkv, tb, tb), F32),
            pltpu.VMEM((1, tb), I32),
            pltpu.VMEM((A_HEADS, 1, tb), F32),
            pltpu.VMEM((A_HEADS, VT_ROWS, tb), F32),
            pltpu.VMEM((A_HEADS, 2 * tb, tb), F32),
            pltpu.VMEM((A_HEADS, 2 * tb, tb), F32),
        ],
        compiler_params=_params(("parallel", "parallel")),
        name=name,
    )(iq, iwt, ik, q, k, vt, gate, bias_tabs)


def _ssd_body(zg_ref, xbc_ref, dt_ref, dtt_ref, dtb_ref, dtbt_ref, alog_ref, alogt_ref,
              dsk_ref, nw_ref, exp_ref, h0_ref, y_ref, hout_ref, h_scr, yi_scr):
    c = pl.program_id(1)
    l = CHUNK
    gw = B_WIDTH // B_GROUPS

    @pl.when(c == 0)
    def _():
        h_scr[...] = h0_ref[0]

    xbc = xbc_ref[0]
    xs = xbc[:, :B_WIDTH]

    dt = jax.nn.softplus(dt_ref[0, 0] + dtb_ref[...])
    dtt = jax.nn.softplus(dtt_ref[0, 0] + dtbt_ref[...])
    a = -jnp.exp(alog_ref[...])
    at = -jnp.exp(alogt_ref[...])
    ti = lax.broadcasted_iota(I32, (l, l), 0)
    si = lax.broadcasted_iota(I32, (l, l), 1)
    causal = si <= ti
    lower = jnp.where(causal, 1.0, 0.0).astype(BF16)
    upper = jnp.where(ti <= si, 1.0, 0.0).astype(BF16)
    acum = _dot01(lower, dt * a, split_rhs=True)
    acumt = _dot01(dtt * at, upper, split_rhs=False)
    a_last = acum[l - 1:l, :]
    expand = exp_ref[...]
    e_full = _dot01(jnp.exp(acum), expand, split_rhs=False)
    tail_full = _dot01(jnp.exp(a_last - acum) * dt, expand, split_rhs=False)
    xt = (xs * tail_full).astype(BF16)
    xs16 = xs.astype(BF16)
    lane = lax.broadcasted_iota(I32, (l, 2 * B_HEAD_DIM), 1)

    for g in range(B_GROUPS):
        bm = xbc[:, B_WIDTH + g * B_STATE:B_WIDTH + (g + 1) * B_STATE].astype(BF16)
        cm = xbc[:, B_WIDTH + (B_GROUPS + g) * B_STATE:B_WIDTH + (B_GROUPS + g + 1) * B_STATE].astype(BF16)
        cb = _dot_nt(cm, bm)
        gs = slice(g * gw, (g + 1) * gw)
        hg = h_scr[g]
        y_state = jnp.dot(cm, hg.astype(BF16), preferred_element_type=F32) * e_full[:, gs]
        for pr in range(B_HPG // 2):
            ws = []
            for r in (g * B_HPG + 2 * pr, g * B_HPG + 2 * pr + 1):
                seg = acum[:, r:r + 1] - acumt[r:r + 1, :]
                decay = jnp.exp(jnp.where(causal, seg, -jnp.inf))
                ws.append((cb * decay * dtt[r:r + 1, :]).astype(BF16))
            c0 = g * gw + pr * 2 * B_HEAD_DIM
            xp = xs16[:, c0:c0 + 2 * B_HEAD_DIM]
            y0 = jnp.dot(ws[0], xp, preferred_element_type=F32)
            y1 = jnp.dot(ws[1], xp, preferred_element_type=F32)
            yi_scr[:, c0:c0 + 2 * B_HEAD_DIM] = jnp.where(lane < B_HEAD_DIM, y0, y1)
        yi_scr[:, gs] = yi_scr[:, gs] + y_state
        upd = lax.dot_general(bm, xt[:, gs], (((0,), (0,)), ((), ())), preferred_element_type=F32)
        h_scr[g] = hg * e_full[l - 1:l, gs] + upd

    y = (yi_scr[...] + dsk_ref[...] * xs) * zg_ref[0]
    for g in range(B_GROUPS):
        gs = slice(g * gw, (g + 1) * gw)
        yg = y[:, gs]
        yg = yg * lax.rsqrt(jnp.mean(yg * yg, axis=-1, keepdims=True) + EPS)
        y_ref[0, :, gs] = (yg * nw_ref[:, gs]).astype(y_ref.dtype)

    @pl.when(c == pl.num_programs(1) - 1)
    def _():
        hout_ref[0] = h_scr[...]


def _ssd(name, zg, xbc, dt_raw, dt_bias, a_log, d_skip, norm_w, h0):
    b, t, _ = zg.shape
    nc = t // CHUNK
    gw = B_WIDTH // B_GROUPS
    dt4 = dt_raw.reshape(b, nc, CHUNK, B_HEADS)
    dtt4 = jnp.swapaxes(dt4, 2, 3)
    h0t = jnp.transpose(h0.reshape(b, B_GROUPS, B_HPG, B_HEAD_DIM, B_STATE), (0, 1, 4, 2, 3)).reshape(b, B_GROUPS, B_STATE, gw)
    expand = jnp.repeat(jnp.eye(B_HEADS, dtype=BF16), B_HEAD_DIM, axis=1)
    dsk = jnp.repeat(d_skip, B_HEAD_DIM).reshape(1, B_WIDTH)
    row = lambda v: v.reshape(1, -1)
    colv = lambda v: v.reshape(-1, 1)
    const2 = lambda bi, ci: (0, 0)
    y, hout = pl.pallas_call(
        _ssd_body,
        grid=(b, nc),
        in_specs=[
            pl.BlockSpec((1, CHUNK, B_WIDTH), lambda bi, ci: (bi, ci, 0)),
            pl.BlockSpec((1, CHUNK, B_CONV_DIM), lambda bi, ci: (bi, ci, 0)),
            pl.BlockSpec((1, 1, CHUNK, B_HEADS), lambda bi, ci: (bi, ci, 0, 0)),
            pl.BlockSpec((1, 1, B_HEADS, CHUNK), lambda bi, ci: (bi, ci, 0, 0)),
            pl.BlockSpec((1, B_HEADS), const2),
            pl.BlockSpec((B_HEADS, 1), const2),
            pl.BlockSpec((1, B_HEADS), const2),
            pl.BlockSpec((B_HEADS, 1), const2),
            pl.BlockSpec((1, B_WIDTH), const2),
            pl.BlockSpec((1, B_WIDTH), const2),
            pl.BlockSpec((B_HEADS, B_WIDTH), const2),
            pl.BlockSpec((1, B_GROUPS, B_STATE, gw), lambda bi, ci: (bi, 0, 0, 0)),
        ],
        out_specs=[pl.BlockSpec((1, CHUNK, B_WIDTH), lambda bi, ci: (bi, ci, 0)),
                   pl.BlockSpec((1, B_GROUPS, B_STATE, gw), lambda bi, ci: (bi, 0, 0, 0))],
        out_shape=[jax.ShapeDtypeStruct((b, t, B_WIDTH), BF16),
                   jax.ShapeDtypeStruct((b, B_GROUPS, B_STATE, gw), F32)],
        scratch_shapes=[pltpu.VMEM((B_GROUPS, B_STATE, gw), F32),
                        pltpu.VMEM((CHUNK, B_WIDTH), F32)],
        compiler_params=_params(("parallel", "arbitrary")),
        name=name,
    )(zg, xbc, dt4, dtt4, row(dt_bias), colv(dt_bias), row(a_log), colv(a_log),
      dsk, row(norm_w), expand, h0t)
    hnew = jnp.transpose(hout.reshape(b, B_GROUPS, B_STATE, B_HPG, B_HEAD_DIM), (0, 1, 3, 4, 2))
    return y, hnew.reshape(b, B_HEADS, B_HEAD_DIM, B_STATE)


def _mla_body(q_ref, kn_ref, kr_ref, vt_ref, g_ref, o_ref, m_scr, acc_scr, qka_scr, qkb_scr, *,
              tb, qt0, heads, pair_blocks):
    qt = pl.program_id(2) + qt0
    _softmax_init(m_scr, acc_scr, heads, tb)
    krow = lax.broadcasted_iota(I32, (tb, tb), 0)
    qcol = lax.broadcasted_iota(I32, (tb, tb), 1)
    diag_vis = (krow // CHUNK) <= (qcol // CHUNK)

    def tiles(j, nt, masked):
        rows = pl.ds(pl.multiple_of(j * tb, tb), nt * tb)
        kr = kr_ref[0, rows, :]
        qk = [_dot_nt(jnp.concatenate([kn_ref[0, rows, h * QK_NOPE:(h + 1) * QK_NOPE], kr], axis=1),
                      q_ref[0, :, h * 2 * LANES:(h + 1) * 2 * LANES]) for h in range(heads)]
        for h in range(heads):
            s = jnp.where(diag_vis, qk[h], -jnp.inf) if masked else qk[h]
            _softmax_step(s, [vt_ref[0, h, j + i] for i in range(nt)], m_scr, acc_scr, h)

    def pair_logits(p, buf):
        rows = pl.ds(pl.multiple_of(p * 2 * tb, 2 * tb), 2 * tb)
        kr = kr_ref[0, rows, :]
        for h in range(heads):
            kc = jnp.concatenate([kn_ref[0, rows, h * QK_NOPE:(h + 1) * QK_NOPE], kr], axis=1)
            buf[h] = _dot_nt(kc, q_ref[0, :, h * 2 * LANES:(h + 1) * 2 * LANES])

    def pair_softmax(p, buf):
        for h in range(heads):
            _softmax_step(buf[h], [vt_ref[0, h, 2 * p], vt_ref[0, h, 2 * p + 1]], m_scr, acc_scr, h)

    def last_pair_softmax(p, buf):
        kchunk = (p * 2 * tb + lax.broadcasted_iota(I32, (2 * tb, tb), 0)) // CHUNK
        vis = kchunk <= (qt * tb + lax.broadcasted_iota(I32, (2 * tb, tb), 1)) // CHUNK
        for h in range(heads):
            _softmax_step(jnp.where(vis, buf[h], -jnp.inf), [vt_ref[0, h, 2 * p], vt_ref[0, h, 2 * p + 1]],
                          m_scr, acc_scr, h)

    def full_tile(j, carry):
        tiles(j, 1, False)
        return carry

    if pair_blocks:
        _pipelined_blocks_final(qt // 2, pair_logits, pair_softmax, last_pair_softmax, qka_scr, qkb_scr)
    else:
        lax.fori_loop(0, qt, full_tile, 0)
        tiles(qt, 1, True)
    _softmax_finish(acc_scr, g_ref, o_ref, heads)


def _mla(name, q, kn, kr, vt, gate, tb, past_len, heads=8):
    b, t, _ = q.shape
    l = kn.shape[1]
    assert t % tb == 0 and past_len % tb == 0 and l == past_len + t and C_HEADS % heads == 0
    nkv = l // tb
    body = functools.partial(_mla_body, tb=tb, qt0=past_len // tb, heads=heads, pair_blocks=nkv % 2 == 0)
    return pl.pallas_call(
        body,
        grid=(b, C_HEADS // heads, t // tb),
        in_specs=[
            pl.BlockSpec((1, tb, heads * 2 * LANES), lambda bi, h, qi: (bi, qi, h)),
            pl.BlockSpec((1, l, heads * QK_NOPE), lambda bi, h, qi: (bi, 0, h)),
            pl.BlockSpec((1, l, LANES), lambda bi, h, qi: (bi, 0, 0)),
            pl.BlockSpec((1, heads, nkv, VT_ROWS, tb), lambda bi, h, qi: (bi, h, 0, 0, 0)),
            pl.BlockSpec((1, tb, heads * V_DIM), lambda bi, h, qi: (bi, qi, h)),
        ],
        out_specs=pl.BlockSpec((1, tb, heads * V_DIM), lambda bi, h, qi: (bi, qi, h)),
        out_shape=jax.ShapeDtypeStruct((b, t, C_WIDTH), BF16),
        scratch_shapes=[pltpu.VMEM((heads, 1, tb), F32), pltpu.VMEM((heads, VT_ROWS, tb), F32),
                        pltpu.VMEM((heads, 2 * tb, tb), F32), pltpu.VMEM((heads, 2 * tb, tb), F32)],
        compiler_params=_params(("parallel", "parallel", "parallel")),
        name=name,
    )(q, kn, kr, vt, gate)


def _even_weights(w_in):
    offs = [0]
    for s in (A_WIDTH, A_WIDTH, A_WIDTH, A_WIDTH, IDX_HEADS * IDX_DIM, IDX_DIM, IDX_HEADS, B_WIDTH, B_CONV_DIM, B_HEADS):
        offs.append(offs[-1] + s)
    cols = [w_in[:, offs[i]:offs[i + 1]] for i in range(10)]
    aq, ak, av, ag, iq, ik, iw, bz, bxbc, bdt = cols
    aq = aq * (A_HEAD_DIM ** -0.5 * LOG2E)
    pad = jnp.zeros((w_in.shape[0], LANES - IDX_DIM - IDX_HEADS - B_HEADS), w_in.dtype)
    small = jnp.concatenate([ik, iw, bdt, pad], axis=1)
    return [c.astype(BF16) for c in (aq, ak, av, ag, iq, small, bz, bxbc)]


def _even_layer(tag, x, past, wts, w_out, conv_w, conv_b, dt_bias, a_log, d_skip, norm_w, ln_g, ln_b, t5_bias, tb):
    b, t, _ = x.shape
    m = b * t
    x2 = x.reshape(m, D_MODEL)
    w_aq, w_ak, w_av, w_ag, w_iq, w_small, w_bz, w_bxbc = wts
    aq, xb = _mm(tag + "_in_aq", x2, w_aq, [BF16, ("acopy",)])
    ak, ak16 = _mm(tag + "_in_ak", xb, w_ak, [F32, BF16])
    (ag,) = _mm(tag + "_in_ag", xb, w_ag, [("silu", F32)])
    (iqt,) = _mm(tag + "_in_iq", xb, w_iq, [("heads", t, IDX_DIM, BF16)])
    (small,) = _mm(tag + "_in_small", xb, w_small, [F32])
    (bzg,) = _mm(tag + "_in_bz", xb, w_bz, [("silu", F32)])
    ik = small[:, :IDX_DIM].reshape(b, t, IDX_DIM)
    iwt = jnp.swapaxes(small[:, IDX_DIM:IDX_DIM + IDX_HEADS].reshape(b, t, IDX_HEADS), 1, 2)
    bdt = small[:, IDX_DIM + IDX_HEADS:IDX_DIM + IDX_HEADS + B_HEADS].reshape(b, t, B_HEADS)
    k16 = ak16.reshape(b, t, A_WIDTH)
    ik16 = ik.astype(BF16)
    if past is None:
        p_len = 0
        av, vt = _mm(tag + "_in_av", xb, w_av, [F32, ("vt", t, tb, BF16)])
        conv0 = jnp.zeros((b, SUBLANES, B_CONV_DIM), F32)
        h0 = jnp.zeros((b, B_HEADS, B_HEAD_DIM, B_STATE), F32)
    else:
        pk, pv, pki, pconv, pssm = past
        p_len = pk.shape[1]
        av, av16 = _mm(tag + "_in_av", xb, w_av, [F32, BF16])
        k16 = jnp.concatenate([pk.reshape(b, p_len, A_WIDTH).astype(BF16), k16], axis=1)
        vt = _to_vt(jnp.concatenate([pv.reshape(b, p_len, A_WIDTH).astype(BF16), av16.reshape(b, t, A_WIDTH)], axis=1), tb)
        ik16 = jnp.concatenate([pki.astype(BF16), ik16], axis=1)
        conv0 = jnp.pad(pconv, ((0, 0), (SUBLANES - (CONV_W - 1), 0), (0, 0)))
        h0 = pssm
    xbc, tails = _mm_conv(tag + "_in_bxbc", xb, w_bxbc, conv0, conv_w, conv_b, t)
    conv_new = tails.reshape(b, -1, SUBLANES, B_CONV_DIM)[:, -1, SUBLANES - (CONV_W - 1):]
    a_out = _dsa(tag + "_dsa", iqt, iwt, ik16, aq.reshape(b, t, A_WIDTH), k16, vt,
                 ag.reshape(b, t, A_WIDTH), _bias_tables(t5_bias, tb), tb, p_len)
    b_out, ssm_new = _ssd(tag + "_ssd", bzg.reshape(b, t, B_WIDTH), xbc.reshape(b, t, B_CONV_DIM), bdt, dt_bias, a_log,
                          d_skip, norm_w, h0)
    wo = w_out.astype(BF16)
    y, y16 = _oproj_ln(tag + "_out0", [a_out.reshape(m, A_WIDTH), b_out.reshape(m, B_WIDTH)],
                       [wo[:A_WIDTH], wo[A_WIDTH:]], x2, ln_g, ln_b)
    state = (ak.reshape(b, t, A_HEADS, A_HEAD_DIM), av.reshape(b, t, A_HEADS, A_HEAD_DIM), ik, conv_new, ssm_new)
    return y.reshape(b, t, D_MODEL), y16, state


def _rope_rot_cols(w):
    half = QK_ROPE // 2
    return jnp.concatenate([-w[..., half:], w[..., :half]], axis=-1)


def _odd_weights(w_in, w_uq, w_ukv):
    w_cq = w_in[:, :Q_LORA]
    w_ckv = w_in[:, Q_LORA:Q_LORA + KV_LORA]
    w_kr = w_in[:, Q_LORA + KV_LORA:Q_LORA + KV_LORA + QK_ROPE]
    w_gate = w_in[:, Q_LORA + KV_LORA + QK_ROPE:]
    w_kr2 = jnp.concatenate([w_kr, _rope_rot_cols(w_kr)], axis=1)
    uq = w_uq.reshape(Q_LORA, C_HEADS, QK_NOPE + QK_ROPE) * (MLA_SCALE * LOG2E)
    uq_rope = uq[..., QK_NOPE:]
    uq2 = jnp.concatenate([uq[..., :QK_NOPE], uq_rope, _rope_rot_cols(uq_rope)], axis=-1).reshape(Q_LORA, C_HEADS * 2 * LANES)
    ukv = w_ukv.reshape(KV_LORA, C_HEADS, QK_NOPE + V_DIM)
    w_uk = ukv[..., :QK_NOPE].reshape(KV_LORA, C_HEADS * QK_NOPE)
    w_uv = ukv[..., QK_NOPE:].reshape(KV_LORA, C_HEADS * V_DIM)
    return [c.astype(BF16) for c in (w_cq, w_ckv, w_kr2, w_gate, uq2, w_uk, w_uv)]


def _rope_table(pos):
    half = QK_ROPE // 2
    inv = ROPE_THETA ** (-jnp.arange(half, dtype=F32) / half)
    ang = pos.astype(F32)[:, None] * inv[None, :]
    cos, sin = jnp.cos(ang), jnp.sin(ang)
    return jnp.concatenate([cos, cos, sin, sin], axis=1)


def _odd_layer(tag, x, x16, past, wts, q_norm_w, kv_norm_w, w_out, ln_g, ln_b, tb):
    b, t, _ = x.shape
    m = b * t
    w_cq, w_ckv, w_kr2, w_gate, w_uq2, w_uk, w_uv = wts
    p_len = 0 if past is None else past[0].shape[1]
    tab = _rope_table(p_len + jnp.arange(t, dtype=I32))
    (cq16,) = _mm_rms(tag + "_in_cq", x16, w_cq, q_norm_w, [BF16])
    ckv, ckv16 = _mm_rms(tag + "_in_ckv", x16, w_ckv, kv_norm_w, [F32, BF16])
    kr, kr16 = _mm_rope_k(tag + "_in_kr", x16, w_kr2, tab)
    (gate,) = _mm(tag + "_in_gate", x16, w_gate, [("silu", F32)])
    q = _mm_rope_q(tag + "_uq", cq16, w_uq2, tab)
    lat16 = ckv16.reshape(b, t, KV_LORA)
    kr16 = kr16.reshape(b, t, LANES)
    if past is not None:
        lat16 = jnp.concatenate([past[0].astype(BF16), lat16], axis=1)
        kr_past = jnp.pad(past[1], ((0, 0), (0, 0), (0, LANES - QK_ROPE))).astype(BF16)
        kr16 = jnp.concatenate([kr_past, kr16], axis=1)
    l = p_len + t
    lat2 = lat16.reshape(b * l, KV_LORA)
    wide = C_HEADS * QK_NOPE
    (kn,) = _mm(tag + "_uk", lat2, w_uk, [BF16], tn=wide)
    if tb % LANES == 0:
        (vt,) = _mm(tag + "_uv", lat2, w_uv, [("vt", l, tb, BF16)], tn=wide)
    else:
        (v,) = _mm(tag + "_uv", lat2, w_uv, [BF16], tn=wide)
        vt = _to_vt(v.reshape(b, l, C_WIDTH), tb)
    o = _mla(tag + "_mla", q.reshape(b, t, C_HEADS * 2 * LANES), kn.reshape(b, l, C_HEADS * QK_NOPE), kr16,
             vt, gate.reshape(b, t, C_WIDTH), tb, p_len)
    y, _ = _oproj_ln(tag + "_out1", [o.reshape(m, C_WIDTH)], [w_out.astype(BF16)], x.reshape(m, D_MODEL), ln_g, ln_b)
    return y.reshape(b, t, D_MODEL), (ckv.reshape(b, t, KV_LORA), kr.reshape(b, t, QK_ROPE))


def kernel(x_prompt, x_sample, cache_a_k, cache_a_v, cache_a_kidx, state_b_conv, state_b_ssm, cache_c_latent, cache_c_krope, t5_bias, w_in0, w_out0, conv_w, conv_b, dt_bias, a_log, d_skip, ssm_norm_w, ln0_g, ln0_b, w_in1, q_norm_w, kv_norm_w, w_uq, w_ukv, w_out1, ln1_g, ln1_b):
    tb_prompt = 256
    tb_sample = CHUNK
    ew = _even_weights(w_in0[0])
    eprm = (w_out0[0], conv_w[0], conv_b[0], dt_bias[0], a_log[0], d_skip[0], ssm_norm_w[0], ln0_g[0], ln0_b[0], t5_bias)
    yp, yp16, st_p = _even_layer("p0", x_prompt, None, ew, *eprm, tb_prompt)
    past = (cache_a_k[0], cache_a_v[0], cache_a_kidx[0], state_b_conv[0], state_b_ssm[0])
    ys, ys16, st_s = _even_layer("s0", x_sample, past, ew, *eprm, tb_sample)
    ow = _odd_weights(w_in1[0], w_uq[0], w_ukv[0])
    oprm = (q_norm_w[0], kv_norm_w[0], w_out1[0], ln1_g[0], ln1_b[0])
    yp, od_p = _odd_layer("p1", yp, yp16, None, ow, *oprm, tb_prompt)
    ys, od_s = _odd_layer("s1", ys, ys16, (cache_c_latent[0], cache_c_krope[0]), ow, *oprm, tb_sample)
    e = lambda a: a[None]
    return (yp, ys, e(st_p[0]), e(st_s[0]), e(st_p[1]), e(st_s[1]), e(st_p[2]), e(st_s[2]),
            e(st_p[3]), e(st_s[3]), e(st_p[4]), e(st_s[4]), e(od_p[0]), e(od_s[0]), e(od_p[1]), e(od_s[1]))
```

```python
import functools
import math

import jax
import jax.numpy as jnp
from jax import lax
from jax.experimental import pallas as pl
from jax.experimental.pallas import tpu as pltpu

F32 = jnp.float32
BF16 = jnp.bfloat16
I32 = jnp.int32
I16 = jnp.int16

D_MODEL = 2048
CHUNK = 64
A_HEADS = 8
A_HEAD_DIM = 128
A_WIDTH = A_HEADS * A_HEAD_DIM
IDX_HEADS = 16
IDX_DIM = 64
IDX_W_SCALE = (IDX_HEADS * IDX_DIM) ** -0.5
TOPK_MAX = 256
N_BUCKETS = 32
MAX_DISTANCE = 128
B_HEAD_DIM = 64
B_WIDTH = D_MODEL
B_HEADS = B_WIDTH // B_HEAD_DIM
B_GROUPS = 4
B_HPG = B_HEADS // B_GROUPS
B_STATE = 128
CONV_W = 4
B_CONV_DIM = B_WIDTH + 2 * B_GROUPS * B_STATE
C_HEADS = 16
Q_LORA = 512
KV_LORA = 512
QK_NOPE = 128
QK_ROPE = 64
V_DIM = 128
C_WIDTH = C_HEADS * V_DIM
ROPE_THETA = 10000.0
MLA_SCALE = (QK_NOPE + QK_ROPE) ** -0.5
DEPTH = 2
ALPHA = (2 * DEPTH) ** 0.25
EPS = 1e-5

LANES = 128
SUBLANES = 8
VT_ROWS = LANES + 2 * SUBLANES
VMEM_LIMIT = 56 * 1024 * 1024
LOG2E = math.log2(math.e)
INT_MIN = -(2 ** 31)
INT_MAX = 2 ** 31 - 1
KEY_NEG_INF = (0xFF800000 ^ 0x7FFFFFFF) - (1 << 32)
HALF_BITS = 16
HALF_MASK = (1 << HALF_BITS) - 1
HALF_OFFSET = 1 << (HALF_BITS - 1)
HALF_MIN = -HALF_OFFSET
NEG_BIG = -1e30


def _params(sem):
    return pltpu.CompilerParams(dimension_semantics=sem, vmem_limit_bytes=VMEM_LIMIT)


def _dot_nt(a, b):
    return lax.dot_general(a, b, (((1,), (1,)), ((), ())), preferred_element_type=F32)


def _silu(x):
    return x * (1.0 / (1.0 + jnp.exp(-x)))


def _split3(v):
    hi = v.astype(BF16)
    r = v - hi.astype(F32)
    mid = r.astype(BF16)
    return hi, mid, (r - mid.astype(F32)).astype(BF16)


def _dot01(a, b, split_rhs):
    if split_rhs:
        parts = [jnp.dot(a, t, preferred_element_type=F32) for t in _split3(b)]
    else:
        parts = [jnp.dot(t, b, preferred_element_type=F32) for t in _split3(a)]
    return parts[0] + parts[1] + parts[2]


def _mm_body(a_ref, w_ref, *refs, kinds, sub):
    o_refs, bufs = refs[:len(kinds)], refs[len(kinds):]
    tm, tn = a_ref.shape[0], w_ref.shape[1]

    def lhs(rows):
        a = a_ref[rows, :]
        return a if a.dtype == w_ref.dtype else a.astype(w_ref.dtype)

    def write(acc, r0, nr):
        rows = slice(r0, r0 + nr)
        for kind, o in zip(kinds, o_refs):
            if kind[0] == "plain":
                o[rows, :] = acc.astype(o.dtype)
            elif kind[0] == "silu":
                o[rows, :] = _silu(acc).astype(o.dtype)
            elif kind[0] == "acopy":
                @pl.when(pl.program_id(1) == 0)
                def _():
                    o[rows, :] = lhs(rows)
            elif kind[0] == "vt":
                tb = kind[2]
                for hh in range(tn // LANES):
                    for kt in range(nr // tb):
                        tile = acc[kt * tb:(kt + 1) * tb, hh * LANES:(hh + 1) * LANES]
                        o[0, hh, r0 // tb + kt, 0:LANES, :] = tile.T.astype(o.dtype)
                        o[0, hh, r0 // tb + kt, LANES:VT_ROWS, :] = _ones_rows(tb)
            elif kind[0] == "heads":
                hd = kind[2]
                for hh in range(tn // hd):
                    o[0, hh, rows, :] = acc[:, hh * hd:(hh + 1) * hd].astype(o.dtype)

    if sub == tm:
        write(jnp.dot(lhs(slice(0, tm)), w_ref[...], preferred_element_type=F32), 0, tm)
        return
    n_sub = tm // sub

    def matmul(r):
        bufs[r % 2][...] = jnp.dot(lhs(slice(r * sub, (r + 1) * sub)), w_ref[...], preferred_element_type=F32)

    matmul(0)
    for r in range(n_sub):
        if r + 1 < n_sub:
            matmul(r + 1)
        write(bufs[r % 2][...], r * sub, sub)


def _mm_rms_body(a_ref, w_ref, g_ref, *o_refs):
    acc = jnp.dot(a_ref[...], w_ref[...], preferred_element_type=F32)
    y = acc * lax.rsqrt(jnp.mean(acc * acc, axis=-1, keepdims=True) + EPS) * g_ref[...]
    for o in o_refs:
        o[...] = y.astype(o.dtype)


def _rope_half(t):
    return t + pltpu.roll(t, QK_ROPE, 1)


def _mm_rope_k_body(a_ref, w_ref, tab_ref, o32_ref, o16_ref):
    acc = jnp.dot(a_ref[...], w_ref[...], preferred_element_type=F32)
    r = _rope_half(acc * tab_ref[...])
    lane = lax.broadcasted_iota(I32, r.shape, 1)
    o32_ref[...] = r[:, :QK_ROPE]
    o16_ref[...] = jnp.where(lane < QK_ROPE, r, 0.0).astype(o16_ref.dtype)


def _mm_rope_q_body(a_ref, w_ref, tab_ref, o_ref, *, heads):
    acc = jnp.dot(a_ref[...], w_ref[...], preferred_element_type=F32)
    tab = tab_ref[...]
    lane = lax.broadcasted_iota(I32, tab.shape, 1)
    for h in range(heads):
        base = h * 2 * LANES
        o_ref[:, base:base + LANES] = acc[:, base:base + LANES].astype(o_ref.dtype)
        r = _rope_half(acc[:, base + LANES:base + 2 * LANES] * tab)
        o_ref[:, base + LANES:base + 2 * LANES] = jnp.where(lane < QK_ROPE, r, 0.0).astype(o_ref.dtype)


def _mm_call(name, body, a, w, extra, extra_specs, out_cols, out_dtypes, tm, tn):
    m, k = a.shape
    n = w.shape[1]
    tm = math.gcd(tm, m)
    tn = min(tn, n)
    assert tm % SUBLANES == 0 and n % tn == 0, (m, n, tm, tn)
    oc = [tn if c is None else c for c in out_cols]
    return pl.pallas_call(
        body,
        grid=(m // tm, n // tn),
        in_specs=[pl.BlockSpec((tm, k), lambda i, j: (i, 0)),
                  pl.BlockSpec((k, tn), lambda i, j: (0, j))] + extra_specs(tm, tn),
        out_specs=[pl.BlockSpec((tm, c), lambda i, j: (i, j)) for c in oc],
        out_shape=[jax.ShapeDtypeStruct((m, (n // tn) * c), d) for c, d in zip(oc, out_dtypes)],
        compiler_params=_params(("parallel", "parallel")),
        name=name,
    )(a, w, *extra)


def _mm(name, a, w, outs, tm=1024, tn=512):
    m, k = a.shape
    n = w.shape[1]
    tm = math.gcd(tm, m, *[o[1] for o in outs if isinstance(o, tuple) and o[0] in ("vt", "heads")])
    tn = min(tn, n)
    assert tm % SUBLANES == 0 and n % tn == 0, (m, n, tm, tn)
    kinds, specs, shapes = [], [], []
    for o in outs:
        o = o if isinstance(o, tuple) else ("plain", o)
        if o[0] in ("plain", "silu"):
            kinds.append((o[0],))
            specs.append(pl.BlockSpec((tm, tn), lambda i, j: (i, j)))
            shapes.append(jax.ShapeDtypeStruct((m, n), o[1]))
        elif o[0] == "acopy":
            kinds.append(("acopy",))
            specs.append(pl.BlockSpec((tm, k), lambda i, j: (i, 0)))
            shapes.append(jax.ShapeDtypeStruct((m, k), w.dtype))
        elif o[0] == "vt":
            _, t, tb, dt = o
            assert t % tm == 0 and tm % tb == 0 and tn % LANES == 0
            per = t // tm
            kinds.append(("vt", t, tb))
            specs.append(pl.BlockSpec((1, tn // LANES, tm // tb, VT_ROWS, tb),
                                      lambda i, j: (i // per, j, i % per, 0, 0)))
            shapes.append(jax.ShapeDtypeStruct((m // t, n // LANES, t // tb, VT_ROWS, tb), dt))
        elif o[0] == "heads":
            _, t, hd, dt = o
            assert t % tm == 0 and tn % hd == 0
            per = t // tm
            kinds.append(("heads", t, hd))
            specs.append(pl.BlockSpec((1, tn // hd, tm, hd), lambda i, j: (i // per, j, i % per, 0)))
            shapes.append(jax.ShapeDtypeStruct((m // t, n // hd, t, hd), dt))
        else:
            raise ValueError(o)
    busy = any(kd[0] in ("silu", "vt", "heads") for kd in kinds)
    sub = math.gcd(tm, 2 * LANES) if busy else tm
    for kd in kinds:
        if kd[0] == "vt":
            sub = max(sub, kd[2])
    sub = sub if tm % sub == 0 else tm
    return pl.pallas_call(
        functools.partial(_mm_body, kinds=tuple(kinds), sub=sub),
        grid=(m // tm, n // tn),
        in_specs=[pl.BlockSpec((tm, k), lambda i, j: (i, 0)), pl.BlockSpec((k, tn), lambda i, j: (0, j))],
        out_specs=specs,
        out_shape=shapes,
        scratch_shapes=[pltpu.VMEM((sub, tn), F32)] * (2 if sub < tm else 0),
        compiler_params=_params(("parallel", "arbitrary")),
        name=name,
    )(a, w)


def _mm_conv_body(a_ref, w_ref, st_ref, cw_ref, cb_ref, o_ref, tail_ref, raw_a, raw_b, prev_scr, *,
                  tiles_per_seq, sub):
    i, j = pl.program_id(0), pl.program_id(1)
    tm, tn = o_ref.shape
    bufs = (raw_a, raw_b)

    @pl.when(i % tiles_per_seq == 0)
    def _():
        raw_a[0:SUBLANES, :] = st_ref[0]

    @pl.when(i % tiles_per_seq != 0)
    def _():
        raw_a[0:SUBLANES, :] = prev_scr[j]

    def matmul(r):
        bufs[r % 2][SUBLANES:SUBLANES + sub, :] = jnp.dot(a_ref[r * sub:(r + 1) * sub, :], w_ref[...],
                                                          preferred_element_type=F32)

    def epilogue(r):
        buf, nxt = bufs[r % 2], bufs[(r + 1) % 2]
        conv = cb_ref[...]
        for tap in range(CONV_W):
            lo = SUBLANES - (CONV_W - 1) + tap
            conv = conv + buf[lo:lo + sub, :] * cw_ref[tap:tap + 1, :]
        o_ref[r * sub:(r + 1) * sub, :] = _silu(conv).astype(o_ref.dtype)
        nxt[0:SUBLANES, :] = buf[sub:sub + SUBLANES, :]

    n_sub = tm // sub
    matmul(0)
    for r in range(n_sub):
        if r + 1 < n_sub:
            matmul(r + 1)
        epilogue(r)
    tail = bufs[n_sub % 2][0:SUBLANES, :]
    prev_scr[j] = tail
    tail_ref[0] = tail


def _mm_conv(name, a, w, state, conv_w, conv_b, t, tm=1024, tn=512):
    m, k = a.shape
    n = w.shape[1]
    tm = math.gcd(tm, t)
    tn = min(tn, n)
    assert n % tn == 0 and tm % SUBLANES == 0
    per = t // tm
    sub = math.gcd(tm, 2 * LANES)
    return pl.pallas_call(
        functools.partial(_mm_conv_body, tiles_per_seq=per, sub=sub),
        grid=(m // tm, n // tn),
        in_specs=[pl.BlockSpec((tm, k), lambda i, j: (i, 0)),
                  pl.BlockSpec((k, tn), lambda i, j: (0, j)),
                  pl.BlockSpec((1, SUBLANES, tn), lambda i, j: (i // per, 0, j)),
                  pl.BlockSpec((CONV_W, tn), lambda i, j: (0, j)),
                  pl.BlockSpec((1, tn), lambda i, j: (0, j))],
        out_specs=[pl.BlockSpec((tm, tn), lambda i, j: (i, j)),
                   pl.BlockSpec((1, SUBLANES, tn), lambda i, j: (i, 0, j))],
        out_shape=[jax.ShapeDtypeStruct((m, n), F32), jax.ShapeDtypeStruct((m // tm, SUBLANES, n), F32)],
        scratch_shapes=[pltpu.VMEM((sub + SUBLANES, tn), F32), pltpu.VMEM((sub + SUBLANES, tn), F32),
                        pltpu.VMEM((n // tn, SUBLANES, tn), F32)],
        compiler_params=_params(("arbitrary", "arbitrary")),
        name=name,
    )(a, w, state, conv_w, conv_b.reshape(1, n))


def _mm_rms(name, a, w, g, out_dtypes, tm=1024):
    n = w.shape[1]
    return _mm_call(name, _mm_rms_body, a, w, [g.reshape(1, n)],
                    lambda tm_, tn_: [pl.BlockSpec((1, n), lambda i, j: (0, 0))],
                    [None] * len(out_dtypes), out_dtypes, tm, n)


def _tab_spec(t_rows):
    def spec(tm, tn):
        nt = t_rows // tm
        return [pl.BlockSpec((tm, LANES), lambda i, j: (i % nt, 0))]
    return spec


def _mm_rope_k(name, a, w, tab, tm=1024):
    tm = min(tm, tab.shape[0])
    return _mm_call(name, _mm_rope_k_body, a, w, [tab], _tab_spec(tab.shape[0]), [QK_ROPE, LANES], [F32, BF16],
                    tm, LANES)


def _mm_rope_q(name, a, w, tab, tm=1024, heads_per_block=8):
    tm = min(tm, tab.shape[0])
    body = functools.partial(_mm_rope_q_body, heads=heads_per_block)
    return _mm_call(name, body, a, w, [tab], _tab_spec(tab.shape[0]), [None], [BF16], tm,
                    heads_per_block * 2 * LANES)[0]


def _oproj_body(*refs, n_parts, sub):
    parts = refs[:n_parts]
    ws = refs[n_parts:2 * n_parts]
    x_ref, g_ref, b_ref, o32_ref, o16_ref, buf_a, buf_b = refs[2 * n_parts:]
    bufs = (buf_a, buf_b)
    n_sub = x_ref.shape[0] // sub

    def matmul(r):
        rows = slice(r * sub, (r + 1) * sub)
        acc = ALPHA * x_ref[rows, :]
        for p, w in zip(parts, ws):
            acc = acc + jnp.dot(p[rows, :], w[...], preferred_element_type=F32)
        bufs[r % 2][...] = acc

    def layer_norm(r):
        rows = slice(r * sub, (r + 1) * sub)
        acc = bufs[r % 2][...]
        mu = jnp.mean(acc, axis=-1, keepdims=True)
        d = acc - mu
        var = jnp.mean(d * d, axis=-1, keepdims=True)
        y = d * lax.rsqrt(var + EPS) * g_ref[...] + b_ref[...]
        o32_ref[rows, :] = y
        o16_ref[rows, :] = y.astype(o16_ref.dtype)

    matmul(0)
    for r in range(n_sub):
        if r + 1 < n_sub:
            matmul(r + 1)
        layer_norm(r)


def _oproj_ln(name, parts, ws, x, g, b, tm=512):
    m, n = x.shape
    tm = min(tm, m)
    sub = math.gcd(tm, 2 * LANES)
    np_ = len(parts)
    const = lambda i: (0, 0)
    return pl.pallas_call(
        functools.partial(_oproj_body, n_parts=np_, sub=sub),
        scratch_shapes=[pltpu.VMEM((sub, n), F32)] * 2,
        grid=(m // tm,),
        in_specs=[pl.BlockSpec((tm, p.shape[1]), lambda i: (i, 0)) for p in parts]
        + [pl.BlockSpec(w.shape, const, pipeline_mode=pl.Buffered(1)) for w in ws]
        + [pl.BlockSpec((tm, n), lambda i: (i, 0)),
           pl.BlockSpec((1, n), const), pl.BlockSpec((1, n), const)],
        out_specs=[pl.BlockSpec((tm, n), lambda i: (i, 0))] * 2,
        out_shape=[jax.ShapeDtypeStruct((m, n), F32), jax.ShapeDtypeStruct((m, n), BF16)],
        compiler_params=_params(("parallel",)),
        name=name,
    )(*parts, *ws, x, g.reshape(1, n), b.reshape(1, n))


def _ones_rows(tb):
    return jnp.ones((VT_ROWS - LANES, tb), BF16)


def _to_vt(v, tb):
    b, l, hd = v.shape
    h = hd // LANES
    return jnp.transpose(v.reshape(b, l // tb, tb, h, LANES), (0, 3, 1, 4, 2))


def _softmax_init(m_scr, acc_scr, heads, tb):
    for h in range(heads):
        m_scr[h] = jnp.full((1, tb), NEG_BIG, F32)
        acc_scr[h] = jnp.zeros(acc_scr.shape[1:], F32)


def _softmax_step(s, vts, m_scr, acc_scr, h):
    m_prev = m_scr[h]
    m_new = jnp.maximum(m_prev, jnp.max(s, axis=0, keepdims=True))
    alpha = jnp.exp2(m_prev - m_new)
    p = jnp.exp2(s - m_new)
    pb = p.astype(BF16)
    tk = s.shape[0] // len(vts)
    acc = alpha * acc_scr[h]
    if vts[0].shape[0] == VT_ROWS:
        for i, vt in enumerate(vts):
            acc = acc + jnp.dot(vt, pb[i * tk:(i + 1) * tk], preferred_element_type=F32)
    else:
        pv = jnp.dot(vts[0], pb[0:tk], preferred_element_type=F32)
        for i in range(1, len(vts)):
            pv = pv + jnp.dot(vts[i], pb[i * tk:(i + 1) * tk], preferred_element_type=F32)
        den = jnp.sum(p, axis=0, keepdims=True)
        acc = acc + jnp.concatenate([pv, jnp.broadcast_to(den, (VT_ROWS - LANES, den.shape[1]))], axis=0)
    acc_scr[h] = acc
    m_scr[h] = m_new


def _pipelined_blocks(n_blocks, logits_fn, consume_fn, buf_a, buf_b):
    @pl.when(n_blocks > 0)
    def _():
        logits_fn(0, buf_a)

    def two_blocks(u, carry):
        i = 2 * u
        logits_fn(i + 1, buf_b)
        consume_fn(i, buf_a)
        logits_fn(jnp.minimum(i + 2, n_blocks - 1), buf_a)
        consume_fn(i + 1, buf_b)
        return carry

    lax.fori_loop(0, n_blocks // 2, two_blocks, 0)

    @pl.when(n_blocks % 2 == 1)
    def _():
        consume_fn(n_blocks - 1, buf_a)


def _pipelined_blocks_final(n_plain, logits_fn, consume_fn, final_fn, buf_a, buf_b):
    logits_fn(0, buf_a)

    def two_blocks(u, carry):
        i = 2 * u
        logits_fn(i + 1, buf_b)
        consume_fn(i, buf_a)
        logits_fn(i + 2, buf_a)
        consume_fn(i + 1, buf_b)
        return carry

    lax.fori_loop(0, n_plain // 2, two_blocks, 0)

    @pl.when(n_plain % 2 == 0)
    def _():
        final_fn(n_plain, buf_a)

    @pl.when(n_plain % 2 == 1)
    def _():
        logits_fn(n_plain, buf_b)
        consume_fn(n_plain - 1, buf_a)
        final_fn(n_plain, buf_b)


def _softmax_finish(acc_scr, g_ref, o_ref, heads):
    for h in range(heads):
        hs = slice(h * LANES, (h + 1) * LANES)
        o = (acc_scr[h, 0:LANES, :] / acc_scr[h, LANES:LANES + 1, :]).T
        o_ref[0, :, hs] = (o * g_ref[0, :, hs]).astype(o_ref.dtype)


def _t5_bucket(rel):
    half = N_BUCKETS // 2
    max_exact = half // 2
    ret = jnp.where(rel < 0, half, 0)
    n = jnp.abs(rel)
    nf = jnp.maximum(n, 1).astype(F32)
    large = max_exact + (jnp.log(nf / max_exact) / math.log(MAX_DISTANCE / max_exact) * (half - max_exact)).astype(jnp.int32)
    large = jnp.minimum(large, half - 1)
    return ret + jnp.where(n < max_exact, n, large)


def _num_special_tiles(tb):
    return (MAX_DISTANCE - 2 + 2 * tb) // tb


def _bias_tables(t5_bias, tb):
    ns = _num_special_tiles(tb)
    s = jnp.arange(tb, dtype=I32)[:, None]
    t = jnp.arange(tb, dtype=I32)[None, :]
    bucket = jnp.stack([_t5_bucket(tb * d + t - s) for d in range(ns)])
    far = t5_bias[_t5_bucket(jnp.int32(tb * ns))]
    out = jnp.zeros((ns, A_HEADS, tb, tb), F32)
    for k in range(N_BUCKETS):
        out = jnp.where(bucket[:, None] == k, t5_bias[k][None, :, None, None], out)
    return (out - far[None, :, None, None]) * LOG2E


def _dsa_body(iq_ref, iwt_ref, ik_ref, q_ref, k_ref, vt_ref, g_ref, bias_ref, o_ref,
              key_scr, hi_scr, lo_scr, mb_scr, x_scr, m_scr, acc_scr, qka_scr, qkb_scr, *,
              tb, qt0, ns, n_sel, idx_bits):
    qt = pl.program_id(1) + qt0
    nkv = qt + 1
    krow = lax.broadcasted_iota(I32, (tb, tb), 0)
    qcol = lax.broadcasted_iota(I32, (tb, tb), 1)
    diag_vis = (krow // CHUNK) <= (qcol // CHUNK)

    wt = iwt_ref[0] * IDX_W_SCALE

    def idx_tile(j, carry):
        kt = ik_ref[0, pl.ds(pl.multiple_of(j * tb, tb), tb), :]
        acc = jnp.zeros((tb, tb), F32)
        for h in range(IDX_HEADS):
            acc = acc + jnp.maximum(_dot_nt(kt, iq_ref[0, h]), 0.0) * wt[h:h + 1, :]
        acc = jnp.where(acc == 0.0, 0.0, acc)
        s = jnp.where(jnp.logical_or(diag_vis, j < qt), acc, -jnp.inf)
        bits = pltpu.bitcast(s, I32)
        key = bits ^ ((bits >> 31) & 0x7FFFFFFF)
        key_scr[j] = key
        hi_scr[j] = (key >> HALF_BITS).astype(I16)
        lo_scr[j] = ((key & HALF_MASK) - HALF_OFFSET).astype(I16)
        return carry

    lax.fori_loop(0, nkv, idx_tile, 0)

    def count(pred):
        def body(j, c):
            f = jnp.where(pred(key_scr[j], j), 1.0, 0.0)
            return c + jnp.sum(f.reshape(tb // SUBLANES, SUBLANES, tb), axis=0)
        c = lax.fori_loop(0, nkv, body, jnp.zeros((SUBLANES, tb), F32))
        return jnp.sum(c, axis=0, keepdims=True)

    pack = 2 * SUBLANES
    one16, zero16 = jnp.ones((), I16), jnp.zeros((), I16)

    def count16(ref, cand, strict):
        c16 = cand.astype(I16)

        def tile_count(j):
            v = ref[j]
            f = jnp.where(v > c16 if strict else v >= c16, one16, zero16).reshape(tb // pack, pack, tb)
            part = f[0]
            for r in range(1, tb // pack):
                part = part + f[r]
            return part

        def two_tiles(u, c):
            return c + (tile_count(2 * u) + tile_count(2 * u + 1)).astype(I32)

        def one_tile(j, c):
            return c + tile_count(j).astype(I32)

        c = lax.fori_loop(0, nkv // 2, two_tiles, jnp.zeros((pack, tb), I32))
        c = lax.fori_loop(2 * (nkv // 2), nkv, one_tile, c)
        return jnp.sum(c, axis=0, keepdims=True)

    def search16(ref, target):
        v0 = jnp.where(count16(ref, jnp.zeros((1, tb), I32), False) >= target, 0, HALF_MIN).astype(I32)

        def bit_body(i, v):
            cand = v + jnp.left_shift(jnp.int32(1), HALF_BITS - 2 - i)
            return jnp.where(count16(ref, cand, False) >= target, cand, v)
        return lax.fori_loop(0, HALF_BITS - 1, bit_body, v0)

    thr_hi = search16(hi_scr, n_sel)
    need_lo = n_sel - count16(hi_scr, thr_hi, True)
    hi16 = thr_hi.astype(I16)

    def band_tile(j, carry):
        hi_scr[j] = jnp.where(hi_scr[j] == hi16, lo_scr[j], jnp.full((), HALF_MIN, I16))
        return carry

    lax.fori_loop(0, nkv, band_tile, 0)
    thr_lo = search16(hi_scr, need_lo)
    thr = thr_hi * (HALF_MASK + 1) + (thr_lo + HALF_OFFSET)
    nsel = float(n_sel)

    need = nsel - count(lambda kt, j: kt > thr)
    excess = jnp.logical_and(count(lambda kt, j: kt == thr) > need, thr > KEY_NEG_INF)
    x_scr[...] = jnp.full((1, tb), INT_MAX, I32)

    @pl.when(jnp.max(jnp.where(excess, 1.0, 0.0)) > 0.0)
    def _():
        def xbit(i, x):
            cand = x + jnp.left_shift(jnp.int32(1), idx_bits - 1 - i)
            c = count(lambda kt, j: jnp.logical_and(kt == thr, krow + j * tb < cand))
            return jnp.where(c < need, cand, x)
        x = lax.fori_loop(0, idx_bits, xbit, jnp.zeros((1, tb), I32))
        x_scr[...] = jnp.where(excess, x, INT_MAX)

    xcut = x_scr[...]

    def mask_tile(j, carry):
        kt = key_scr[j]
        tie = jnp.logical_and(kt == thr, krow + j * tb <= xcut)
        sel = jnp.logical_and(jnp.logical_or(kt > thr, tie), kt != KEY_NEG_INF)
        mb_scr[j] = jnp.where(sel, 0.0, -jnp.inf)
        return carry

    lax.fori_loop(0, nkv, mask_tile, 0)

    _softmax_init(m_scr, acc_scr, A_HEADS, tb)

    def attn_tiles(j, nt, biased):
        rows = pl.ds(pl.multiple_of(j * tb, tb), nt * tb)
        mb = mb_scr[pl.ds(j, nt)].reshape(nt * tb, tb)
        qk = [_dot_nt(k_ref[0, rows, h * A_HEAD_DIM:(h + 1) * A_HEAD_DIM],
                      q_ref[0, :, h * A_HEAD_DIM:(h + 1) * A_HEAD_DIM]) for h in range(A_HEADS)]
        for h in range(A_HEADS):
            s = mb + qk[h]
            if biased:
                s = bias_ref[qt - j, h] + s
            _softmax_step(s, [vt_ref[0, h, j + i] for i in range(nt)], m_scr, acc_scr, h)

    first = (qt + 1) % 2 if ns == 2 else 0

    def pair_logits(p, buf):
        rows = pl.ds(pl.multiple_of((first + 2 * p) * tb, tb), 2 * tb)
        for h in range(A_HEADS):
            hs = slice(h * A_HEAD_DIM, (h + 1) * A_HEAD_DIM)
            buf[h] = _dot_nt(k_ref[0, rows, hs], q_ref[0, :, hs])

    def pair_softmax(p, buf, biased=False):
        j = first + 2 * p
        mb = mb_scr[pl.ds(j, 2)].reshape(2 * tb, tb)
        for h in range(A_HEADS):
            s = mb + buf[h]
            if biased:
                s = jnp.concatenate([bias_ref[1, h], bias_ref[0, h]], axis=0) + s
            _softmax_step(s, [vt_ref[0, h, j], vt_ref[0, h, j + 1]], m_scr, acc_scr, h)

    def far_tile(j, carry):
        attn_tiles(j, 1, False)
        return carry

    def near_tile(j, carry):
        attn_tiles(j, 1, True)
        return carry

    if ns == 2:
        @pl.when(qt == 0)
        def _():
            attn_tiles(0, 1, True)

        @pl.when(jnp.logical_and(qt > 0, first == 1))
        def _():
            attn_tiles(0, 1, False)

        @pl.when(qt > 0)
        def _():
            _pipelined_blocks_final((qt + 1 - first) // 2 - 1, pair_logits, pair_softmax,
                                    functools.partial(pair_softmax, biased=True), qka_scr, qkb_scr)
    else:
        n_far = jnp.maximum(qt - (ns - 1), 0)
        n_pair = n_far // 2
        _pipelined_blocks(n_pair, pair_logits, pair_softmax, qka_scr, qkb_scr)
        lax.fori_loop(2 * n_pair, n_far, far_tile, 0)
        lax.fori_loop(n_far, nkv, near_tile, 0)
    _softmax_finish(acc_scr, g_ref, o_ref, A_HEADS)


def _dsa(name, iq, iwt, ik, q, k, vt, gate, bias_tabs, tb, past_len):
    b, t, aw = q.shape
    l = k.shape[1]
    assert t % tb == 0 and l % tb == 0 and past_len % tb == 0 and l == past_len + t
    nq, nkv = t // tb, l // tb
    ns = _num_special_tiles(tb)
    n_sel = min(TOPK_MAX, l // 4)
    body = functools.partial(_dsa_body, tb=tb, qt0=past_len // tb, ns=ns, n_sel=n_sel,
                             idx_bits=max(1, (l - 1).bit_length()))
    once = pl.Buffered(1)
    return pl.pallas_call(
        body,
        grid=(b, nq),
        in_specs=[
            pl.BlockSpec((1, IDX_HEADS, tb, IDX_DIM), lambda bi, qi: (bi, 0, qi, 0)),
            pl.BlockSpec((1, IDX_HEADS, tb), lambda bi, qi: (bi, 0, qi)),
            pl.BlockSpec((1, l, IDX_DIM), lambda bi, qi: (bi, 0, 0), pipeline_mode=once),
            pl.BlockSpec((1, tb, aw), lambda bi, qi: (bi, qi, 0)),
            pl.BlockSpec((1, l, aw), lambda bi, qi: (bi, 0, 0), pipeline_mode=once),
            pl.BlockSpec((1, A_HEADS, nkv, vt.shape[3], tb), lambda bi, qi: (bi, 0, 0, 0, 0), pipeline_mode=once),
            pl.BlockSpec((1, tb, aw), lambda bi, qi: (bi, qi, 0)),
            pl.BlockSpec(bias_tabs.shape, lambda bi, qi: (0, 0, 0, 0), pipeline_mode=once),
        ],
        out_specs=pl.BlockSpec((1, tb, aw), lambda bi, qi: (bi, qi, 0)),
        out_shape=jax.ShapeDtypeStruct((b, t, aw), BF16),
        scratch_shapes=[
            pltpu.VMEM((nkv, tb, tb), I32),
            pltpu.VMEM((nkv, tb, tb), I16),
            pltpu.VMEM((nkv, tb, tb), I16),
            pltpu.VMEM((nkv, tb, tb), F32),
            pltpu.VMEM((1, tb), I32),
            pltpu.VMEM((A_HEADS, 1, tb), F32),
            pltpu.VMEM((A_HEADS, VT_ROWS, tb), F32),
            pltpu.VMEM((A_HEADS, 2 * tb, tb), F32),
            pltpu.VMEM((A_HEADS, 2 * tb, tb), F32),
        ],
        compiler_params=_params(("parallel", "parallel")),
        name=name,
    )(iq, iwt, ik, q, k, vt, gate, bias_tabs)


def _ssd_multi_body(zg_ref, xbc_ref, dt_ref, dtt_ref, dtb_ref, dtbt_ref, alog_ref, alogt_ref,
                    dsk_ref, nw_ref, exp_ref, h0_ref, y_ref, hout_ref, h_scr, yi_scr, *, nb):
    c = pl.program_id(1)
    l = CHUNK
    gw = B_WIDTH // B_GROUPS

    @pl.when(c == 0)
    def _():
        h_scr[...] = h0_ref[...]

    a = -jnp.exp(alog_ref[...])
    at = -jnp.exp(alogt_ref[...])
    ti = lax.broadcasted_iota(I32, (l, l), 0)
    si = lax.broadcasted_iota(I32, (l, l), 1)
    causal = si <= ti
    lower = jnp.where(causal, 1.0, 0.0).astype(BF16)
    upper = jnp.where(ti <= si, 1.0, 0.0).astype(BF16)
    expand = exp_ref[...]
    lane = lax.broadcasted_iota(I32, (l, 2 * B_HEAD_DIM), 1)

    def prepare(s):
        xbc = xbc_ref[s]
        xs = xbc[:, :B_WIDTH]
        dt = jax.nn.softplus(dt_ref[s, 0] + dtb_ref[...])
        dtt = jax.nn.softplus(dtt_ref[s, 0] + dtbt_ref[...])
        acum = _dot01(lower, dt * a, split_rhs=True)
        acumt = _dot01(dtt * at, upper, split_rhs=False)
        e_full = _dot01(jnp.exp(acum), expand, split_rhs=False)
        tail_full = _dot01(jnp.exp(acum[l - 1:l, :] - acum) * dt, expand, split_rhs=False)
        return dict(xbc=xbc, xs=xs, dtt=dtt, acum=acum, acumt=acumt, e_full=e_full,
                    xt=(xs * tail_full).astype(BF16), xs16=xs.astype(BF16))

    def group(s, v, g):
        xbc = v["xbc"]
        bm = xbc[:, B_WIDTH + g * B_STATE:B_WIDTH + (g + 1) * B_STATE].astype(BF16)
        cm = xbc[:, B_WIDTH + (B_GROUPS + g) * B_STATE:B_WIDTH + (B_GROUPS + g + 1) * B_STATE].astype(BF16)
        cb = _dot_nt(cm, bm)
        gs = slice(g * gw, (g + 1) * gw)
        hg = h_scr[s, g]
        y_state = jnp.dot(cm, hg.astype(BF16), preferred_element_type=F32) * v["e_full"][:, gs]
        for pr in range(B_HPG // 2):
            ws = []
            for r in (g * B_HPG + 2 * pr, g * B_HPG + 2 * pr + 1):
                seg = v["acum"][:, r:r + 1] - v["acumt"][r:r + 1, :]
                decay = jnp.exp(jnp.where(causal, seg, -jnp.inf))
                ws.append((cb * decay * v["dtt"][r:r + 1, :]).astype(BF16))
            c0 = g * gw + pr * 2 * B_HEAD_DIM
            xp = v["xs16"][:, c0:c0 + 2 * B_HEAD_DIM]
            y0 = jnp.dot(ws[0], xp, preferred_element_type=F32)
            y1 = jnp.dot(ws[1], xp, preferred_element_type=F32)
            yi_scr[s, :, c0:c0 + 2 * B_HEAD_DIM] = jnp.where(lane < B_HEAD_DIM, y0, y1)
        yi_scr[s, :, gs] = yi_scr[s, :, gs] + y_state
        upd = lax.dot_general(bm, v["xt"][:, gs], (((0,), (0,)), ((), ())), preferred_element_type=F32)
        h_scr[s, g] = hg * v["e_full"][l - 1:l, gs] + upd

    def finish(s, v):
        y = (yi_scr[s] + dsk_ref[...] * v["xs"]) * zg_ref[s]
        for g in range(B_GROUPS):
            gs = slice(g * gw, (g + 1) * gw)
            yg = y[:, gs]
            yg = yg * lax.rsqrt(jnp.mean(yg * yg, axis=-1, keepdims=True) + EPS)
            y_ref[s, :, gs] = (yg * nw_ref[:, gs]).astype(y_ref.dtype)

    vals = [prepare(s) for s in range(nb)]
    for g in range(B_GROUPS):
        for s in range(nb):
            group(s, vals[s], g)
    for s in range(nb):
        finish(s, vals[s])

    @pl.when(c == pl.num_programs(1) - 1)
    def _():
        hout_ref[...] = h_scr[...]


def _ssd(name, zg, xbc, dt_raw, dt_bias, a_log, d_skip, norm_w, h0):
    b, t, _ = zg.shape
    nc = t // CHUNK
    gw = B_WIDTH // B_GROUPS
    dt4 = dt_raw.reshape(b, nc, CHUNK, B_HEADS)
    dtt4 = jnp.swapaxes(dt4, 2, 3)
    h0t = jnp.transpose(h0.reshape(b, B_GROUPS, B_HPG, B_HEAD_DIM, B_STATE), (0, 1, 4, 2, 3)).reshape(b, B_GROUPS, B_STATE, gw)
    expand = jnp.repeat(jnp.eye(B_HEADS, dtype=BF16), B_HEAD_DIM, axis=1)
    dsk = jnp.repeat(d_skip, B_HEAD_DIM).reshape(1, B_WIDTH)
    row = lambda v: v.reshape(1, -1)
    colv = lambda v: v.reshape(-1, 1)
    const2 = lambda bi, ci: (0, 0)
    nb = 2 if b % 2 == 0 else 1
    y, hout = pl.pallas_call(
        functools.partial(_ssd_multi_body, nb=nb),
        grid=(b // nb, nc),
        in_specs=[
            pl.BlockSpec((nb, CHUNK, B_WIDTH), lambda bi, ci: (bi, ci, 0)),
            pl.BlockSpec((nb, CHUNK, B_CONV_DIM), lambda bi, ci: (bi, ci, 0)),
            pl.BlockSpec((nb, 1, CHUNK, B_HEADS), lambda bi, ci: (bi, ci, 0, 0)),
            pl.BlockSpec((nb, 1, B_HEADS, CHUNK), lambda bi, ci: (bi, ci, 0, 0)),
            pl.BlockSpec((1, B_HEADS), const2),
            pl.BlockSpec((B_HEADS, 1), const2),
            pl.BlockSpec((1, B_HEADS), const2),
            pl.BlockSpec((B_HEADS, 1), const2),
            pl.BlockSpec((1, B_WIDTH), const2),
            pl.BlockSpec((1, B_WIDTH), const2),
            pl.BlockSpec((B_HEADS, B_WIDTH), const2),
            pl.BlockSpec((nb, B_GROUPS, B_STATE, gw), lambda bi, ci: (bi, 0, 0, 0)),
        ],
        out_specs=[pl.BlockSpec((nb, CHUNK, B_WIDTH), lambda bi, ci: (bi, ci, 0)),
                   pl.BlockSpec((nb, B_GROUPS, B_STATE, gw), lambda bi, ci: (bi, 0, 0, 0))],
        out_shape=[jax.ShapeDtypeStruct((b, t, B_WIDTH), BF16),
                   jax.ShapeDtypeStruct((b, B_GROUPS, B_STATE, gw), F32)],
        scratch_shapes=[pltpu.VMEM((nb, B_GROUPS, B_STATE, gw), F32),
                        pltpu.VMEM((nb, CHUNK, B_WIDTH), F32)],
        compiler_params=_params(("parallel", "arbitrary")),
        name=name,
    )(zg, xbc, dt4, dtt4, row(dt_bias), colv(dt_bias), row(a_log), colv(a_log),
      dsk, row(norm_w), expand, h0t)
    hnew = jnp.transpose(hout.reshape(b, B_GROUPS, B_STATE, B_HPG, B_HEAD_DIM), (0, 1, 3, 4, 2))
    return y, hnew.reshape(b, B_HEADS, B_HEAD_DIM, B_STATE)


def _mla_body(q_ref, kn_ref, kr_ref, vt_ref, g_ref, o_ref, m_scr, acc_scr, qka_scr, qkb_scr, *,
              tb, qt0, heads, pair_blocks):
    qt = pl.program_id(2) + qt0
    _softmax_init(m_scr, acc_scr, heads, tb)
    krow = lax.broadcasted_iota(I32, (tb, tb), 0)
    qcol = lax.broadcasted_iota(I32, (tb, tb), 1)
    diag_vis = (krow // CHUNK) <= (qcol // CHUNK)

    def tiles(j, nt, masked):
        rows = pl.ds(pl.multiple_of(j * tb, tb), nt * tb)
        kr = kr_ref[0, rows, :]
        qk = [_dot_nt(jnp.concatenate([kn_ref[0, rows, h * QK_NOPE:(h + 1) * QK_NOPE], kr], axis=1),
                      q_ref[0, :, h * 2 * LANES:(h + 1) * 2 * LANES]) for h in range(heads)]
        for h in range(heads):
            s = jnp.where(diag_vis, qk[h], -jnp.inf) if masked else qk[h]
            _softmax_step(s, [vt_ref[0, h, j + i] for i in range(nt)], m_scr, acc_scr, h)

    def pair_logits(p, buf):
        rows = pl.ds(pl.multiple_of(p * 2 * tb, 2 * tb), 2 * tb)
        kr = kr_ref[0, rows, :]
        for h in range(heads):
            kc = jnp.concatenate([kn_ref[0, rows, h * QK_NOPE:(h + 1) * QK_NOPE], kr], axis=1)
            buf[h] = _dot_nt(kc, q_ref[0, :, h * 2 * LANES:(h + 1) * 2 * LANES])

    def pair_softmax(p, buf):
        for h in range(heads):
            _softmax_step(buf[h], [vt_ref[0, h, 2 * p], vt_ref[0, h, 2 * p + 1]], m_scr, acc_scr, h)

    def last_pair_softmax(p, buf):
        kchunk = (p * 2 * tb + lax.broadcasted_iota(I32, (2 * tb, tb), 0)) // CHUNK
        vis = kchunk <= (qt * tb + lax.broadcasted_iota(I32, (2 * tb, tb), 1)) // CHUNK
        for h in range(heads):
            _softmax_step(jnp.where(vis, buf[h], -jnp.inf), [vt_ref[0, h, 2 * p], vt_ref[0, h, 2 * p + 1]],
                          m_scr, acc_scr, h)

    def full_tile(j, carry):
        tiles(j, 1, False)
        return carry

    if pair_blocks:
        _pipelined_blocks_final(qt // 2, pair_logits, pair_softmax, last_pair_softmax, qka_scr, qkb_scr)
    else:
        def full_pair(j2, carry):
            tiles(2 * j2, 2, False)
            return carry

        lax.fori_loop(0, qt // 2, full_pair, 0)
        lax.fori_loop(2 * (qt // 2), qt, full_tile, 0)
        tiles(qt, 1, True)
    _softmax_finish(acc_scr, g_ref, o_ref, heads)


def _mla(name, q, kn, kr, vt, gate, tb, past_len, heads=8):
    b, t, _ = q.shape
    l = kn.shape[1]
    assert t % tb == 0 and past_len % tb == 0 and l == past_len + t and C_HEADS % heads == 0
    nkv = l // tb
    body = functools.partial(_mla_body, tb=tb, qt0=past_len // tb, heads=heads, pair_blocks=nkv % 2 == 0)
    return pl.pallas_call(
        body,
        grid=(b, C_HEADS // heads, t // tb),
        in_specs=[
            pl.BlockSpec((1, tb, heads * 2 * LANES), lambda bi, h, qi: (bi, qi, h)),
            pl.BlockSpec((1, l, heads * QK_NOPE), lambda bi, h, qi: (bi, 0, h)),
            pl.BlockSpec((1, l, LANES), lambda bi, h, qi: (bi, 0, 0)),
            pl.BlockSpec((1, heads, nkv, vt.shape[3], tb), lambda bi, h, qi: (bi, h, 0, 0, 0)),
            pl.BlockSpec((1, tb, heads * V_DIM), lambda bi, h, qi: (bi, qi, h)),
        ],
        out_specs=pl.BlockSpec((1, tb, heads * V_DIM), lambda bi, h, qi: (bi, qi, h)),
        out_shape=jax.ShapeDtypeStruct((b, t, C_WIDTH), BF16),
        scratch_shapes=[pltpu.VMEM((heads, 1, tb), F32), pltpu.VMEM((heads, VT_ROWS, tb), F32),
                        pltpu.VMEM((heads, 2 * tb, tb), F32), pltpu.VMEM((heads, 2 * tb, tb), F32)],
        compiler_params=_params(("parallel", "parallel", "parallel")),
        name=name,
    )(q, kn, kr, vt, gate)


def _even_weights(w_in):
    offs = [0]
    for s in (A_WIDTH, A_WIDTH, A_WIDTH, A_WIDTH, IDX_HEADS * IDX_DIM, IDX_DIM, IDX_HEADS, B_WIDTH, B_CONV_DIM, B_HEADS):
        offs.append(offs[-1] + s)
    cols = [w_in[:, offs[i]:offs[i + 1]] for i in range(10)]
    aq, ak, av, ag, iq, ik, iw, bz, bxbc, bdt = cols
    aq = aq * (A_HEAD_DIM ** -0.5 * LOG2E)
    pad = jnp.zeros((w_in.shape[0], LANES - IDX_DIM - IDX_HEADS - B_HEADS), w_in.dtype)
    small = jnp.concatenate([ik, iw, bdt, pad], axis=1)
    return [c.astype(BF16) for c in (aq, ak, av, ag, iq, small, bz, bxbc)]


def _even_layer(tag, x, past, wts, w_out, conv_w, conv_b, dt_bias, a_log, d_skip, norm_w, ln_g, ln_b, t5_bias, tb):
    b, t, _ = x.shape
    m = b * t
    x2 = x.reshape(m, D_MODEL)
    w_aq, w_ak, w_av, w_ag, w_iq, w_small, w_bz, w_bxbc = wts
    aq, xb = _mm(tag + "_in_aq", x2, w_aq, [BF16, ("acopy",)])
    ak, ak16 = _mm(tag + "_in_ak", xb, w_ak, [F32, BF16])
    (ag,) = _mm(tag + "_in_ag", xb, w_ag, [("silu", F32)])
    (iqt,) = _mm(tag + "_in_iq", xb, w_iq, [("heads", t, IDX_DIM, BF16)])
    (small,) = _mm(tag + "_in_small", xb, w_small, [F32])
    (bzg,) = _mm(tag + "_in_bz", xb, w_bz, [("silu", F32)])
    ik = small[:, :IDX_DIM].reshape(b, t, IDX_DIM)
    iwt = jnp.swapaxes(small[:, IDX_DIM:IDX_DIM + IDX_HEADS].reshape(b, t, IDX_HEADS), 1, 2)
    bdt = small[:, IDX_DIM + IDX_HEADS:IDX_DIM + IDX_HEADS + B_HEADS].reshape(b, t, B_HEADS)
    k16 = ak16.reshape(b, t, A_WIDTH)
    ik16 = ik.astype(BF16)
    if past is None:
        p_len = 0
        av, vt = _mm(tag + "_in_av", xb, w_av, [F32, ("vt", t, tb, BF16)])
        conv0 = jnp.zeros((b, SUBLANES, B_CONV_DIM), F32)
        h0 = jnp.zeros((b, B_HEADS, B_HEAD_DIM, B_STATE), F32)
    else:
        pk, pv, pki, pconv, pssm = past
        p_len = pk.shape[1]
        av, av16 = _mm(tag + "_in_av", xb, w_av, [F32, BF16])
        k16 = jnp.concatenate([pk.reshape(b, p_len, A_WIDTH).astype(BF16), k16], axis=1)
        vt = _to_vt(jnp.concatenate([pv.reshape(b, p_len, A_WIDTH).astype(BF16), av16.reshape(b, t, A_WIDTH)], axis=1), tb)
        ik16 = jnp.concatenate([pki.astype(BF16), ik16], axis=1)
        conv0 = jnp.pad(pconv, ((0, 0), (SUBLANES - (CONV_W - 1), 0), (0, 0)))
        h0 = pssm
    xbc, tails = _mm_conv(tag + "_in_bxbc", xb, w_bxbc, conv0, conv_w, conv_b, t)
    conv_new = tails.reshape(b, -1, SUBLANES, B_CONV_DIM)[:, -1, SUBLANES - (CONV_W - 1):]
    a_out = _dsa(tag + "_dsa", iqt, iwt, ik16, aq.reshape(b, t, A_WIDTH), k16, vt,
                 ag.reshape(b, t, A_WIDTH), _bias_tables(t5_bias, tb), tb, p_len)
    b_out, ssm_new = _ssd(tag + "_ssd", bzg.reshape(b, t, B_WIDTH), xbc.reshape(b, t, B_CONV_DIM), bdt, dt_bias, a_log,
                          d_skip, norm_w, h0)
    wo = w_out.astype(BF16)
    y, y16 = _oproj_ln(tag + "_out0", [a_out.reshape(m, A_WIDTH), b_out.reshape(m, B_WIDTH)],
                       [wo[:A_WIDTH], wo[A_WIDTH:]], x2, ln_g, ln_b)
    state = (ak.reshape(b, t, A_HEADS, A_HEAD_DIM), av.reshape(b, t, A_HEADS, A_HEAD_DIM), ik, conv_new, ssm_new)
    return y.reshape(b, t, D_MODEL), y16, state


def _rope_rot_cols(w):
    half = QK_ROPE // 2
    return jnp.concatenate([-w[..., half:], w[..., :half]], axis=-1)


def _odd_weights(w_in, w_uq, w_ukv):
    w_cq = w_in[:, :Q_LORA]
    w_ckv = w_in[:, Q_LORA:Q_LORA + KV_LORA]
    w_kr = w_in[:, Q_LORA + KV_LORA:Q_LORA + KV_LORA + QK_ROPE]
    w_gate = w_in[:, Q_LORA + KV_LORA + QK_ROPE:]
    w_kr2 = jnp.concatenate([w_kr, _rope_rot_cols(w_kr)], axis=1)
    uq = w_uq.reshape(Q_LORA, C_HEADS, QK_NOPE + QK_ROPE) * (MLA_SCALE * LOG2E)
    uq_rope = uq[..., QK_NOPE:]
    uq2 = jnp.concatenate([uq[..., :QK_NOPE], uq_rope, _rope_rot_cols(uq_rope)], axis=-1).reshape(Q_LORA, C_HEADS * 2 * LANES)
    ukv = w_ukv.reshape(KV_LORA, C_HEADS, QK_NOPE + V_DIM)
    w_uk = ukv[..., :QK_NOPE].reshape(KV_LORA, C_HEADS * QK_NOPE)
    w_uv = ukv[..., QK_NOPE:].reshape(KV_LORA, C_HEADS * V_DIM)
    return [c.astype(BF16) for c in (w_cq, w_ckv, w_kr2, w_gate, uq2, w_uk, w_uv)]


def _rope_table(pos):
    half = QK_ROPE // 2
    inv = ROPE_THETA ** (-jnp.arange(half, dtype=F32) / half)
    ang = pos.astype(F32)[:, None] * inv[None, :]
    cos, sin = jnp.cos(ang), jnp.sin(ang)
    return jnp.concatenate([cos, cos, sin, sin], axis=1)


def _odd_layer(tag, x, x16, past, wts, q_norm_w, kv_norm_w, w_out, ln_g, ln_b, tb):
    b, t, _ = x.shape
    m = b * t
    w_cq, w_ckv, w_kr2, w_gate, w_uq2, w_uk, w_uv = wts
    p_len = 0 if past is None else past[0].shape[1]
    tab = _rope_table(p_len + jnp.arange(t, dtype=I32))
    (cq16,) = _mm_rms(tag + "_in_cq", x16, w_cq, q_norm_w, [BF16])
    ckv, ckv16 = _mm_rms(tag + "_in_ckv", x16, w_ckv, kv_norm_w, [F32, BF16])
    kr, kr16 = _mm_rope_k(tag + "_in_kr", x16, w_kr2, tab)
    (gate,) = _mm(tag + "_in_gate", x16, w_gate, [("silu", F32)])
    q = _mm_rope_q(tag + "_uq", cq16, w_uq2, tab)
    lat16 = ckv16.reshape(b, t, KV_LORA)
    kr16 = kr16.reshape(b, t, LANES)
    if past is not None:
        lat16 = jnp.concatenate([past[0].astype(BF16), lat16], axis=1)
        kr_past = jnp.pad(past[1], ((0, 0), (0, 0), (0, LANES - QK_ROPE))).astype(BF16)
        kr16 = jnp.concatenate([kr_past, kr16], axis=1)
    l = p_len + t
    lat2 = lat16.reshape(b * l, KV_LORA)
    wide = C_HEADS * QK_NOPE
    (kn,) = _mm(tag + "_uk", lat2, w_uk, [BF16], tn=wide)
    if tb % LANES == 0:
        (vt,) = _mm(tag + "_uv", lat2, w_uv, [("vt", l, tb, BF16)], tn=wide)
    else:
        (v,) = _mm(tag + "_uv", lat2, w_uv, [BF16], tn=wide)
        vt = _to_vt(v.reshape(b, l, C_WIDTH), tb)
    o = _mla(tag + "_mla", q.reshape(b, t, C_HEADS * 2 * LANES), kn.reshape(b, l, C_HEADS * QK_NOPE), kr16,
             vt, gate.reshape(b, t, C_WIDTH), tb, p_len)
    y, _ = _oproj_ln(tag + "_out1", [o.reshape(m, C_WIDTH)], [w_out.astype(BF16)], x.reshape(m, D_MODEL), ln_g, ln_b)
    return y.reshape(b, t, D_MODEL), (ckv.reshape(b, t, KV_LORA), kr.reshape(b, t, QK_ROPE))


def kernel(x_prompt, x_sample, cache_a_k, cache_a_v, cache_a_kidx, state_b_conv, state_b_ssm, cache_c_latent, cache_c_krope, t5_bias, w_in0, w_out0, conv_w, conv_b, dt_bias, a_log, d_skip, ssm_norm_w, ln0_g, ln0_b, w_in1, q_norm_w, kv_norm_w, w_uq, w_ukv, w_out1, ln1_g, ln1_b):
    tb_prompt = 256
    tb_sample = CHUNK
    ew = _even_weights(w_in0[0])
    eprm = (w_out0[0], conv_w[0], conv_b[0], dt_bias[0], a_log[0], d_skip[0], ssm_norm_w[0], ln0_g[0], ln0_b[0], t5_bias)
    yp, yp16, st_p = _even_layer("p0", x_prompt, None, ew, *eprm, tb_prompt)
    past = (cache_a_k[0], cache_a_v[0], cache_a_kidx[0], state_b_conv[0], state_b_ssm[0])
    ys, ys16, st_s = _even_layer("s0", x_sample, past, ew, *eprm, tb_sample)
    ow = _odd_weights(w_in1[0], w_uq[0], w_ukv[0])
    oprm = (q_norm_w[0], kv_norm_w[0], w_out1[0], ln1_g[0], ln1_b[0])
    yp, od_p = _odd_layer("p1", yp, yp16, None, ow, *oprm, tb_prompt)
    ys, od_s = _odd_layer("s1", ys, ys16, (cache_c_latent[0], cache_c_krope[0]), ow, *oprm, tb_sample)
    e = lambda a: a[None]
    return (yp, ys, e(st_p[0]), e(st_s[0]), e(st_p[1]), e(st_s[1]), e(st_p[2]), e(st_s[2]),
            e(st_p[3]), e(st_s[3]), e(st_p[4]), e(st_s[4]), e(od_p[0]), e(od_s[0]), e(od_p[1]), e(od_s[1]))
```

```python
import functools
import math

import jax
import jax.numpy as jnp
from jax import lax
from jax.experimental import pallas as pl
from jax.experimental.pallas import tpu as pltpu

F32 = jnp.float32
BF16 = jnp.bfloat16
I32 = jnp.int32
I16 = jnp.int16

D_MODEL = 2048
CHUNK = 64
A_HEADS = 8
A_HEAD_DIM = 128
A_WIDTH = A_HEADS * A_HEAD_DIM
IDX_HEADS = 16
IDX_DIM = 64
IDX_W_SCALE = (IDX_HEADS * IDX_DIM) ** -0.5
TOPK_MAX = 256
N_BUCKETS = 32
MAX_DISTANCE = 128
B_HEAD_DIM = 64
B_WIDTH = D_MODEL
B_HEADS = B_WIDTH // B_HEAD_DIM
B_GROUPS = 4
B_HPG = B_HEADS // B_GROUPS
B_STATE = 128
CONV_W = 4
B_CONV_DIM = B_WIDTH + 2 * B_GROUPS * B_STATE
C_HEADS = 16
Q_LORA = 512
KV_LORA = 512
QK_NOPE = 128
QK_ROPE = 64
V_DIM = 128
C_WIDTH = C_HEADS * V_DIM
ROPE_THETA = 10000.0
MLA_SCALE = (QK_NOPE + QK_ROPE) ** -0.5
DEPTH = 2
ALPHA = (2 * DEPTH) ** 0.25
EPS = 1e-5

LANES = 128
SUBLANES = 8
VT_ROWS = LANES + 2 * SUBLANES
VMEM_LIMIT = 56 * 1024 * 1024
LOG2E = math.log2(math.e)
INT_MIN = -(2 ** 31)
INT_MAX = 2 ** 31 - 1
KEY_NEG_INF = (0xFF800000 ^ 0x7FFFFFFF) - (1 << 32)
HALF_BITS = 16
HALF_MASK = (1 << HALF_BITS) - 1
HALF_OFFSET = 1 << (HALF_BITS - 1)
HALF_MIN = -HALF_OFFSET
NEG_BIG = -1e30


def _params(sem):
    return pltpu.CompilerParams(dimension_semantics=sem, vmem_limit_bytes=VMEM_LIMIT)


def _dot_nt(a, b):
    return lax.dot_general(a, b, (((1,), (1,)), ((), ())), preferred_element_type=F32)


def _silu(x):
    return x * (1.0 / (1.0 + jnp.exp(-x)))


def _split3(v):
    hi = v.astype(BF16)
    r = v - hi.astype(F32)
    mid = r.astype(BF16)
    return hi, mid, (r - mid.astype(F32)).astype(BF16)


def _dot01(a, b, split_rhs):
    if split_rhs:
        parts = [jnp.dot(a, t, preferred_element_type=F32) for t in _split3(b)]
    else:
        parts = [jnp.dot(t, b, preferred_element_type=F32) for t in _split3(a)]
    return parts[0] + parts[1] + parts[2]


def _mm_body(a_ref, w_ref, *refs, kinds, sub):
    o_refs, bufs = refs[:len(kinds)], refs[len(kinds):]
    tm, tn = a_ref.shape[0], w_ref.shape[1]

    def lhs(rows):
        a = a_ref[rows, :]
        return a if a.dtype == w_ref.dtype else a.astype(w_ref.dtype)

    def write(acc, r0, nr):
        rows = slice(r0, r0 + nr)
        for kind, o in zip(kinds, o_refs):
            if kind[0] == "plain":
                o[rows, :] = acc.astype(o.dtype)
            elif kind[0] == "silu":
                o[rows, :] = _silu(acc).astype(o.dtype)
            elif kind[0] == "acopy":
                @pl.when(pl.program_id(1) == 0)
                def _():
                    o[rows, :] = lhs(rows)
            elif kind[0] == "vt":
                tb = kind[2]
                for hh in range(tn // LANES):
                    for kt in range(nr // tb):
                        tile = acc[kt * tb:(kt + 1) * tb, hh * LANES:(hh + 1) * LANES]
                        o[0, hh, r0 // tb + kt, 0:LANES, :] = tile.T.astype(o.dtype)
                        o[0, hh, r0 // tb + kt, LANES:VT_ROWS, :] = _ones_rows(tb)
            elif kind[0] == "heads":
                hd = kind[2]
                for hh in range(tn // hd):
                    o[0, hh, rows, :] = acc[:, hh * hd:(hh + 1) * hd].astype(o.dtype)

    if sub == tm:
        write(jnp.dot(lhs(slice(0, tm)), w_ref[...], preferred_element_type=F32), 0, tm)
        return
    n_sub = tm // sub

    def matmul(r):
        bufs[r % 2][...] = jnp.dot(lhs(slice(r * sub, (r + 1) * sub)), w_ref[...], preferred_element_type=F32)

    matmul(0)
    for r in range(n_sub):
        if r + 1 < n_sub:
            matmul(r + 1)
        write(bufs[r % 2][...], r * sub, sub)


def _mm_rms_body(a_ref, w_ref, g_ref, *o_refs):
    acc = jnp.dot(a_ref[...], w_ref[...], preferred_element_type=F32)
    y = acc * lax.rsqrt(jnp.mean(acc * acc, axis=-1, keepdims=True) + EPS) * g_ref[...]
    for o in o_refs:
        o[...] = y.astype(o.dtype)


def _rope_half(t):
    return t + pltpu.roll(t, QK_ROPE, 1)


def _mm_rope_k_body(a_ref, w_ref, tab_ref, o32_ref, o16_ref):
    acc = jnp.dot(a_ref[...], w_ref[...], preferred_element_type=F32)
    r = _rope_half(acc * tab_ref[...])
    lane = lax.broadcasted_iota(I32, r.shape, 1)
    o32_ref[...] = r[:, :QK_ROPE]
    o16_ref[...] = jnp.where(lane < QK_ROPE, r, 0.0).astype(o16_ref.dtype)


def _mm_rope_q_body(a_ref, w_ref, tab_ref, o_ref, *, heads):
    acc = jnp.dot(a_ref[...], w_ref[...], preferred_element_type=F32)
    tab = tab_ref[...]
    lane = lax.broadcasted_iota(I32, tab.shape, 1)
    for h in range(heads):
        base = h * 2 * LANES
        o_ref[:, base:base + LANES] = acc[:, base:base + LANES].astype(o_ref.dtype)
        r = _rope_half(acc[:, base + LANES:base + 2 * LANES] * tab)
        o_ref[:, base + LANES:base + 2 * LANES] = jnp.where(lane < QK_ROPE, r, 0.0).astype(o_ref.dtype)


def _mm_call(name, body, a, w, extra, extra_specs, out_cols, out_dtypes, tm, tn):
    m, k = a.shape
    n = w.shape[1]
    tm = math.gcd(tm, m)
    tn = min(tn, n)
    assert tm % SUBLANES == 0 and n % tn == 0, (m, n, tm, tn)
    oc = [tn if c is None else c for c in out_cols]
    return pl.pallas_call(
        body,
        grid=(m // tm, n // tn),
        in_specs=[pl.BlockSpec((tm, k), lambda i, j: (i, 0)),
                  pl.BlockSpec((k, tn), lambda i, j: (0, j))] + extra_specs(tm, tn),
        out_specs=[pl.BlockSpec((tm, c), lambda i, j: (i, j)) for c in oc],
        out_shape=[jax.ShapeDtypeStruct((m, (n // tn) * c), d) for c, d in zip(oc, out_dtypes)],
        compiler_params=_params(("parallel", "parallel")),
        name=name,
    )(a, w, *extra)


def _mm(name, a, w, outs, tm=1024, tn=1024):
    m, k = a.shape
    n = w.shape[1]
    tm = math.gcd(tm, m, *[o[1] for o in outs if isinstance(o, tuple) and o[0] in ("vt", "heads")])
    tn = min(tn, n)
    assert tm % SUBLANES == 0 and n % tn == 0, (m, n, tm, tn)
    kinds, specs, shapes = [], [], []
    for o in outs:
        o = o if isinstance(o, tuple) else ("plain", o)
        if o[0] in ("plain", "silu"):
            kinds.append((o[0],))
            specs.append(pl.BlockSpec((tm, tn), lambda i, j: (i, j)))
            shapes.append(jax.ShapeDtypeStruct((m, n), o[1]))
        elif o[0] == "acopy":
            kinds.append(("acopy",))
            specs.append(pl.BlockSpec((tm, k), lambda i, j: (i, 0)))
            shapes.append(jax.ShapeDtypeStruct((m, k), w.dtype))
        elif o[0] == "vt":
            _, t, tb, dt = o
            assert t % tm == 0 and tm % tb == 0 and tn % LANES == 0
            per = t // tm
            kinds.append(("vt", t, tb))
            specs.append(pl.BlockSpec((1, tn // LANES, tm // tb, VT_ROWS, tb),
                                      lambda i, j: (i // per, j, i % per, 0, 0)))
            shapes.append(jax.ShapeDtypeStruct((m // t, n // LANES, t // tb, VT_ROWS, tb), dt))
        elif o[0] == "heads":
            _, t, hd, dt = o
            assert t % tm == 0 and tn % hd == 0
            per = t // tm
            kinds.append(("heads", t, hd))
            specs.append(pl.BlockSpec((1, tn // hd, tm, hd), lambda i, j: (i // per, j, i % per, 0)))
            shapes.append(jax.ShapeDtypeStruct((m // t, n // hd, t, hd), dt))
        else:
            raise ValueError(o)
    busy = any(kd[0] in ("silu", "vt", "heads") for kd in kinds)
    sub = math.gcd(tm, 2 * LANES) if busy else tm
    for kd in kinds:
        if kd[0] == "vt":
            sub = max(sub, kd[2])
    sub = sub if tm % sub == 0 else tm
    return pl.pallas_call(
        functools.partial(_mm_body, kinds=tuple(kinds), sub=sub),
        grid=(m // tm, n // tn),
        in_specs=[pl.BlockSpec((tm, k), lambda i, j: (i, 0)), pl.BlockSpec((k, tn), lambda i, j: (0, j))],
        out_specs=specs,
        out_shape=shapes,
        scratch_shapes=[pltpu.VMEM((sub, tn), F32)] * (2 if sub < tm else 0),
        compiler_params=_params(("parallel", "arbitrary")),
        name=name,
    )(a, w)


def _mm_conv_body(a_ref, w_ref, st_ref, cw_ref, cb_ref, o_ref, tail_ref, raw_a, raw_b, prev_scr, *,
                  tiles_per_seq, sub):
    i, j = pl.program_id(0), pl.program_id(1)
    tm, tn = o_ref.shape
    bufs = (raw_a, raw_b)

    @pl.when(i % tiles_per_seq == 0)
    def _():
        raw_a[0:SUBLANES, :] = st_ref[0]

    @pl.when(i % tiles_per_seq != 0)
    def _():
        raw_a[0:SUBLANES, :] = prev_scr[j]

    def matmul(r):
        bufs[r % 2][SUBLANES:SUBLANES + sub, :] = jnp.dot(a_ref[r * sub:(r + 1) * sub, :], w_ref[...],
                                                          preferred_element_type=F32)

    def epilogue(r):
        buf, nxt = bufs[r % 2], bufs[(r + 1) % 2]
        conv = cb_ref[...]
        for tap in range(CONV_W):
            lo = SUBLANES - (CONV_W - 1) + tap
            conv = conv + buf[lo:lo + sub, :] * cw_ref[tap:tap + 1, :]
        o_ref[r * sub:(r + 1) * sub, :] = _silu(conv).astype(o_ref.dtype)
        nxt[0:SUBLANES, :] = buf[sub:sub + SUBLANES, :]

    n_sub = tm // sub
    matmul(0)
    for r in range(n_sub):
        if r + 1 < n_sub:
            matmul(r + 1)
        epilogue(r)
    tail = bufs[n_sub % 2][0:SUBLANES, :]
    prev_scr[j] = tail
    tail_ref[0] = tail


def _mm_conv(name, a, w, state, conv_w, conv_b, t, tm=1024, tn=1024):
    m, k = a.shape
    n = w.shape[1]
    tm = math.gcd(tm, t)
    tn = min(tn, n)
    assert n % tn == 0 and tm % SUBLANES == 0
    per = t // tm
    sub = math.gcd(tm, 2 * LANES)
    return pl.pallas_call(
        functools.partial(_mm_conv_body, tiles_per_seq=per, sub=sub),
        grid=(m // tm, n // tn),
        in_specs=[pl.BlockSpec((tm, k), lambda i, j: (i, 0)),
                  pl.BlockSpec((k, tn), lambda i, j: (0, j)),
                  pl.BlockSpec((1, SUBLANES, tn), lambda i, j: (i // per, 0, j)),
                  pl.BlockSpec((CONV_W, tn), lambda i, j: (0, j)),
                  pl.BlockSpec((1, tn), lambda i, j: (0, j))],
        out_specs=[pl.BlockSpec((tm, tn), lambda i, j: (i, j)),
                   pl.BlockSpec((1, SUBLANES, tn), lambda i, j: (i, 0, j))],
        out_shape=[jax.ShapeDtypeStruct((m, n), F32), jax.ShapeDtypeStruct((m // tm, SUBLANES, n), F32)],
        scratch_shapes=[pltpu.VMEM((sub + SUBLANES, tn), F32), pltpu.VMEM((sub + SUBLANES, tn), F32),
                        pltpu.VMEM((n // tn, SUBLANES, tn), F32)],
        compiler_params=_params(("arbitrary", "arbitrary")),
        name=name,
    )(a, w, state, conv_w, conv_b.reshape(1, n))


def _mm_rms(name, a, w, g, out_dtypes, tm=1024):
    n = w.shape[1]
    return _mm_call(name, _mm_rms_body, a, w, [g.reshape(1, n)],
                    lambda tm_, tn_: [pl.BlockSpec((1, n), lambda i, j: (0, 0))],
                    [None] * len(out_dtypes), out_dtypes, tm, n)


def _tab_spec(t_rows):
    def spec(tm, tn):
        nt = t_rows // tm
        return [pl.BlockSpec((tm, LANES), lambda i, j: (i % nt, 0))]
    return spec


def _mm_rope_k(name, a, w, tab, tm=1024):
    tm = min(tm, tab.shape[0])
    return _mm_call(name, _mm_rope_k_body, a, w, [tab], _tab_spec(tab.shape[0]), [QK_ROPE, LANES], [F32, BF16],
                    tm, LANES)


def _mm_rope_q(name, a, w, tab, tm=1024, heads_per_block=8):
    tm = min(tm, tab.shape[0])
    body = functools.partial(_mm_rope_q_body, heads=heads_per_block)
    return _mm_call(name, body, a, w, [tab], _tab_spec(tab.shape[0]), [None], [BF16], tm,
                    heads_per_block * 2 * LANES)[0]


def _oproj_body(*refs, n_parts, sub):
    parts = refs[:n_parts]
    ws = refs[n_parts:2 * n_parts]
    x_ref, g_ref, b_ref, o32_ref, o16_ref, buf_a, buf_b = refs[2 * n_parts:]
    bufs = (buf_a, buf_b)
    n_sub = x_ref.shape[0] // sub

    def matmul(r):
        rows = slice(r * sub, (r + 1) * sub)
        acc = ALPHA * x_ref[rows, :]
        for p, w in zip(parts, ws):
            acc = acc + jnp.dot(p[rows, :], w[...], preferred_element_type=F32)
        bufs[r % 2][...] = acc

    def layer_norm(r):
        rows = slice(r * sub, (r + 1) * sub)
        acc = bufs[r % 2][...]
        mu = jnp.mean(acc, axis=-1, keepdims=True)
        d = acc - mu
        var = jnp.mean(d * d, axis=-1, keepdims=True)
        y = d * lax.rsqrt(var + EPS) * g_ref[...] + b_ref[...]
        o32_ref[rows, :] = y
        o16_ref[rows, :] = y.astype(o16_ref.dtype)

    matmul(0)
    for r in range(n_sub):
        if r + 1 < n_sub:
            matmul(r + 1)
        layer_norm(r)


def _oproj_ln(name, parts, ws, x, g, b, tm=512):
    m, n = x.shape
    tm = min(tm, m)
    sub = math.gcd(tm, 2 * LANES)
    np_ = len(parts)
    const = lambda i: (0, 0)
    return pl.pallas_call(
        functools.partial(_oproj_body, n_parts=np_, sub=sub),
        scratch_shapes=[pltpu.VMEM((sub, n), F32)] * 2,
        grid=(m // tm,),
        in_specs=[pl.BlockSpec((tm, p.shape[1]), lambda i: (i, 0)) for p in parts]
        + [pl.BlockSpec(w.shape, const, pipeline_mode=pl.Buffered(1)) for w in ws]
        + [pl.BlockSpec((tm, n), lambda i: (i, 0)),
           pl.BlockSpec((1, n), const), pl.BlockSpec((1, n), const)],
        out_specs=[pl.BlockSpec((tm, n), lambda i: (i, 0))] * 2,
        out_shape=[jax.ShapeDtypeStruct((m, n), F32), jax.ShapeDtypeStruct((m, n), BF16)],
        compiler_params=_params(("parallel",)),
        name=name,
    )(*parts, *ws, x, g.reshape(1, n), b.reshape(1, n))


def _ones_rows(tb):
    return jnp.ones((VT_ROWS - LANES, tb), BF16)


def _to_vt(v, tb):
    b, l, hd = v.shape
    h = hd // LANES
    return jnp.transpose(v.reshape(b, l // tb, tb, h, LANES), (0, 3, 1, 4, 2))


def _softmax_init(m_scr, acc_scr, heads, tb):
    for h in range(heads):
        m_scr[h] = jnp.full((1, tb), NEG_BIG, F32)
        acc_scr[h] = jnp.zeros(acc_scr.shape[1:], F32)


def _softmax_step(s, vts, m_scr, acc_scr, h):
    m_prev = m_scr[h]
    m_new = jnp.maximum(m_prev, jnp.max(s, axis=0, keepdims=True))
    alpha = jnp.exp2(m_prev - m_new)
    p = jnp.exp2(s - m_new)
    pb = p.astype(BF16)
    tk = s.shape[0] // len(vts)
    acc = alpha * acc_scr[h]
    if vts[0].shape[0] == VT_ROWS:
        for i, vt in enumerate(vts):
            acc = acc + jnp.dot(vt, pb[i * tk:(i + 1) * tk], preferred_element_type=F32)
    else:
        pv = jnp.dot(vts[0], pb[0:tk], preferred_element_type=F32)
        for i in range(1, len(vts)):
            pv = pv + jnp.dot(vts[i], pb[i * tk:(i + 1) * tk], preferred_element_type=F32)
        den = jnp.sum(p, axis=0, keepdims=True)
        acc = acc + jnp.concatenate([pv, jnp.broadcast_to(den, (VT_ROWS - LANES, den.shape[1]))], axis=0)
    acc_scr[h] = acc
    m_scr[h] = m_new


def _pipelined_blocks(n_blocks, logits_fn, consume_fn, buf_a, buf_b):
    @pl.when(n_blocks > 0)
    def _():
        logits_fn(0, buf_a)

    def two_blocks(u, carry):
        i = 2 * u
        logits_fn(i + 1, buf_b)
        consume_fn(i, buf_a)
        logits_fn(jnp.minimum(i + 2, n_blocks - 1), buf_a)
        consume_fn(i + 1, buf_b)
        return carry

    lax.fori_loop(0, n_blocks // 2, two_blocks, 0)

    @pl.when(n_blocks % 2 == 1)
    def _():
        consume_fn(n_blocks - 1, buf_a)


def _pipelined_blocks_final(n_plain, logits_fn, consume_fn, final_fn, buf_a, buf_b):
    logits_fn(0, buf_a)

    def two_blocks(u, carry):
        i = 2 * u
        logits_fn(i + 1, buf_b)
        consume_fn(i, buf_a)
        logits_fn(i + 2, buf_a)
        consume_fn(i + 1, buf_b)
        return carry

    lax.fori_loop(0, n_plain // 2, two_blocks, 0)

    @pl.when(n_plain % 2 == 0)
    def _():
        final_fn(n_plain, buf_a)

    @pl.when(n_plain % 2 == 1)
    def _():
        logits_fn(n_plain, buf_b)
        consume_fn(n_plain - 1, buf_a)
        final_fn(n_plain, buf_b)


def _softmax_finish(acc_scr, g_ref, o_ref, heads):
    for h in range(heads):
        hs = slice(h * LANES, (h + 1) * LANES)
        o = (acc_scr[h, 0:LANES, :] / acc_scr[h, LANES:LANES + 1, :]).T
        o_ref[0, :, hs] = (o * g_ref[0, :, hs]).astype(o_ref.dtype)


def _t5_bucket(rel):
    half = N_BUCKETS // 2
    max_exact = half // 2
    ret = jnp.where(rel < 0, half, 0)
    n = jnp.abs(rel)
    nf = jnp.maximum(n, 1).astype(F32)
    large = max_exact + (jnp.log(nf / max_exact) / math.log(MAX_DISTANCE / max_exact) * (half - max_exact)).astype(jnp.int32)
    large = jnp.minimum(large, half - 1)
    return ret + jnp.where(n < max_exact, n, large)


def _num_special_tiles(tb):
    return (MAX_DISTANCE - 2 + 2 * tb) // tb


def _bias_tables(t5_bias, tb):
    ns = _num_special_tiles(tb)
    s = jnp.arange(tb, dtype=I32)[:, None]
    t = jnp.arange(tb, dtype=I32)[None, :]
    bucket = jnp.stack([_t5_bucket(tb * d + t - s) for d in range(ns)])
    far = t5_bias[_t5_bucket(jnp.int32(tb * ns))]
    out = jnp.zeros((ns, A_HEADS, tb, tb), F32)
    for k in range(N_BUCKETS):
        out = jnp.where(bucket[:, None] == k, t5_bias[k][None, :, None, None], out)
    return (out - far[None, :, None, None]) * LOG2E


def _dsa_body(iq_ref, iwt_ref, ik_ref, q_ref, k_ref, vt_ref, g_ref, bias_ref, o_ref,
              key_scr, hi_scr, lo_scr, mb_scr, x_scr, m_scr, acc_scr, qka_scr, qkb_scr, *,
              tb, qt0, ns, n_sel, idx_bits):
    qt = pl.program_id(1) + qt0
    nkv = qt + 1
    krow = lax.broadcasted_iota(I32, (tb, tb), 0)
    qcol = lax.broadcasted_iota(I32, (tb, tb), 1)
    diag_vis = (krow // CHUNK) <= (qcol // CHUNK)

    wt = iwt_ref[0] * IDX_W_SCALE

    def idx_tiles(j, nt):
        kt = ik_ref[0, pl.ds(pl.multiple_of(j * tb, tb), nt * tb), :]
        acc = jnp.zeros((nt * tb, tb), F32)
        for h in range(IDX_HEADS):
            acc = acc + jnp.maximum(_dot_nt(kt, iq_ref[0, h]), 0.0) * wt[h:h + 1, :]
        acc = jnp.where(acc == 0.0, 0.0, acc)
        for i in range(nt):
            s = jnp.where(jnp.logical_or(diag_vis, j + i < qt), acc[i * tb:(i + 1) * tb], -jnp.inf)
            bits = pltpu.bitcast(s, I32)
            key = bits ^ ((bits >> 31) & 0x7FFFFFFF)
            key_scr[j + i] = key
            hi_scr[j + i] = (key >> HALF_BITS).astype(I16)
            lo_scr[j + i] = ((key & HALF_MASK) - HALF_OFFSET).astype(I16)

    def idx_pair(u, carry):
        idx_tiles(2 * u, 2)
        return carry

    def idx_tile(j, carry):
        idx_tiles(j, 1)
        return carry

    lax.fori_loop(0, nkv // 2, idx_pair, 0)
    lax.fori_loop(2 * (nkv // 2), nkv, idx_tile, 0)

    def count(pred):
        def body(j, c):
            f = jnp.where(pred(key_scr[j], j), 1.0, 0.0)
            return c + jnp.sum(f.reshape(tb // SUBLANES, SUBLANES, tb), axis=0)
        c = lax.fori_loop(0, nkv, body, jnp.zeros((SUBLANES, tb), F32))
        return jnp.sum(c, axis=0, keepdims=True)

    pack = 2 * SUBLANES
    one16, zero16 = jnp.ones((), I16), jnp.zeros((), I16)

    def count16(ref, cand, strict):
        c16 = cand.astype(I16)

        def tile_count(j):
            v = ref[j]
            f = jnp.where(v > c16 if strict else v >= c16, one16, zero16).reshape(tb // pack, pack, tb)
            part = f[0]
            for r in range(1, tb // pack):
                part = part + f[r]
            return part

        def two_tiles(u, c):
            return c + (tile_count(2 * u) + tile_count(2 * u + 1)).astype(I32)

        def one_tile(j, c):
            return c + tile_count(j).astype(I32)

        c = lax.fori_loop(0, nkv // 2, two_tiles, jnp.zeros((pack, tb), I32))
        c = lax.fori_loop(2 * (nkv // 2), nkv, one_tile, c)
        return jnp.sum(c, axis=0, keepdims=True)

    def search16(ref, target):
        v0 = jnp.where(count16(ref, jnp.zeros((1, tb), I32), False) >= target, 0, HALF_MIN).astype(I32)

        def bit_body(i, v):
            cand = v + jnp.left_shift(jnp.int32(1), HALF_BITS - 2 - i)
            return jnp.where(count16(ref, cand, False) >= target, cand, v)
        return lax.fori_loop(0, HALF_BITS - 1, bit_body, v0)

    thr_hi = search16(hi_scr, n_sel)
    need_lo = n_sel - count16(hi_scr, thr_hi, True)
    hi16 = thr_hi.astype(I16)

    def band_tile(j, carry):
        hi_scr[j] = jnp.where(hi_scr[j] == hi16, lo_scr[j], jnp.full((), HALF_MIN, I16))
        return carry

    lax.fori_loop(0, nkv, band_tile, 0)
    thr_lo = search16(hi_scr, need_lo)
    thr = thr_hi * (HALF_MASK + 1) + (thr_lo + HALF_OFFSET)
    nsel = float(n_sel)

    need = nsel - count(lambda kt, j: kt > thr)
    excess = jnp.logical_and(count(lambda kt, j: kt == thr) > need, thr > KEY_NEG_INF)
    x_scr[...] = jnp.full((1, tb), INT_MAX, I32)

    @pl.when(jnp.max(jnp.where(excess, 1.0, 0.0)) > 0.0)
    def _():
        def xbit(i, x):
            cand = x + jnp.left_shift(jnp.int32(1), idx_bits - 1 - i)
            c = count(lambda kt, j: jnp.logical_and(kt == thr, krow + j * tb < cand))
            return jnp.where(c < need, cand, x)
        x = lax.fori_loop(0, idx_bits, xbit, jnp.zeros((1, tb), I32))
        x_scr[...] = jnp.where(excess, x, INT_MAX)

    xcut = x_scr[...]

    def mask_tile(j, carry):
        kt = key_scr[j]
        tie = jnp.logical_and(kt == thr, krow + j * tb <= xcut)
        sel = jnp.logical_and(jnp.logical_or(kt > thr, tie), kt != KEY_NEG_INF)
        mb_scr[j] = jnp.where(sel, 0.0, -jnp.inf)
        return carry

    lax.fori_loop(0, nkv, mask_tile, 0)

    _softmax_init(m_scr, acc_scr, A_HEADS, tb)

    def attn_tiles(j, nt, biased):
        rows = pl.ds(pl.multiple_of(j * tb, tb), nt * tb)
        mb = mb_scr[pl.ds(j, nt)].reshape(nt * tb, tb)
        qk = [_dot_nt(k_ref[0, rows, h * A_HEAD_DIM:(h + 1) * A_HEAD_DIM],
                      q_ref[0, :, h * A_HEAD_DIM:(h + 1) * A_HEAD_DIM]) for h in range(A_HEADS)]
        for h in range(A_HEADS):
            s = mb + qk[h]
            if biased:
                s = bias_ref[qt - j, h] + s
            _softmax_step(s, [vt_ref[0, h, j + i] for i in range(nt)], m_scr, acc_scr, h)

    first = (qt + 1) % 2 if ns == 2 else 0

    def pair_logits(p, buf):
        rows = pl.ds(pl.multiple_of((first + 2 * p) * tb, tb), 2 * tb)
        for h in range(A_HEADS):
            hs = slice(h * A_HEAD_DIM, (h + 1) * A_HEAD_DIM)
            buf[h] = _dot_nt(k_ref[0, rows, hs], q_ref[0, :, hs])

    def pair_softmax(p, buf, biased=False):
        j = first + 2 * p
        mb = mb_scr[pl.ds(j, 2)].reshape(2 * tb, tb)
        for h in range(A_HEADS):
            s = mb + buf[h]
            if biased:
                s = jnp.concatenate([bias_ref[1, h], bias_ref[0, h]], axis=0) + s
            _softmax_step(s, [vt_ref[0, h, j], vt_ref[0, h, j + 1]], m_scr, acc_scr, h)

    def far_tile(j, carry):
        attn_tiles(j, 1, False)
        return carry

    def near_tile(j, carry):
        attn_tiles(j, 1, True)
        return carry

    if ns == 2:
        @pl.when(qt == 0)
        def _():
            attn_tiles(0, 1, True)

        @pl.when(jnp.logical_and(qt > 0, first == 1))
        def _():
            attn_tiles(0, 1, False)

        @pl.when(qt > 0)
        def _():
            _pipelined_blocks_final((qt + 1 - first) // 2 - 1, pair_logits, pair_softmax,
                                    functools.partial(pair_softmax, biased=True), qka_scr, qkb_scr)
    else:
        n_far = jnp.maximum(qt - (ns - 1), 0)
        n_pair = n_far // 2
        _pipelined_blocks(n_pair, pair_logits, pair_softmax, qka_scr, qkb_scr)
        lax.fori_loop(2 * n_pair, n_far, far_tile, 0)
        lax.fori_loop(n_far, nkv, near_tile, 0)
    _softmax_finish(acc_scr, g_ref, o_ref, A_HEADS)


def _dsa(name, iq, iwt, ik, q, k, vt, gate, bias_tabs, tb, past_len):
    b, t, aw = q.shape
    l = k.shape[1]
    assert t % tb == 0 and l % tb == 0 and past_len % tb == 0 and l == past_len + t
    nq, nkv = t // tb, l // tb
    ns = _num_special_tiles(tb)
    n_sel = min(TOPK_MAX, l // 4)
    body = functools.partial(_dsa_body, tb=tb, qt0=past_len // tb, ns=ns, n_sel=n_sel,
                             idx_bits=max(1, (l - 1).bit_length()))
    once = pl.Buffered(1)
    return pl.pallas_call(
        body,
        grid=(b, nq),
        in_specs=[
            pl.BlockSpec((1, IDX_HEADS, tb, IDX_DIM), lambda bi, qi: (bi, 0, qi, 0)),
            pl.BlockSpec((1, IDX_HEADS, tb), lambda bi, qi: (bi, 0, qi)),
            pl.BlockSpec((1, l, IDX_DIM), lambda bi, qi: (bi, 0, 0), pipeline_mode=once),
            pl.BlockSpec((1, tb, aw), lambda bi, qi: (bi, qi, 0)),
            pl.BlockSpec((1, l, aw), lambda bi, qi: (bi, 0, 0), pipeline_mode=once),
            pl.BlockSpec((1, A_HEADS, nkv, vt.shape[3], tb), lambda bi, qi: (bi, 0, 0, 0, 0), pipeline_mode=once),
            pl.BlockSpec((1, tb, aw), lambda bi, qi: (bi, qi, 0)),
            pl.BlockSpec(bias_tabs.shape, lambda bi, qi: (0, 0, 0, 0), pipeline_mode=once),
        ],
        out_specs=pl.BlockSpec((1, tb, aw), lambda bi, qi: (bi, qi, 0)),
        out_shape=jax.ShapeDtypeStruct((b, t, aw), BF16),
        scratch_shapes=[
            pltpu.VMEM((nkv, tb, tb), I32),
            pltpu.VMEM((nkv, tb, tb), I16),
            pltpu.VMEM((nkv, tb, tb), I16),
            pltpu.VMEM((nkv, tb, tb), F32),
            pltpu.VMEM((1, tb), I32),
            pltpu.VMEM((A_HEADS, 1, tb), F32),
            pltpu.VMEM((A_HEADS, VT_ROWS, tb), F32),
            pltpu.VMEM((A_HEADS, 2 * tb, tb), F32),
            pltpu.VMEM((A_HEADS, 2 * tb, tb), F32),
        ],
        compiler_params=_params(("parallel", "parallel")),
        name=name,
    )(iq, iwt, ik, q, k, vt, gate, bias_tabs)


def _ssd_multi_body(zg_ref, xbc_ref, dt_ref, dtt_ref, dtb_ref, dtbt_ref, alog_ref, alogt_ref,
                    dsk_ref, nw_ref, exp_ref, h0_ref, y_ref, hout_ref, h_scr, yi_scr, *, nb):
    c = pl.program_id(1)
    l = CHUNK
    gw = B_WIDTH // B_GROUPS

    @pl.when(c == 0)
    def _():
        h_scr[...] = h0_ref[...]

    a = -jnp.exp(alog_ref[...])
    at = -jnp.exp(alogt_ref[...])
    ti = lax.broadcasted_iota(I32, (l, l), 0)
    si = lax.broadcasted_iota(I32, (l, l), 1)
    causal = si <= ti
    lower = jnp.where(causal, 1.0, 0.0).astype(BF16)
    upper = jnp.where(ti <= si, 1.0, 0.0).astype(BF16)
    expand = exp_ref[...]
    lane = lax.broadcasted_iota(I32, (l, 2 * B_HEAD_DIM), 1)

    def prepare(s):
        xbc = xbc_ref[s]
        xs = xbc[:, :B_WIDTH]
        dt = jax.nn.softplus(dt_ref[s, 0] + dtb_ref[...])
        dtt = jax.nn.softplus(dtt_ref[s, 0] + dtbt_ref[...])
        acum = _dot01(lower, dt * a, split_rhs=True)
        acumt = _dot01(dtt * at, upper, split_rhs=False)
        e_full = _dot01(jnp.exp(acum), expand, split_rhs=False)
        tail_full = _dot01(jnp.exp(acum[l - 1:l, :] - acum) * dt, expand, split_rhs=False)
        return dict(xbc=xbc, xs=xs, dtt=dtt, acum=acum, acumt=acumt, e_full=e_full,
                    xt=(xs * tail_full).astype(BF16), xs16=xs.astype(BF16))

    def group(s, v, g):
        xbc = v["xbc"]
        bm = xbc[:, B_WIDTH + g * B_STATE:B_WIDTH + (g + 1) * B_STATE].astype(BF16)
        cm = xbc[:, B_WIDTH + (B_GROUPS + g) * B_STATE:B_WIDTH + (B_GROUPS + g + 1) * B_STATE].astype(BF16)
        cb = _dot_nt(cm, bm)
        gs = slice(g * gw, (g + 1) * gw)
        hg = h_scr[s, g]
        y_state = jnp.dot(cm, hg.astype(BF16), preferred_element_type=F32) * v["e_full"][:, gs]
        for pr in range(B_HPG // 2):
            ws = []
            for r in (g * B_HPG + 2 * pr, g * B_HPG + 2 * pr + 1):
                seg = v["acum"][:, r:r + 1] - v["acumt"][r:r + 1, :]
                decay = jnp.exp(jnp.where(causal, seg, -jnp.inf))
                ws.append((cb * decay * v["dtt"][r:r + 1, :]).astype(BF16))
            c0 = g * gw + pr * 2 * B_HEAD_DIM
            xp = v["xs16"][:, c0:c0 + 2 * B_HEAD_DIM]
            y0 = jnp.dot(ws[0], xp, preferred_element_type=F32)
            y1 = jnp.dot(ws[1], xp, preferred_element_type=F32)
            yi_scr[s, :, c0:c0 + 2 * B_HEAD_DIM] = jnp.where(lane < B_HEAD_DIM, y0, y1)
        yi_scr[s, :, gs] = yi_scr[s, :, gs] + y_state
        upd = lax.dot_general(bm, v["xt"][:, gs], (((0,), (0,)), ((), ())), preferred_element_type=F32)
        h_scr[s, g] = hg * v["e_full"][l - 1:l, gs] + upd

    def finish(s, v):
        y = (yi_scr[s] + dsk_ref[...] * v["xs"]) * zg_ref[s]
        for g in range(B_GROUPS):
            gs = slice(g * gw, (g + 1) * gw)
            yg = y[:, gs]
            yg = yg * lax.rsqrt(jnp.mean(yg * yg, axis=-1, keepdims=True) + EPS)
            y_ref[s, :, gs] = (yg * nw_ref[:, gs]).astype(y_ref.dtype)

    vals = [prepare(s) for s in range(nb)]
    for g in range(B_GROUPS):
        for s in range(nb):
            group(s, vals[s], g)
    for s in range(nb):
        finish(s, vals[s])

    @pl.when(c == pl.num_programs(1) - 1)
    def _():
        hout_ref[...] = h_scr[...]


def _ssd(name, zg, xbc, dt_raw, dt_bias, a_log, d_skip, norm_w, h0):
    b, t, _ = zg.shape
    nc = t // CHUNK
    gw = B_WIDTH // B_GROUPS
    dt4 = dt_raw.reshape(b, nc, CHUNK, B_HEADS)
    dtt4 = jnp.swapaxes(dt4, 2, 3)
    h0t = jnp.transpose(h0.reshape(b, B_GROUPS, B_HPG, B_HEAD_DIM, B_STATE), (0, 1, 4, 2, 3)).reshape(b, B_GROUPS, B_STATE, gw)
    expand = jnp.repeat(jnp.eye(B_HEADS, dtype=BF16), B_HEAD_DIM, axis=1)
    dsk = jnp.repeat(d_skip, B_HEAD_DIM).reshape(1, B_WIDTH)
    row = lambda v: v.reshape(1, -1)
    colv = lambda v: v.reshape(-1, 1)
    const2 = lambda bi, ci: (0, 0)
    nb = 2 if b % 2 == 0 else 1
    y, hout = pl.pallas_call(
        functools.partial(_ssd_multi_body, nb=nb),
        grid=(b // nb, nc),
        in_specs=[
            pl.BlockSpec((nb, CHUNK, B_WIDTH), lambda bi, ci: (bi, ci, 0)),
            pl.BlockSpec((nb, CHUNK, B_CONV_DIM), lambda bi, ci: (bi, ci, 0)),
            pl.BlockSpec((nb, 1, CHUNK, B_HEADS), lambda bi, ci: (bi, ci, 0, 0)),
            pl.BlockSpec((nb, 1, B_HEADS, CHUNK), lambda bi, ci: (bi, ci, 0, 0)),
            pl.BlockSpec((1, B_HEADS), const2),
            pl.BlockSpec((B_HEADS, 1), const2),
            pl.BlockSpec((1, B_HEADS), const2),
            pl.BlockSpec((B_HEADS, 1), const2),
            pl.BlockSpec((1, B_WIDTH), const2),
            pl.BlockSpec((1, B_WIDTH), const2),
            pl.BlockSpec((B_HEADS, B_WIDTH), const2),
            pl.BlockSpec((nb, B_GROUPS, B_STATE, gw), lambda bi, ci: (bi, 0, 0, 0)),
        ],
        out_specs=[pl.BlockSpec((nb, CHUNK, B_WIDTH), lambda bi, ci: (bi, ci, 0)),
                   pl.BlockSpec((nb, B_GROUPS, B_STATE, gw), lambda bi, ci: (bi, 0, 0, 0))],
        out_shape=[jax.ShapeDtypeStruct((b, t, B_WIDTH), BF16),
                   jax.ShapeDtypeStruct((b, B_GROUPS, B_STATE, gw), F32)],
        scratch_shapes=[pltpu.VMEM((nb, B_GROUPS, B_STATE, gw), F32),
                        pltpu.VMEM((nb, CHUNK, B_WIDTH), F32)],
        compiler_params=_params(("parallel", "arbitrary")),
        name=name,
    )(zg, xbc, dt4, dtt4, row(dt_bias), colv(dt_bias), row(a_log), colv(a_log),
      dsk, row(norm_w), expand, h0t)
    hnew = jnp.transpose(hout.reshape(b, B_GROUPS, B_STATE, B_HPG, B_HEAD_DIM), (0, 1, 3, 4, 2))
    return y, hnew.reshape(b, B_HEADS, B_HEAD_DIM, B_STATE)


def _mla_body(q_ref, kn_ref, kr_ref, vt_ref, g_ref, o_ref, m_scr, acc_scr, qka_scr, qkb_scr, *,
              tb, qt0, heads, pair_blocks):
    qt = pl.program_id(2) + qt0
    _softmax_init(m_scr, acc_scr, heads, tb)
    krow = lax.broadcasted_iota(I32, (tb, tb), 0)
    qcol = lax.broadcasted_iota(I32, (tb, tb), 1)
    diag_vis = (krow // CHUNK) <= (qcol // CHUNK)

    def tiles(j, nt, masked):
        rows = pl.ds(pl.multiple_of(j * tb, tb), nt * tb)
        kr = kr_ref[0, rows, :]
        qk = [_dot_nt(jnp.concatenate([kn_ref[0, rows, h * QK_NOPE:(h + 1) * QK_NOPE], kr], axis=1),
                      q_ref[0, :, h * 2 * LANES:(h + 1) * 2 * LANES]) for h in range(heads)]
        for h in range(heads):
            s = jnp.where(diag_vis, qk[h], -jnp.inf) if masked else qk[h]
            _softmax_step(s, [vt_ref[0, h, j + i] for i in range(nt)], m_scr, acc_scr, h)

    def pair_logits(p, buf):
        rows = pl.ds(pl.multiple_of(p * 2 * tb, 2 * tb), 2 * tb)
        kr = kr_ref[0, rows, :]
        for h in range(heads):
            kc = jnp.concatenate([kn_ref[0, rows, h * QK_NOPE:(h + 1) * QK_NOPE], kr], axis=1)
            buf[h] = _dot_nt(kc, q_ref[0, :, h * 2 * LANES:(h + 1) * 2 * LANES])

    def pair_softmax(p, buf):
        for h in range(heads):
            _softmax_step(buf[h], [vt_ref[0, h, 2 * p], vt_ref[0, h, 2 * p + 1]], m_scr, acc_scr, h)

    def last_pair_softmax(p, buf):
        kchunk = (p * 2 * tb + lax.broadcasted_iota(I32, (2 * tb, tb), 0)) // CHUNK
        vis = kchunk <= (qt * tb + lax.broadcasted_iota(I32, (2 * tb, tb), 1)) // CHUNK
        for h in range(heads):
            _softmax_step(jnp.where(vis, buf[h], -jnp.inf), [vt_ref[0, h, 2 * p], vt_ref[0, h, 2 * p + 1]],
                          m_scr, acc_scr, h)

    def full_tile(j, carry):
        tiles(j, 1, False)
        return carry

    if pair_blocks:
        _pipelined_blocks_final(qt // 2, pair_logits, pair_softmax, last_pair_softmax, qka_scr, qkb_scr)
    else:
        def full_pair(j2, carry):
            tiles(2 * j2, 2, False)
            return carry

        lax.fori_loop(0, qt // 2, full_pair, 0)
        lax.fori_loop(2 * (qt // 2), qt, full_tile, 0)
        tiles(qt, 1, True)
    _softmax_finish(acc_scr, g_ref, o_ref, heads)


def _mla(name, q, kn, kr, vt, gate, tb, past_len, heads=8):
    b, t, _ = q.shape
    l = kn.shape[1]
    assert t % tb == 0 and past_len % tb == 0 and l == past_len + t and C_HEADS % heads == 0
    nkv = l // tb
    body = functools.partial(_mla_body, tb=tb, qt0=past_len // tb, heads=heads, pair_blocks=nkv % 2 == 0)
    return pl.pallas_call(
        body,
        grid=(b, C_HEADS // heads, t // tb),
        in_specs=[
            pl.BlockSpec((1, tb, heads * 2 * LANES), lambda bi, h, qi: (bi, qi, h)),
            pl.BlockSpec((1, l, heads * QK_NOPE), lambda bi, h, qi: (bi, 0, h)),
            pl.BlockSpec((1, l, LANES), lambda bi, h, qi: (bi, 0, 0)),
            pl.BlockSpec((1, heads, nkv, vt.shape[3], tb), lambda bi, h, qi: (bi, h, 0, 0, 0)),
            pl.BlockSpec((1, tb, heads * V_DIM), lambda bi, h, qi: (bi, qi, h)),
        ],
        out_specs=pl.BlockSpec((1, tb, heads * V_DIM), lambda bi, h, qi: (bi, qi, h)),
        out_shape=jax.ShapeDtypeStruct((b, t, C_WIDTH), BF16),
        scratch_shapes=[pltpu.VMEM((heads, 1, tb), F32), pltpu.VMEM((heads, VT_ROWS, tb), F32),
                        pltpu.VMEM((heads, 2 * tb, tb), F32), pltpu.VMEM((heads, 2 * tb, tb), F32)],
        compiler_params=_params(("parallel", "parallel", "parallel")),
        name=name,
    )(q, kn, kr, vt, gate)


def _even_weights(w_in):
    offs = [0]
    for s in (A_WIDTH, A_WIDTH, A_WIDTH, A_WIDTH, IDX_HEADS * IDX_DIM, IDX_DIM, IDX_HEADS, B_WIDTH, B_CONV_DIM, B_HEADS):
        offs.append(offs[-1] + s)
    cols = [w_in[:, offs[i]:offs[i + 1]] for i in range(10)]
    aq, ak, av, ag, iq, ik, iw, bz, bxbc, bdt = cols
    aq = aq * (A_HEAD_DIM ** -0.5 * LOG2E)
    pad = jnp.zeros((w_in.shape[0], LANES - IDX_DIM - IDX_HEADS - B_HEADS), w_in.dtype)
    small = jnp.concatenate([ik, iw, bdt, pad], axis=1)
    return [c.astype(BF16) for c in (aq, ak, av, ag, iq, small, bz, bxbc)]


def _even_layer(tag, x, past, wts, w_out, conv_w, conv_b, dt_bias, a_log, d_skip, norm_w, ln_g, ln_b, t5_bias, tb):
    b, t, _ = x.shape
    m = b * t
    x2 = x.reshape(m, D_MODEL)
    w_aq, w_ak, w_av, w_ag, w_iq, w_small, w_bz, w_bxbc = wts
    aq, xb = _mm(tag + "_in_aq", x2, w_aq, [BF16, ("acopy",)])
    ak, ak16 = _mm(tag + "_in_ak", xb, w_ak, [F32, BF16])
    (ag,) = _mm(tag + "_in_ag", xb, w_ag, [("silu", F32)])
    (iqt,) = _mm(tag + "_in_iq", xb, w_iq, [("heads", t, IDX_DIM, BF16)])
    (small,) = _mm(tag + "_in_small", xb, w_small, [F32])
    (bzg,) = _mm(tag + "_in_bz", xb, w_bz, [("silu", F32)])
    ik = small[:, :IDX_DIM].reshape(b, t, IDX_DIM)
    iwt = jnp.swapaxes(small[:, IDX_DIM:IDX_DIM + IDX_HEADS].reshape(b, t, IDX_HEADS), 1, 2)
    bdt = small[:, IDX_DIM + IDX_HEADS:IDX_DIM + IDX_HEADS + B_HEADS].reshape(b, t, B_HEADS)
    k16 = ak16.reshape(b, t, A_WIDTH)
    ik16 = ik.astype(BF16)
    if past is None:
        p_len = 0
        av, vt = _mm(tag + "_in_av", xb, w_av, [F32, ("vt", t, tb, BF16)])
        conv0 = jnp.zeros((b, SUBLANES, B_CONV_DIM), F32)
        h0 = jnp.zeros((b, B_HEADS, B_HEAD_DIM, B_STATE), F32)
    else:
        pk, pv, pki, pconv, pssm = past
        p_len = pk.shape[1]
        av, av16 = _mm(tag + "_in_av", xb, w_av, [F32, BF16])
        k16 = jnp.concatenate([pk.reshape(b, p_len, A_WIDTH).astype(BF16), k16], axis=1)
        vt = _to_vt(jnp.concatenate([pv.reshape(b, p_len, A_WIDTH).astype(BF16), av16.reshape(b, t, A_WIDTH)], axis=1), tb)
        ik16 = jnp.concatenate([pki.astype(BF16), ik16], axis=1)
        conv0 = jnp.pad(pconv, ((0, 0), (SUBLANES - (CONV_W - 1), 0), (0, 0)))
        h0 = pssm
    xbc, tails = _mm_conv(tag + "_in_bxbc", xb, w_bxbc, conv0, conv_w, conv_b, t)
    conv_new = tails.reshape(b, -1, SUBLANES, B_CONV_DIM)[:, -1, SUBLANES - (CONV_W - 1):]
    a_out = _dsa(tag + "_dsa", iqt, iwt, ik16, aq.reshape(b, t, A_WIDTH), k16, vt,
                 ag.reshape(b, t, A_WIDTH), _bias_tables(t5_bias, tb), tb, p_len)
    b_out, ssm_new = _ssd(tag + "_ssd", bzg.reshape(b, t, B_WIDTH), xbc.reshape(b, t, B_CONV_DIM), bdt, dt_bias, a_log,
                          d_skip, norm_w, h0)
    wo = w_out.astype(BF16)
    y, y16 = _oproj_ln(tag + "_out0", [a_out.reshape(m, A_WIDTH), b_out.reshape(m, B_WIDTH)],
                       [wo[:A_WIDTH], wo[A_WIDTH:]], x2, ln_g, ln_b)
    state = (ak.reshape(b, t, A_HEADS, A_HEAD_DIM), av.reshape(b, t, A_HEADS, A_HEAD_DIM), ik, conv_new, ssm_new)
    return y.reshape(b, t, D_MODEL), y16, state


def _rope_rot_cols(w):
    half = QK_ROPE // 2
    return jnp.concatenate([-w[..., half:], w[..., :half]], axis=-1)


def _odd_weights(w_in, w_uq, w_ukv):
    w_cq = w_in[:, :Q_LORA]
    w_ckv = w_in[:, Q_LORA:Q_LORA + KV_LORA]
    w_kr = w_in[:, Q_LORA + KV_LORA:Q_LORA + KV_LORA + QK_ROPE]
    w_gate = w_in[:, Q_LORA + KV_LORA + QK_ROPE:]
    w_kr2 = jnp.concatenate([w_kr, _rope_rot_cols(w_kr)], axis=1)
    uq = w_uq.reshape(Q_LORA, C_HEADS, QK_NOPE + QK_ROPE) * (MLA_SCALE * LOG2E)
    uq_rope = uq[..., QK_NOPE:]
    uq2 = jnp.concatenate([uq[..., :QK_NOPE], uq_rope, _rope_rot_cols(uq_rope)], axis=-1).reshape(Q_LORA, C_HEADS * 2 * LANES)
    ukv = w_ukv.reshape(KV_LORA, C_HEADS, QK_NOPE + V_DIM)
    w_uk = ukv[..., :QK_NOPE].reshape(KV_LORA, C_HEADS * QK_NOPE)
    w_uv = ukv[..., QK_NOPE:].reshape(KV_LORA, C_HEADS * V_DIM)
    return [c.astype(BF16) for c in (w_cq, w_ckv, w_kr2, w_gate, uq2, w_uk, w_uv)]


def _rope_table(pos):
    half = QK_ROPE // 2
    inv = ROPE_THETA ** (-jnp.arange(half, dtype=F32) / half)
    ang = pos.astype(F32)[:, None] * inv[None, :]
    cos, sin = jnp.cos(ang), jnp.sin(ang)
    return jnp.concatenate([cos, cos, sin, sin], axis=1)


def _odd_layer(tag, x, x16, past, wts, q_norm_w, kv_norm_w, w_out, ln_g, ln_b, tb):
    b, t, _ = x.shape
    m = b * t
    w_cq, w_ckv, w_kr2, w_gate, w_uq2, w_uk, w_uv = wts
    p_len = 0 if past is None else past[0].shape[1]
    tab = _rope_table(p_len + jnp.arange(t, dtype=I32))
    (cq16,) = _mm_rms(tag + "_in_cq", x16, w_cq, q_norm_w, [BF16])
    ckv, ckv16 = _mm_rms(tag + "_in_ckv", x16, w_ckv, kv_norm_w, [F32, BF16])
    kr, kr16 = _mm_rope_k(tag + "_in_kr", x16, w_kr2, tab)
    (gate,) = _mm(tag + "_in_gate", x16, w_gate, [("silu", F32)])
    q = _mm_rope_q(tag + "_uq", cq16, w_uq2, tab)
    lat16 = ckv16.reshape(b, t, KV_LORA)
    kr16 = kr16.reshape(b, t, LANES)
    if past is not None:
        lat16 = jnp.concatenate([past[0].astype(BF16), lat16], axis=1)
        kr_past = jnp.pad(past[1], ((0, 0), (0, 0), (0, LANES - QK_ROPE))).astype(BF16)
        kr16 = jnp.concatenate([kr_past, kr16], axis=1)
    l = p_len + t
    lat2 = lat16.reshape(b * l, KV_LORA)
    wide = C_HEADS * QK_NOPE
    (kn,) = _mm(tag + "_uk", lat2, w_uk, [BF16], tn=wide)
    if tb % LANES == 0:
        (vt,) = _mm(tag + "_uv", lat2, w_uv, [("vt", l, tb, BF16)], tn=wide)
    else:
        (v,) = _mm(tag + "_uv", lat2, w_uv, [BF16], tn=wide)
        vt = _to_vt(v.reshape(b, l, C_WIDTH), tb)
    o = _mla(tag + "_mla", q.reshape(b, t, C_HEADS * 2 * LANES), kn.reshape(b, l, C_HEADS * QK_NOPE), kr16,
             vt, gate.reshape(b, t, C_WIDTH), tb, p_len)
    y, _ = _oproj_ln(tag + "_out1", [o.reshape(m, C_WIDTH)], [w_out.astype(BF16)], x.reshape(m, D_MODEL), ln_g, ln_b)
    return y.reshape(b, t, D_MODEL), (ckv.reshape(b, t, KV_LORA), kr.reshape(b, t, QK_ROPE))


def kernel(x_prompt, x_sample, cache_a_k, cache_a_v, cache_a_kidx, state_b_conv, state_b_ssm, cache_c_latent, cache_c_krope, t5_bias, w_in0, w_out0, conv_w, conv_b, dt_bias, a_log, d_skip, ssm_norm_w, ln0_g, ln0_b, w_in1, q_norm_w, kv_norm_w, w_uq, w_ukv, w_out1, ln1_g, ln1_b):
    tb_prompt = 256
    tb_sample = CHUNK
    ew = _even_weights(w_in0[0])
    eprm = (w_out0[0], conv_w[0], conv_b[0], dt_bias[0], a_log[0], d_skip[0], ssm_norm_w[0], ln0_g[0], ln0_b[0], t5_bias)
    yp, yp16, st_p = _even_layer("p0", x_prompt, None, ew, *eprm, tb_prompt)
    past = (cache_a_k[0], cache_a_v[0], cache_a_kidx[0], state_b_conv[0], state_b_ssm[0])
    ys, ys16, st_s = _even_layer("s0", x_sample, past, ew, *eprm, tb_sample)
    ow = _odd_weights(w_in1[0], w_uq[0], w_ukv[0])
    oprm = (q_norm_w[0], kv_norm_w[0], w_out1[0], ln1_g[0], ln1_b[0])
    yp, od_p = _odd_layer("p1", yp, yp16, None, ow, *oprm, tb_prompt)
    ys, od_s = _odd_layer("s1", ys, ys16, (cache_c_latent[0], cache_c_krope[0]), ow, *oprm, tb_sample)
    e = lambda a: a[None]
    return (yp, ys, e(st_p[0]), e(st_s[0]), e(st_p[1]), e(st_s[1]), e(st_p[2]), e(st_s[2]),
            e(st_p[3]), e(st_s[3]), e(st_p[4]), e(st_s[4]), e(od_p[0]), e(od_s[0]), e(od_p[1]), e(od_s[1]))
```

```python
import functools
import math

import jax
import jax.numpy as jnp
from jax import lax
from jax.experimental import pallas as pl
from jax.experimental.pallas import tpu as pltpu

F32 = jnp.float32
BF16 = jnp.bfloat16
I32 = jnp.int32
I16 = jnp.int16

D_MODEL = 2048
CHUNK = 64
A_HEADS = 8
A_HEAD_DIM = 128
A_WIDTH = A_HEADS * A_HEAD_DIM
IDX_HEADS = 16
IDX_DIM = 64
IDX_W_SCALE = (IDX_HEADS * IDX_DIM) ** -0.5
TOPK_MAX = 256
N_BUCKETS = 32
MAX_DISTANCE = 128
B_HEAD_DIM = 64
B_WIDTH = D_MODEL
B_HEADS = B_WIDTH // B_HEAD_DIM
B_GROUPS = 4
B_HPG = B_HEADS // B_GROUPS
B_STATE = 128
CONV_W = 4
B_CONV_DIM = B_WIDTH + 2 * B_GROUPS * B_STATE
C_HEADS = 16
Q_LORA = 512
KV_LORA = 512
QK_NOPE = 128
QK_ROPE = 64
V_DIM = 128
C_WIDTH = C_HEADS * V_DIM
ROPE_THETA = 10000.0
MLA_SCALE = (QK_NOPE + QK_ROPE) ** -0.5
DEPTH = 2
ALPHA = (2 * DEPTH) ** 0.25
EPS = 1e-5

LANES = 128
SUBLANES = 8
VT_ROWS = LANES + 2 * SUBLANES
VMEM_LIMIT = 56 * 1024 * 1024
LOG2E = math.log2(math.e)
INT_MIN = -(2 ** 31)
INT_MAX = 2 ** 31 - 1
KEY_NEG_INF = (0xFF800000 ^ 0x7FFFFFFF) - (1 << 32)
HALF_BITS = 16
HALF_MASK = (1 << HALF_BITS) - 1
HALF_OFFSET = 1 << (HALF_BITS - 1)
HALF_MIN = -HALF_OFFSET
NEG_BIG = -1e30


def _params(sem):
    return pltpu.CompilerParams(dimension_semantics=sem, vmem_limit_bytes=VMEM_LIMIT)


def _dot_nt(a, b):
    return lax.dot_general(a, b, (((1,), (1,)), ((), ())), preferred_element_type=F32)


def _silu(x):
    return x * (1.0 / (1.0 + jnp.exp(-x)))


def _split3(v):
    hi = v.astype(BF16)
    r = v - hi.astype(F32)
    mid = r.astype(BF16)
    return hi, mid, (r - mid.astype(F32)).astype(BF16)


def _dot01(a, b, split_rhs):
    if split_rhs:
        parts = [jnp.dot(a, t, preferred_element_type=F32) for t in _split3(b)]
    else:
        parts = [jnp.dot(t, b, preferred_element_type=F32) for t in _split3(a)]
    return parts[0] + parts[1] + parts[2]


def _mm_body(a_ref, w_ref, *refs, kinds, sub):
    o_refs, bufs = refs[:len(kinds)], refs[len(kinds):]
    tm, tn = a_ref.shape[0], w_ref.shape[1]

    def lhs(rows):
        a = a_ref[rows, :]
        return a if a.dtype == w_ref.dtype else a.astype(w_ref.dtype)

    def write(acc, r0, nr):
        rows = slice(r0, r0 + nr)
        for kind, o in zip(kinds, o_refs):
            if kind[0] == "plain":
                o[rows, :] = acc.astype(o.dtype)
            elif kind[0] == "silu":
                o[rows, :] = _silu(acc).astype(o.dtype)
            elif kind[0] == "acopy":
                @pl.when(pl.program_id(1) == 0)
                def _():
                    o[rows, :] = lhs(rows)
            elif kind[0] == "vt":
                tb = kind[2]
                for hh in range(tn // LANES):
                    for kt in range(nr // tb):
                        tile = acc[kt * tb:(kt + 1) * tb, hh * LANES:(hh + 1) * LANES]
                        o[0, hh, r0 // tb + kt, 0:LANES, :] = tile.T.astype(o.dtype)
                        o[0, hh, r0 // tb + kt, LANES:VT_ROWS, :] = _ones_rows(tb)
            elif kind[0] == "heads":
                hd = kind[2]
                for hh in range(tn // hd):
                    o[0, hh, rows, :] = acc[:, hh * hd:(hh + 1) * hd].astype(o.dtype)

    if sub == tm:
        write(jnp.dot(lhs(slice(0, tm)), w_ref[...], preferred_element_type=F32), 0, tm)
        return
    n_sub = tm // sub

    def matmul(r):
        bufs[r % 2][...] = jnp.dot(lhs(slice(r * sub, (r + 1) * sub)), w_ref[...], preferred_element_type=F32)

    matmul(0)
    for r in range(n_sub):
        if r + 1 < n_sub:
            matmul(r + 1)
        write(bufs[r % 2][...], r * sub, sub)


def _mm_rms_body(a_ref, w_ref, g_ref, *o_refs):
    acc = jnp.dot(a_ref[...], w_ref[...], preferred_element_type=F32)
    y = acc * lax.rsqrt(jnp.mean(acc * acc, axis=-1, keepdims=True) + EPS) * g_ref[...]
    for o in o_refs:
        o[...] = y.astype(o.dtype)


def _rope_half(t):
    return t + pltpu.roll(t, QK_ROPE, 1)


def _mm_rope_k_body(a_ref, w_ref, tab_ref, o32_ref, o16_ref):
    acc = jnp.dot(a_ref[...], w_ref[...], preferred_element_type=F32)
    r = _rope_half(acc * tab_ref[...])
    lane = lax.broadcasted_iota(I32, r.shape, 1)
    o32_ref[...] = r[:, :QK_ROPE]
    o16_ref[...] = jnp.where(lane < QK_ROPE, r, 0.0).astype(o16_ref.dtype)


def _mm_rope_q_body(a_ref, w_ref, tab_ref, o_ref, *, heads):
    acc = jnp.dot(a_ref[...], w_ref[...], preferred_element_type=F32)
    tab = tab_ref[...]
    lane = lax.broadcasted_iota(I32, tab.shape, 1)
    for h in range(heads):
        base = h * 2 * LANES
        o_ref[:, base:base + LANES] = acc[:, base:base + LANES].astype(o_ref.dtype)
        r = _rope_half(acc[:, base + LANES:base + 2 * LANES] * tab)
        o_ref[:, base + LANES:base + 2 * LANES] = jnp.where(lane < QK_ROPE, r, 0.0).astype(o_ref.dtype)


def _mm_call(name, body, a, w, extra, extra_specs, out_cols, out_dtypes, tm, tn):
    m, k = a.shape
    n = w.shape[1]
    tm = math.gcd(tm, m)
    tn = min(tn, n)
    assert tm % SUBLANES == 0 and n % tn == 0, (m, n, tm, tn)
    oc = [tn if c is None else c for c in out_cols]
    return pl.pallas_call(
        body,
        grid=(m // tm, n // tn),
        in_specs=[pl.BlockSpec((tm, k), lambda i, j: (i, 0)),
                  pl.BlockSpec((k, tn), lambda i, j: (0, j))] + extra_specs(tm, tn),
        out_specs=[pl.BlockSpec((tm, c), lambda i, j: (i, j)) for c in oc],
        out_shape=[jax.ShapeDtypeStruct((m, (n // tn) * c), d) for c, d in zip(oc, out_dtypes)],
        compiler_params=_params(("parallel", "parallel")),
        name=name,
    )(a, w, *extra)


def _mm(name, a, w, outs, tm=1024, tn=1024):
    m, k = a.shape
    n = w.shape[1]
    tm = math.gcd(tm, m, *[o[1] for o in outs if isinstance(o, tuple) and o[0] in ("vt", "heads")])
    tn = min(tn, n)
    assert tm % SUBLANES == 0 and n % tn == 0, (m, n, tm, tn)
    kinds, specs, shapes = [], [], []
    for o in outs:
        o = o if isinstance(o, tuple) else ("plain", o)
        if o[0] in ("plain", "silu"):
            kinds.append((o[0],))
            specs.append(pl.BlockSpec((tm, tn), lambda i, j: (i, j)))
            shapes.append(jax.ShapeDtypeStruct((m, n), o[1]))
        elif o[0] == "acopy":
            kinds.append(("acopy",))
            specs.append(pl.BlockSpec((tm, k), lambda i, j: (i, 0)))
            shapes.append(jax.ShapeDtypeStruct((m, k), w.dtype))
        elif o[0] == "vt":
            _, t, tb, dt = o
            assert t % tm == 0 and tm % tb == 0 and tn % LANES == 0
            per = t // tm
            kinds.append(("vt", t, tb))
            specs.append(pl.BlockSpec((1, tn // LANES, tm // tb, VT_ROWS, tb),
                                      lambda i, j: (i // per, j, i % per, 0, 0)))
            shapes.append(jax.ShapeDtypeStruct((m // t, n // LANES, t // tb, VT_ROWS, tb), dt))
        elif o[0] == "heads":
            _, t, hd, dt = o
            assert t % tm == 0 and tn % hd == 0
            per = t // tm
            kinds.append(("heads", t, hd))
            specs.append(pl.BlockSpec((1, tn // hd, tm, hd), lambda i, j: (i // per, j, i % per, 0)))
            shapes.append(jax.ShapeDtypeStruct((m // t, n // hd, t, hd), dt))
        else:
            raise ValueError(o)
    busy = any(kd[0] in ("silu", "vt", "heads") for kd in kinds)
    sub = math.gcd(tm, 2 * LANES) if busy else tm
    for kd in kinds:
        if kd[0] == "vt":
            sub = max(sub, kd[2])
    sub = sub if tm % sub == 0 else tm
    return pl.pallas_call(
        functools.partial(_mm_body, kinds=tuple(kinds), sub=sub),
        grid=(m // tm, n // tn),
        in_specs=[pl.BlockSpec((tm, k), lambda i, j: (i, 0)), pl.BlockSpec((k, tn), lambda i, j: (0, j))],
        out_specs=specs,
        out_shape=shapes,
        scratch_shapes=[pltpu.VMEM((sub, tn), F32)] * (2 if sub < tm else 0),
        compiler_params=_params(("parallel", "arbitrary")),
        name=name,
    )(a, w)


def _mm_conv_body(a_ref, w_ref, st_ref, cw_ref, cb_ref, o_ref, tail_ref, raw_a, raw_b, prev_scr, *,
                  tiles_per_seq, sub):
    i, j = pl.program_id(0), pl.program_id(1)
    tm, tn = o_ref.shape
    bufs = (raw_a, raw_b)

    @pl.when(i % tiles_per_seq == 0)
    def _():
        raw_a[0:SUBLANES, :] = st_ref[0]

    @pl.when(i % tiles_per_seq != 0)
    def _():
        raw_a[0:SUBLANES, :] = prev_scr[j]

    def matmul(r):
        bufs[r % 2][SUBLANES:SUBLANES + sub, :] = jnp.dot(a_ref[r * sub:(r + 1) * sub, :], w_ref[...],
                                                          preferred_element_type=F32)

    def epilogue(r):
        buf, nxt = bufs[r % 2], bufs[(r + 1) % 2]
        conv = cb_ref[...]
        for tap in range(CONV_W):
            lo = SUBLANES - (CONV_W - 1) + tap
            conv = conv + buf[lo:lo + sub, :] * cw_ref[tap:tap + 1, :]
        o_ref[r * sub:(r + 1) * sub, :] = _silu(conv).astype(o_ref.dtype)
        nxt[0:SUBLANES, :] = buf[sub:sub + SUBLANES, :]

    n_sub = tm // sub
    matmul(0)
    for r in range(n_sub):
        if r + 1 < n_sub:
            matmul(r + 1)
        epilogue(r)
    tail = bufs[n_sub % 2][0:SUBLANES, :]
    prev_scr[j] = tail
    tail_ref[0] = tail


def _mm_conv(name, a, w, state, conv_w, conv_b, t, tm=1024, tn=1024):
    m, k = a.shape
    n = w.shape[1]
    tm = math.gcd(tm, t)
    tn = min(tn, n)
    assert n % tn == 0 and tm % SUBLANES == 0
    per = t // tm
    sub = math.gcd(tm, 2 * LANES)
    return pl.pallas_call(
        functools.partial(_mm_conv_body, tiles_per_seq=per, sub=sub),
        grid=(m // tm, n // tn),
        in_specs=[pl.BlockSpec((tm, k), lambda i, j: (i, 0)),
                  pl.BlockSpec((k, tn), lambda i, j: (0, j)),
                  pl.BlockSpec((1, SUBLANES, tn), lambda i, j: (i // per, 0, j)),
                  pl.BlockSpec((CONV_W, tn), lambda i, j: (0, j)),
                  pl.BlockSpec((1, tn), lambda i, j: (0, j))],
        out_specs=[pl.BlockSpec((tm, tn), lambda i, j: (i, j)),
                   pl.BlockSpec((1, SUBLANES, tn), lambda i, j: (i, 0, j))],
        out_shape=[jax.ShapeDtypeStruct((m, n), F32), jax.ShapeDtypeStruct((m // tm, SUBLANES, n), F32)],
        scratch_shapes=[pltpu.VMEM((sub + SUBLANES, tn), F32), pltpu.VMEM((sub + SUBLANES, tn), F32),
                        pltpu.VMEM((n // tn, SUBLANES, tn), F32)],
        compiler_params=_params(("arbitrary", "arbitrary")),
        name=name,
    )(a, w, state, conv_w, conv_b.reshape(1, n))


def _mm_rms(name, a, w, g, out_dtypes, tm=1024):
    n = w.shape[1]
    return _mm_call(name, _mm_rms_body, a, w, [g.reshape(1, n)],
                    lambda tm_, tn_: [pl.BlockSpec((1, n), lambda i, j: (0, 0))],
                    [None] * len(out_dtypes), out_dtypes, tm, n)


def _tab_spec(t_rows):
    def spec(tm, tn):
        nt = t_rows // tm
        return [pl.BlockSpec((tm, LANES), lambda i, j: (i % nt, 0))]
    return spec


def _mm_rope_k(name, a, w, tab, tm=1024):
    tm = min(tm, tab.shape[0])
    return _mm_call(name, _mm_rope_k_body, a, w, [tab], _tab_spec(tab.shape[0]), [QK_ROPE, LANES], [F32, BF16],
                    tm, LANES)


def _mm_rope_q(name, a, w, tab, tm=1024, heads_per_block=8):
    tm = min(tm, tab.shape[0])
    body = functools.partial(_mm_rope_q_body, heads=heads_per_block)
    return _mm_call(name, body, a, w, [tab], _tab_spec(tab.shape[0]), [None], [BF16], tm,
                    heads_per_block * 2 * LANES)[0]


def _oproj_body(*refs, n_parts, sub):
    parts = refs[:n_parts]
    ws = refs[n_parts:2 * n_parts]
    x_ref, g_ref, b_ref, o32_ref, o16_ref, buf_a, buf_b = refs[2 * n_parts:]
    bufs = (buf_a, buf_b)
    n_sub = x_ref.shape[0] // sub

    def matmul(r):
        rows = slice(r * sub, (r + 1) * sub)
        acc = ALPHA * x_ref[rows, :]
        for p, w in zip(parts, ws):
            acc = acc + jnp.dot(p[rows, :], w[...], preferred_element_type=F32)
        bufs[r % 2][...] = acc

    def layer_norm(r):
        rows = slice(r * sub, (r + 1) * sub)
        acc = bufs[r % 2][...]
        mu = jnp.mean(acc, axis=-1, keepdims=True)
        d = acc - mu
        var = jnp.mean(d * d, axis=-1, keepdims=True)
        y = d * lax.rsqrt(var + EPS) * g_ref[...] + b_ref[...]
        o32_ref[rows, :] = y
        o16_ref[rows, :] = y.astype(o16_ref.dtype)

    matmul(0)
    for r in range(n_sub):
        if r + 1 < n_sub:
            matmul(r + 1)
        layer_norm(r)


def _oproj_ln(name, parts, ws, x, g, b, tm=512):
    m, n = x.shape
    tm = min(tm, m)
    sub = math.gcd(tm, 2 * LANES)
    np_ = len(parts)
    const = lambda i: (0, 0)
    return pl.pallas_call(
        functools.partial(_oproj_body, n_parts=np_, sub=sub),
        scratch_shapes=[pltpu.VMEM((sub, n), F32)] * 2,
        grid=(m // tm,),
        in_specs=[pl.BlockSpec((tm, p.shape[1]), lambda i: (i, 0)) for p in parts]
        + [pl.BlockSpec(w.shape, const, pipeline_mode=pl.Buffered(1)) for w in ws]
        + [pl.BlockSpec((tm, n), lambda i: (i, 0)),
           pl.BlockSpec((1, n), const), pl.BlockSpec((1, n), const)],
        out_specs=[pl.BlockSpec((tm, n), lambda i: (i, 0))] * 2,
        out_shape=[jax.ShapeDtypeStruct((m, n), F32), jax.ShapeDtypeStruct((m, n), BF16)],
        compiler_params=_params(("parallel",)),
        name=name,
    )(*parts, *ws, x, g.reshape(1, n), b.reshape(1, n))


def _ones_rows(tb):
    return jnp.ones((VT_ROWS - LANES, tb), BF16)


def _to_vt(v, tb):
    b, l, hd = v.shape
    h = hd // LANES
    return jnp.transpose(v.reshape(b, l // tb, tb, h, LANES), (0, 3, 1, 4, 2))


def _softmax_init(m_scr, acc_scr, heads, tb):
    for h in range(heads):
        m_scr[h] = jnp.full((1, tb), NEG_BIG, F32)
        acc_scr[h] = jnp.zeros(acc_scr.shape[1:], F32)


def _softmax_step(s, vts, m_scr, acc_scr, h):
    m_prev = m_scr[h]
    m_new = jnp.maximum(m_prev, jnp.max(s, axis=0, keepdims=True))
    alpha = jnp.exp2(m_prev - m_new)
    p = jnp.exp2(s - m_new)
    pb = p.astype(BF16)
    tk = s.shape[0] // len(vts)
    acc = alpha * acc_scr[h]
    if vts[0].shape[0] == VT_ROWS:
        for i, vt in enumerate(vts):
            acc = acc + jnp.dot(vt, pb[i * tk:(i + 1) * tk], preferred_element_type=F32)
    else:
        pv = jnp.dot(vts[0], pb[0:tk], preferred_element_type=F32)
        for i in range(1, len(vts)):
            pv = pv + jnp.dot(vts[i], pb[i * tk:(i + 1) * tk], preferred_element_type=F32)
        den = jnp.sum(p, axis=0, keepdims=True)
        acc = acc + jnp.concatenate([pv, jnp.broadcast_to(den, (VT_ROWS - LANES, den.shape[1]))], axis=0)
    acc_scr[h] = acc
    m_scr[h] = m_new


def _pipelined_blocks(n_blocks, logits_fn, consume_fn, buf_a, buf_b):
    @pl.when(n_blocks > 0)
    def _():
        logits_fn(0, buf_a)

    def two_blocks(u, carry):
        i = 2 * u
        logits_fn(i + 1, buf_b)
        consume_fn(i, buf_a)
        logits_fn(jnp.minimum(i + 2, n_blocks - 1), buf_a)
        consume_fn(i + 1, buf_b)
        return carry

    lax.fori_loop(0, n_blocks // 2, two_blocks, 0)

    @pl.when(n_blocks % 2 == 1)
    def _():
        consume_fn(n_blocks - 1, buf_a)


def _pipelined_blocks_final(n_plain, logits_fn, consume_fn, final_fn, buf_a, buf_b):
    logits_fn(0, buf_a)

    def two_blocks(u, carry):
        i = 2 * u
        logits_fn(i + 1, buf_b)
        consume_fn(i, buf_a)
        logits_fn(i + 2, buf_a)
        consume_fn(i + 1, buf_b)
        return carry

    lax.fori_loop(0, n_plain // 2, two_blocks, 0)

    @pl.when(n_plain % 2 == 0)
    def _():
        final_fn(n_plain, buf_a)

    @pl.when(n_plain % 2 == 1)
    def _():
        logits_fn(n_plain, buf_b)
        consume_fn(n_plain - 1, buf_a)
        final_fn(n_plain, buf_b)


def _softmax_finish(acc_scr, g_ref, o_ref, heads):
    for h in range(heads):
        hs = slice(h * LANES, (h + 1) * LANES)
        o = (acc_scr[h, 0:LANES, :] / acc_scr[h, LANES:LANES + 1, :]).T
        o_ref[0, :, hs] = (o * g_ref[0, :, hs]).astype(o_ref.dtype)


def _t5_bucket(rel):
    half = N_BUCKETS // 2
    max_exact = half // 2
    ret = jnp.where(rel < 0, half, 0)
    n = jnp.abs(rel)
    nf = jnp.maximum(n, 1).astype(F32)
    large = max_exact + (jnp.log(nf / max_exact) / math.log(MAX_DISTANCE / max_exact) * (half - max_exact)).astype(jnp.int32)
    large = jnp.minimum(large, half - 1)
    return ret + jnp.where(n < max_exact, n, large)


def _num_special_tiles(tb):
    return (MAX_DISTANCE - 2 + 2 * tb) // tb


def _bias_tables(t5_bias, tb):
    ns = _num_special_tiles(tb)
    s = jnp.arange(tb, dtype=I32)[:, None]
    t = jnp.arange(tb, dtype=I32)[None, :]
    bucket = jnp.stack([_t5_bucket(tb * d + t - s) for d in range(ns)])
    far = t5_bias[_t5_bucket(jnp.int32(tb * ns))]
    out = jnp.zeros((ns, A_HEADS, tb, tb), F32)
    for k in range(N_BUCKETS):
        out = jnp.where(bucket[:, None] == k, t5_bias[k][None, :, None, None], out)
    return (out - far[None, :, None, None]) * LOG2E


def _dsa_body(iq_ref, iwt_ref, ik_ref, q_ref, k_ref, vt_ref, g_ref, bias_ref, o_ref,
              key_scr, hi_scr, lo_scr, mb_scr, m_scr, acc_scr, qka_scr, qkb_scr, *,
              tb, qt0, ns, n_sel, idx_bits):
    qt = pl.program_id(1) + qt0
    nkv = qt + 1
    krow = lax.broadcasted_iota(I32, (tb, tb), 0)
    qcol = lax.broadcasted_iota(I32, (tb, tb), 1)
    diag_vis = (krow // CHUNK) <= (qcol // CHUNK)

    wt = iwt_ref[0] * IDX_W_SCALE

    def idx_tiles(j, nt):
        kt = ik_ref[0, pl.ds(pl.multiple_of(j * tb, tb), nt * tb), :]
        acc = jnp.zeros((nt * tb, tb), F32)
        for h in range(IDX_HEADS):
            acc = acc + jnp.maximum(_dot_nt(kt, iq_ref[0, h]), 0.0) * wt[h:h + 1, :]
        acc = jnp.where(acc == 0.0, 0.0, acc)
        for i in range(nt):
            s = jnp.where(jnp.logical_or(diag_vis, j + i < qt), acc[i * tb:(i + 1) * tb], -jnp.inf)
            bits = pltpu.bitcast(s, I32)
            key = bits ^ ((bits >> 31) & 0x7FFFFFFF)
            key_scr[j + i] = key
            hi_scr[j + i] = (key >> HALF_BITS).astype(I16)
            lo_scr[j + i] = ((key & HALF_MASK) - HALF_OFFSET).astype(I16)

    def idx_pair(u, carry):
        idx_tiles(2 * u, 2)
        return carry

    def idx_tile(j, carry):
        idx_tiles(j, 1)
        return carry

    lax.fori_loop(0, nkv // 2, idx_pair, 0)
    lax.fori_loop(2 * (nkv // 2), nkv, idx_tile, 0)

    def count(pred):
        def body(j, c):
            f = jnp.where(pred(key_scr[j], j), 1.0, 0.0)
            return c + jnp.sum(f.reshape(tb // SUBLANES, SUBLANES, tb), axis=0)
        c = lax.fori_loop(0, nkv, body, jnp.zeros((SUBLANES, tb), F32))
        return jnp.sum(c, axis=0, keepdims=True)

    pack = 2 * SUBLANES
    one16, zero16 = jnp.ones((), I16), jnp.zeros((), I16)

    def count16(ref, cand, strict):
        c16 = cand.astype(I16)

        def tile_count(j):
            v = ref[j]
            f = jnp.where(v > c16 if strict else v >= c16, one16, zero16).reshape(tb // pack, pack, tb)
            part = f[0]
            for r in range(1, tb // pack):
                part = part + f[r]
            return part

        def two_tiles(u, c):
            return c + (tile_count(2 * u) + tile_count(2 * u + 1)).astype(I32)

        def one_tile(j, c):
            return c + tile_count(j).astype(I32)

        c = lax.fori_loop(0, nkv // 2, two_tiles, jnp.zeros((pack, tb), I32))
        c = lax.fori_loop(2 * (nkv // 2), nkv, one_tile, c)
        return jnp.sum(c, axis=0, keepdims=True)

    def search16(ref, target):
        v0 = jnp.where(count16(ref, jnp.zeros((1, tb), I32), False) >= target, 0, HALF_MIN).astype(I32)

        def bit_body(i, v):
            cand = v + jnp.left_shift(jnp.int32(1), HALF_BITS - 2 - i)
            return jnp.where(count16(ref, cand, False) >= target, cand, v)
        return lax.fori_loop(0, HALF_BITS - 1, bit_body, v0)

    thr_hi = search16(hi_scr, n_sel)
    need_lo = n_sel - count16(hi_scr, thr_hi, True)
    hi16 = thr_hi.astype(I16)

    def band_tile(j, carry):
        hi_scr[j] = jnp.where(hi_scr[j] == hi16, lo_scr[j], jnp.full((), HALF_MIN, I16))
        return carry

    lax.fori_loop(0, nkv, band_tile, 0)
    thr_lo = search16(hi_scr, need_lo)
    thr = thr_hi * (HALF_MASK + 1) + (thr_lo + HALF_OFFSET)

    need = (need_lo - count16(hi_scr, thr_lo, True)).astype(F32)
    excess = jnp.logical_and(count(lambda kt, j: kt == thr) > need, thr > KEY_NEG_INF)
    any_excess = jnp.max(jnp.where(excess, 1.0, 0.0)) > 0.0

    @pl.when(any_excess)
    def _():
        def xbit(i, x):
            cand = x + jnp.left_shift(jnp.int32(1), idx_bits - 1 - i)
            c = count(lambda kt, j: jnp.logical_and(kt == thr, krow + j * tb < cand))
            return jnp.where(c < need, cand, x)
        xcut = jnp.where(excess, lax.fori_loop(0, idx_bits, xbit, jnp.zeros((1, tb), I32)), INT_MAX)

        def mask_tile(j, carry):
            kt = key_scr[j]
            tie = jnp.logical_and(kt == thr, krow + j * tb <= xcut)
            sel = jnp.logical_and(jnp.logical_or(kt > thr, tie), kt != KEY_NEG_INF)
            mb_scr[j] = jnp.where(sel, 0.0, -jnp.inf)
            return carry

        lax.fori_loop(0, nkv, mask_tile, 0)

    @pl.when(jnp.logical_not(any_excess))
    def _():
        floor = jnp.maximum(thr, KEY_NEG_INF + 1)

        def mask_tile(j, carry):
            mb_scr[j] = jnp.where(key_scr[j] >= floor, 0.0, -jnp.inf)
            return carry

        lax.fori_loop(0, nkv, mask_tile, 0)

    _softmax_init(m_scr, acc_scr, A_HEADS, tb)

    def attn_tiles(j, nt, biased):
        rows = pl.ds(pl.multiple_of(j * tb, tb), nt * tb)
        mb = mb_scr[pl.ds(j, nt)].reshape(nt * tb, tb)
        qk = [_dot_nt(k_ref[0, rows, h * A_HEAD_DIM:(h + 1) * A_HEAD_DIM],
                      q_ref[0, :, h * A_HEAD_DIM:(h + 1) * A_HEAD_DIM]) for h in range(A_HEADS)]
        for h in range(A_HEADS):
            s = mb + qk[h]
            if biased:
                s = bias_ref[qt - j, h] + s
            _softmax_step(s, [vt_ref[0, h, j + i] for i in range(nt)], m_scr, acc_scr, h)

    first = (qt + 1) % 2 if ns == 2 else 0

    def pair_logits(p, buf):
        rows = pl.ds(pl.multiple_of((first + 2 * p) * tb, tb), 2 * tb)
        for h in range(A_HEADS):
            hs = slice(h * A_HEAD_DIM, (h + 1) * A_HEAD_DIM)
            buf[h] = _dot_nt(k_ref[0, rows, hs], q_ref[0, :, hs])

    def pair_softmax(p, buf, biased=False):
        j = first + 2 * p
        mb = mb_scr[pl.ds(j, 2)].reshape(2 * tb, tb)
        for h in range(A_HEADS):
            s = mb + buf[h]
            if biased:
                s = jnp.concatenate([bias_ref[1, h], bias_ref[0, h]], axis=0) + s
            _softmax_step(s, [vt_ref[0, h, j], vt_ref[0, h, j + 1]], m_scr, acc_scr, h)

    def far_tile(j, carry):
        attn_tiles(j, 1, False)
        return carry

    def near_tile(j, carry):
        attn_tiles(j, 1, True)
        return carry

    if ns == 2:
        @pl.when(qt == 0)
        def _():
            attn_tiles(0, 1, True)

        @pl.when(jnp.logical_and(qt > 0, first == 1))
        def _():
            attn_tiles(0, 1, False)

        @pl.when(qt > 0)
        def _():
            _pipelined_blocks_final((qt + 1 - first) // 2 - 1, pair_logits, pair_softmax,
                                    functools.partial(pair_softmax, biased=True), qka_scr, qkb_scr)
    else:
        n_far = jnp.maximum(qt - (ns - 1), 0)
        n_pair = n_far // 2
        _pipelined_blocks(n_pair, pair_logits, pair_softmax, qka_scr, qkb_scr)
        lax.fori_loop(2 * n_pair, n_far, far_tile, 0)
        lax.fori_loop(n_far, nkv, near_tile, 0)
    _softmax_finish(acc_scr, g_ref, o_ref, A_HEADS)


def _dsa(name, iq, iwt, ik, q, k, vt, gate, bias_tabs, tb, past_len):
    b, t, aw = q.shape
    l = k.shape[1]
    assert t % tb == 0 and l % tb == 0 and past_len % tb == 0 and l == past_len + t
    nq, nkv = t // tb, l // tb
    ns = _num_special_tiles(tb)
    n_sel = min(TOPK_MAX, l // 4)
    body = functools.partial(_dsa_body, tb=tb, qt0=past_len // tb, ns=ns, n_sel=n_sel,
                             idx_bits=max(1, (l - 1).bit_length()))
    once = pl.Buffered(1)
    return pl.pallas_call(
        body,
        grid=(b, nq),
        in_specs=[
            pl.BlockSpec((1, IDX_HEADS, tb, IDX_DIM), lambda bi, qi: (bi, 0, qi, 0)),
            pl.BlockSpec((1, IDX_HEADS, tb), lambda bi, qi: (bi, 0, qi)),
            pl.BlockSpec((1, l, IDX_DIM), lambda bi, qi: (bi, 0, 0), pipeline_mode=once),
            pl.BlockSpec((1, tb, aw), lambda bi, qi: (bi, qi, 0)),
            pl.BlockSpec((1, l, aw), lambda bi, qi: (bi, 0, 0), pipeline_mode=once),
            pl.BlockSpec((1, A_HEADS, nkv, vt.shape[3], tb), lambda bi, qi: (bi, 0, 0, 0, 0), pipeline_mode=once),
            pl.BlockSpec((1, tb, aw), lambda bi, qi: (bi, qi, 0)),
            pl.BlockSpec(bias_tabs.shape, lambda bi, qi: (0, 0, 0, 0), pipeline_mode=once),
        ],
        out_specs=pl.BlockSpec((1, tb, aw), lambda bi, qi: (bi, qi, 0)),
        out_shape=jax.ShapeDtypeStruct((b, t, aw), BF16),
        scratch_shapes=[
            pltpu.VMEM((nkv, tb, tb), I32),
            pltpu.VMEM((nkv, tb, tb), I16),
            pltpu.VMEM((nkv, tb, tb), I16),
            pltpu.VMEM((nkv, tb, tb), F32),
            pltpu.VMEM((A_HEADS, 1, tb), F32),
            pltpu.VMEM((A_HEADS, VT_ROWS, tb), F32),
            pltpu.VMEM((A_HEADS, 2 * tb, tb), F32),
            pltpu.VMEM((A_HEADS, 2 * tb, tb), F32),
        ],
        compiler_params=_params(("parallel", "parallel")),
        name=name,
    )(iq, iwt, ik, q, k, vt, gate, bias_tabs)


def _ssd_multi_body(zg_ref, xbc_ref, dt_ref, dtt_ref, dtb_ref, dtbt_ref, alog_ref, alogt_ref,
                    dsk_ref, nw_ref, exp_ref, h0_ref, y_ref, hout_ref, h_scr, yi_scr, *, nb):
    c = pl.program_id(1)
    l = CHUNK
    gw = B_WIDTH // B_GROUPS

    @pl.when(c == 0)
    def _():
        h_scr[...] = h0_ref[...]

    a = -jnp.exp(alog_ref[...])
    at = -jnp.exp(alogt_ref[...])
    ti = lax.broadcasted_iota(I32, (l, l), 0)
    si = lax.broadcasted_iota(I32, (l, l), 1)
    causal = si <= ti
    lower = jnp.where(causal, 1.0, 0.0).astype(BF16)
    upper = jnp.where(ti <= si, 1.0, 0.0).astype(BF16)
    expand = exp_ref[...]
    lane = lax.broadcasted_iota(I32, (l, 2 * B_HEAD_DIM), 1)

    def prepare(s):
        xbc = xbc_ref[s]
        xs = xbc[:, :B_WIDTH]
        dt = jax.nn.softplus(dt_ref[s, 0] + dtb_ref[...])
        dtt = jax.nn.softplus(dtt_ref[s, 0] + dtbt_ref[...])
        acum = _dot01(lower, dt * a, split_rhs=True)
        acumt = _dot01(dtt * at, upper, split_rhs=False)
        e_full = _dot01(jnp.exp(acum), expand, split_rhs=False)
        tail_full = _dot01(jnp.exp(acum[l - 1:l, :] - acum) * dt, expand, split_rhs=False)
        return dict(xbc=xbc, xs=xs, dtt=dtt, acum=acum, acumt=acumt, e_full=e_full,
                    xt=(xs * tail_full).astype(BF16), xs16=xs.astype(BF16))

    def group(s, v, g):
        xbc = v["xbc"]
        bm = xbc[:, B_WIDTH + g * B_STATE:B_WIDTH + (g + 1) * B_STATE].astype(BF16)
        cm = xbc[:, B_WIDTH + (B_GROUPS + g) * B_STATE:B_WIDTH + (B_GROUPS + g + 1) * B_STATE].astype(BF16)
        cb = _dot_nt(cm, bm)
        gs = slice(g * gw, (g + 1) * gw)
        hg = h_scr[s, g]
        y_state = jnp.dot(cm, hg.astype(BF16), preferred_element_type=F32) * v["e_full"][:, gs]
        for pr in range(B_HPG // 2):
            ws = []
            for r in (g * B_HPG + 2 * pr, g * B_HPG + 2 * pr + 1):
                seg = v["acum"][:, r:r + 1] - v["acumt"][r:r + 1, :]
                decay = jnp.exp(jnp.where(causal, seg, -jnp.inf))
                ws.append((cb * decay * v["dtt"][r:r + 1, :]).astype(BF16))
            c0 = g * gw + pr * 2 * B_HEAD_DIM
            xp = v["xs16"][:, c0:c0 + 2 * B_HEAD_DIM]
            y0 = jnp.dot(ws[0], xp, preferred_element_type=F32)
            y1 = jnp.dot(ws[1], xp, preferred_element_type=F32)
            yi_scr[s, :, c0:c0 + 2 * B_HEAD_DIM] = jnp.where(lane < B_HEAD_DIM, y0, y1)
        yi_scr[s, :, gs] = yi_scr[s, :, gs] + y_state
        upd = lax.dot_general(bm, v["xt"][:, gs], (((0,), (0,)), ((), ())), preferred_element_type=F32)
        h_scr[s, g] = hg * v["e_full"][l - 1:l, gs] + upd

    def finish(s, v):
        y = (yi_scr[s] + dsk_ref[...] * v["xs"]) * zg_ref[s]
        for g in range(B_GROUPS):
            gs = slice(g * gw, (g + 1) * gw)
            yg = y[:, gs]
            yg = yg * lax.rsqrt(jnp.mean(yg * yg, axis=-1, keepdims=True) + EPS)
            y_ref[s, :, gs] = (yg * nw_ref[:, gs]).astype(y_ref.dtype)

    vals = [prepare(s) for s in range(nb)]
    for g in range(B_GROUPS):
        for s in range(nb):
            group(s, vals[s], g)
    for s in range(nb):
        finish(s, vals[s])

    @pl.when(c == pl.num_programs(1) - 1)
    def _():
        hout_ref[...] = h_scr[...]


def _ssd(name, zg, xbc, dt_raw, dt_bias, a_log, d_skip, norm_w, h0):
    b, t, _ = zg.shape
    nc = t // CHUNK
    gw = B_WIDTH // B_GROUPS
    dt4 = dt_raw.reshape(b, nc, CHUNK, B_HEADS)
    dtt4 = jnp.swapaxes(dt4, 2, 3)
    h0t = jnp.transpose(h0.reshape(b, B_GROUPS, B_HPG, B_HEAD_DIM, B_STATE), (0, 1, 4, 2, 3)).reshape(b, B_GROUPS, B_STATE, gw)
    expand = jnp.repeat(jnp.eye(B_HEADS, dtype=BF16), B_HEAD_DIM, axis=1)
    dsk = jnp.repeat(d_skip, B_HEAD_DIM).reshape(1, B_WIDTH)
    row = lambda v: v.reshape(1, -1)
    colv = lambda v: v.reshape(-1, 1)
    const2 = lambda bi, ci: (0, 0)
    nb = 2 if b % 2 == 0 else 1
    y, hout = pl.pallas_call(
        functools.partial(_ssd_multi_body, nb=nb),
        grid=(b // nb, nc),
        in_specs=[
            pl.BlockSpec((nb, CHUNK, B_WIDTH), lambda bi, ci: (bi, ci, 0)),
            pl.BlockSpec((nb, CHUNK, B_CONV_DIM), lambda bi, ci: (bi, ci, 0)),
            pl.BlockSpec((nb, 1, CHUNK, B_HEADS), lambda bi, ci: (bi, ci, 0, 0)),
            pl.BlockSpec((nb, 1, B_HEADS, CHUNK), lambda bi, ci: (bi, ci, 0, 0)),
            pl.BlockSpec((1, B_HEADS), const2),
            pl.BlockSpec((B_HEADS, 1), const2),
            pl.BlockSpec((1, B_HEADS), const2),
            pl.BlockSpec((B_HEADS, 1), const2),
            pl.BlockSpec((1, B_WIDTH), const2),
            pl.BlockSpec((1, B_WIDTH), const2),
            pl.BlockSpec((B_HEADS, B_WIDTH), const2),
            pl.BlockSpec((nb, B_GROUPS, B_STATE, gw), lambda bi, ci: (bi, 0, 0, 0)),
        ],
        out_specs=[pl.BlockSpec((nb, CHUNK, B_WIDTH), lambda bi, ci: (bi, ci, 0)),
                   pl.BlockSpec((nb, B_GROUPS, B_STATE, gw), lambda bi, ci: (bi, 0, 0, 0))],
        out_shape=[jax.ShapeDtypeStruct((b, t, B_WIDTH), BF16),
                   jax.ShapeDtypeStruct((b, B_GROUPS, B_STATE, gw), F32)],
        scratch_shapes=[pltpu.VMEM((nb, B_GROUPS, B_STATE, gw), F32),
                        pltpu.VMEM((nb, CHUNK, B_WIDTH), F32)],
        compiler_params=_params(("parallel", "arbitrary")),
        name=name,
    )(zg, xbc, dt4, dtt4, row(dt_bias), colv(dt_bias), row(a_log), colv(a_log),
      dsk, row(norm_w), expand, h0t)
    hnew = jnp.transpose(hout.reshape(b, B_GROUPS, B_STATE, B_HPG, B_HEAD_DIM), (0, 1, 3, 4, 2))
    return y, hnew.reshape(b, B_HEADS, B_HEAD_DIM, B_STATE)


def _mla_body(q_ref, kn_ref, kr_ref, vt_ref, g_ref, o_ref, m_scr, acc_scr, qka_scr, qkb_scr, *,
              tb, qt0, heads, pair_blocks):
    qt = pl.program_id(2) + qt0
    _softmax_init(m_scr, acc_scr, heads, tb)
    krow = lax.broadcasted_iota(I32, (tb, tb), 0)
    qcol = lax.broadcasted_iota(I32, (tb, tb), 1)
    diag_vis = (krow // CHUNK) <= (qcol // CHUNK)

    def tiles(j, nt, masked):
        rows = pl.ds(pl.multiple_of(j * tb, tb), nt * tb)
        kr = kr_ref[0, rows, :]
        qk = [_dot_nt(jnp.concatenate([kn_ref[0, rows, h * QK_NOPE:(h + 1) * QK_NOPE], kr], axis=1),
                      q_ref[0, :, h * 2 * LANES:(h + 1) * 2 * LANES]) for h in range(heads)]
        for h in range(heads):
            s = jnp.where(diag_vis, qk[h], -jnp.inf) if masked else qk[h]
            _softmax_step(s, [vt_ref[0, h, j + i] for i in range(nt)], m_scr, acc_scr, h)

    def pair_logits(p, buf):
        rows = pl.ds(pl.multiple_of(p * 2 * tb, 2 * tb), 2 * tb)
        kr = kr_ref[0, rows, :]
        for h in range(heads):
            kc = jnp.concatenate([kn_ref[0, rows, h * QK_NOPE:(h + 1) * QK_NOPE], kr], axis=1)
            buf[h] = _dot_nt(kc, q_ref[0, :, h * 2 * LANES:(h + 1) * 2 * LANES])

    def pair_softmax(p, buf):
        for h in range(heads):
            _softmax_step(buf[h], [vt_ref[0, h, 2 * p], vt_ref[0, h, 2 * p + 1]], m_scr, acc_scr, h)

    def last_pair_softmax(p, buf):
        kchunk = (p * 2 * tb + lax.broadcasted_iota(I32, (2 * tb, tb), 0)) // CHUNK
        vis = kchunk <= (qt * tb + lax.broadcasted_iota(I32, (2 * tb, tb), 1)) // CHUNK
        for h in range(heads):
            _softmax_step(jnp.where(vis, buf[h], -jnp.inf), [vt_ref[0, h, 2 * p], vt_ref[0, h, 2 * p + 1]],
                          m_scr, acc_scr, h)

    def full_tile(j, carry):
        tiles(j, 1, False)
        return carry

    if pair_blocks:
        _pipelined_blocks_final(qt // 2, pair_logits, pair_softmax, last_pair_softmax, qka_scr, qkb_scr)
    else:
        def full_pair(j2, carry):
            tiles(2 * j2, 2, False)
            return carry

        lax.fori_loop(0, qt // 2, full_pair, 0)
        lax.fori_loop(2 * (qt // 2), qt, full_tile, 0)
        tiles(qt, 1, True)
    _softmax_finish(acc_scr, g_ref, o_ref, heads)


def _mla(name, q, kn, kr, vt, gate, tb, past_len, heads=8):
    b, t, _ = q.shape
    l = kn.shape[1]
    assert t % tb == 0 and past_len % tb == 0 and l == past_len + t and C_HEADS % heads == 0
    nkv = l // tb
    body = functools.partial(_mla_body, tb=tb, qt0=past_len // tb, heads=heads, pair_blocks=nkv % 2 == 0)
    return pl.pallas_call(
        body,
        grid=(b, C_HEADS // heads, t // tb),
        in_specs=[
            pl.BlockSpec((1, tb, heads * 2 * LANES), lambda bi, h, qi: (bi, qi, h)),
            pl.BlockSpec((1, l, heads * QK_NOPE), lambda bi, h, qi: (bi, 0, h)),
            pl.BlockSpec((1, l, LANES), lambda bi, h, qi: (bi, 0, 0)),
            pl.BlockSpec((1, heads, nkv, vt.shape[3], tb), lambda bi, h, qi: (bi, h, 0, 0, 0)),
            pl.BlockSpec((1, tb, heads * V_DIM), lambda bi, h, qi: (bi, qi, h)),
        ],
        out_specs=pl.BlockSpec((1, tb, heads * V_DIM), lambda bi, h, qi: (bi, qi, h)),
        out_shape=jax.ShapeDtypeStruct((b, t, C_WIDTH), BF16),
        scratch_shapes=[pltpu.VMEM((heads, 1, tb), F32), pltpu.VMEM((heads, VT_ROWS, tb), F32),
                        pltpu.VMEM((heads, 2 * tb, tb), F32), pltpu.VMEM((heads, 2 * tb, tb), F32)],
        compiler_params=_params(("parallel", "parallel", "parallel")),
        name=name,
    )(q, kn, kr, vt, gate)


def _even_weights(w_in):
    offs = [0]
    for s in (A_WIDTH, A_WIDTH, A_WIDTH, A_WIDTH, IDX_HEADS * IDX_DIM, IDX_DIM, IDX_HEADS, B_WIDTH, B_CONV_DIM, B_HEADS):
        offs.append(offs[-1] + s)
    cols = [w_in[:, offs[i]:offs[i + 1]] for i in range(10)]
    aq, ak, av, ag, iq, ik, iw, bz, bxbc, bdt = cols
    aq = aq * (A_HEAD_DIM ** -0.5 * LOG2E)
    pad = jnp.zeros((w_in.shape[0], LANES - IDX_DIM - IDX_HEADS - B_HEADS), w_in.dtype)
    small = jnp.concatenate([ik, iw, bdt, pad], axis=1)
    return [c.astype(BF16) for c in (aq, ak, av, ag, iq, small, bz, bxbc)]


def _even_layer(tag, x, past, wts, w_out, conv_w, conv_b, dt_bias, a_log, d_skip, norm_w, ln_g, ln_b, t5_bias, tb):
    b, t, _ = x.shape
    m = b * t
    x2 = x.reshape(m, D_MODEL)
    w_aq, w_ak, w_av, w_ag, w_iq, w_small, w_bz, w_bxbc = wts
    aq, xb = _mm(tag + "_in_aq", x2, w_aq, [BF16, ("acopy",)])
    ak, ak16 = _mm(tag + "_in_ak", xb, w_ak, [F32, BF16])
    (ag,) = _mm(tag + "_in_ag", xb, w_ag, [("silu", F32)])
    (iqt,) = _mm(tag + "_in_iq", xb, w_iq, [("heads", t, IDX_DIM, BF16)])
    (small,) = _mm(tag + "_in_small", xb, w_small, [F32])
    (bzg,) = _mm(tag + "_in_bz", xb, w_bz, [("silu", F32)])
    ik = small[:, :IDX_DIM].reshape(b, t, IDX_DIM)
    iwt = jnp.swapaxes(small[:, IDX_DIM:IDX_DIM + IDX_HEADS].reshape(b, t, IDX_HEADS), 1, 2)
    bdt = small[:, IDX_DIM + IDX_HEADS:IDX_DIM + IDX_HEADS + B_HEADS].reshape(b, t, B_HEADS)
    k16 = ak16.reshape(b, t, A_WIDTH)
    ik16 = ik.astype(BF16)
    if past is None:
        p_len = 0
        av, vt = _mm(tag + "_in_av", xb, w_av, [F32, ("vt", t, tb, BF16)])
        conv0 = jnp.zeros((b, SUBLANES, B_CONV_DIM), F32)
        h0 = jnp.zeros((b, B_HEADS, B_HEAD_DIM, B_STATE), F32)
    else:
        pk, pv, pki, pconv, pssm = past
        p_len = pk.shape[1]
        av, av16 = _mm(tag + "_in_av", xb, w_av, [F32, BF16])
        k16 = jnp.concatenate([pk.reshape(b, p_len, A_WIDTH).astype(BF16), k16], axis=1)
        vt = _to_vt(jnp.concatenate([pv.reshape(b, p_len, A_WIDTH).astype(BF16), av16.reshape(b, t, A_WIDTH)], axis=1), tb)
        ik16 = jnp.concatenate([pki.astype(BF16), ik16], axis=1)
        conv0 = jnp.pad(pconv, ((0, 0), (SUBLANES - (CONV_W - 1), 0), (0, 0)))
        h0 = pssm
    xbc, tails = _mm_conv(tag + "_in_bxbc", xb, w_bxbc, conv0, conv_w, conv_b, t)
    conv_new = tails.reshape(b, -1, SUBLANES, B_CONV_DIM)[:, -1, SUBLANES - (CONV_W - 1):]
    a_out = _dsa(tag + "_dsa", iqt, iwt, ik16, aq.reshape(b, t, A_WIDTH), k16, vt,
                 ag.reshape(b, t, A_WIDTH), _bias_tables(t5_bias, tb), tb, p_len)
    b_out, ssm_new = _ssd(tag + "_ssd", bzg.reshape(b, t, B_WIDTH), xbc.reshape(b, t, B_CONV_DIM), bdt, dt_bias, a_log,
                          d_skip, norm_w, h0)
    wo = w_out.astype(BF16)
    y, y16 = _oproj_ln(tag + "_out0", [a_out.reshape(m, A_WIDTH), b_out.reshape(m, B_WIDTH)],
                       [wo[:A_WIDTH], wo[A_WIDTH:]], x2, ln_g, ln_b)
    state = (ak.reshape(b, t, A_HEADS, A_HEAD_DIM), av.reshape(b, t, A_HEADS, A_HEAD_DIM), ik, conv_new, ssm_new)
    return y.reshape(b, t, D_MODEL), y16, state


def _rope_rot_cols(w):
    half = QK_ROPE // 2
    return jnp.concatenate([-w[..., half:], w[..., :half]], axis=-1)


def _odd_weights(w_in, w_uq, w_ukv):
    w_cq = w_in[:, :Q_LORA]
    w_ckv = w_in[:, Q_LORA:Q_LORA + KV_LORA]
    w_kr = w_in[:, Q_LORA + KV_LORA:Q_LORA + KV_LORA + QK_ROPE]
    w_gate = w_in[:, Q_LORA + KV_LORA + QK_ROPE:]
    w_kr2 = jnp.concatenate([w_kr, _rope_rot_cols(w_kr)], axis=1)
    uq = w_uq.reshape(Q_LORA, C_HEADS, QK_NOPE + QK_ROPE) * (MLA_SCALE * LOG2E)
    uq_rope = uq[..., QK_NOPE:]
    uq2 = jnp.concatenate([uq[..., :QK_NOPE], uq_rope, _rope_rot_cols(uq_rope)], axis=-1).reshape(Q_LORA, C_HEADS * 2 * LANES)
    ukv = w_ukv.reshape(KV_LORA, C_HEADS, QK_NOPE + V_DIM)
    w_uk = ukv[..., :QK_NOPE].reshape(KV_LORA, C_HEADS * QK_NOPE)
    w_uv = ukv[..., QK_NOPE:].reshape(KV_LORA, C_HEADS * V_DIM)
    return [c.astype(BF16) for c in (w_cq, w_ckv, w_kr2, w_gate, uq2, w_uk, w_uv)]


def _rope_table(pos):
    half = QK_ROPE // 2
    inv = ROPE_THETA ** (-jnp.arange(half, dtype=F32) / half)
    ang = pos.astype(F32)[:, None] * inv[None, :]
    cos, sin = jnp.cos(ang), jnp.sin(ang)
    return jnp.concatenate([cos, cos, sin, sin], axis=1)


def _odd_layer(tag, x, x16, past, wts, q_norm_w, kv_norm_w, w_out, ln_g, ln_b, tb):
    b, t, _ = x.shape
    m = b * t
    w_cq, w_ckv, w_kr2, w_gate, w_uq2, w_uk, w_uv = wts
    p_len = 0 if past is None else past[0].shape[1]
    tab = _rope_table(p_len + jnp.arange(t, dtype=I32))
    (cq16,) = _mm_rms(tag + "_in_cq", x16, w_cq, q_norm_w, [BF16])
    ckv, ckv16 = _mm_rms(tag + "_in_ckv", x16, w_ckv, kv_norm_w, [F32, BF16])
    kr, kr16 = _mm_rope_k(tag + "_in_kr", x16, w_kr2, tab)
    (gate,) = _mm(tag + "_in_gate", x16, w_gate, [("silu", F32)])
    q = _mm_rope_q(tag + "_uq", cq16, w_uq2, tab)
    lat16 = ckv16.reshape(b, t, KV_LORA)
    kr16 = kr16.reshape(b, t, LANES)
    if past is not None:
        lat16 = jnp.concatenate([past[0].astype(BF16), lat16], axis=1)
        kr_past = jnp.pad(past[1], ((0, 0), (0, 0), (0, LANES - QK_ROPE))).astype(BF16)
        kr16 = jnp.concatenate([kr_past, kr16], axis=1)
    l = p_len + t
    lat2 = lat16.reshape(b * l, KV_LORA)
    wide = C_HEADS * QK_NOPE
    (kn,) = _mm(tag + "_uk", lat2, w_uk, [BF16], tn=wide)
    if tb % LANES == 0:
        (vt,) = _mm(tag + "_uv", lat2, w_uv, [("vt", l, tb, BF16)], tn=wide)
    else:
        (v,) = _mm(tag + "_uv", lat2, w_uv, [BF16], tn=wide)
        vt = _to_vt(v.reshape(b, l, C_WIDTH), tb)
    o = _mla(tag + "_mla", q.reshape(b, t, C_HEADS * 2 * LANES), kn.reshape(b, l, C_HEADS * QK_NOPE), kr16,
             vt, gate.reshape(b, t, C_WIDTH), tb, p_len)
    y, _ = _oproj_ln(tag + "_out1", [o.reshape(m, C_WIDTH)], [w_out.astype(BF16)], x.reshape(m, D_MODEL), ln_g, ln_b)
    return y.reshape(b, t, D_MODEL), (ckv.reshape(b, t, KV_LORA), kr.reshape(b, t, QK_ROPE))


def kernel(x_prompt, x_sample, cache_a_k, cache_a_v, cache_a_kidx, state_b_conv, state_b_ssm, cache_c_latent, cache_c_krope, t5_bias, w_in0, w_out0, conv_w, conv_b, dt_bias, a_log, d_skip, ssm_norm_w, ln0_g, ln0_b, w_in1, q_norm_w, kv_norm_w, w_uq, w_ukv, w_out1, ln1_g, ln1_b):
    tb_prompt = 256
    tb_sample = CHUNK
    ew = _even_weights(w_in0[0])
    eprm = (w_out0[0], conv_w[0], conv_b[0], dt_bias[0], a_log[0], d_skip[0], ssm_norm_w[0], ln0_g[0], ln0_b[0], t5_bias)
    yp, yp16, st_p = _even_layer("p0", x_prompt, None, ew, *eprm, tb_prompt)
    past = (cache_a_k[0], cache_a_v[0], cache_a_kidx[0], state_b_conv[0], state_b_ssm[0])
    ys, ys16, st_s = _even_layer("s0", x_sample, past, ew, *eprm, tb_sample)
    ow = _odd_weights(w_in1[0], w_uq[0], w_ukv[0])
    oprm = (q_norm_w[0], kv_norm_w[0], w_out1[0], ln1_g[0], ln1_b[0])
    yp, od_p = _odd_layer("p1", yp, yp16, None, ow, *oprm, tb_prompt)
    ys, od_s = _odd_layer("s1", ys, ys16, (cache_c_latent[0], cache_c_krope[0]), ow, *oprm, tb_sample)
    e = lambda a: a[None]
    return (yp, ys, e(st_p[0]), e(st_s[0]), e(st_p[1]), e(st_s[1]), e(st_p[2]), e(st_s[2]),
            e(st_p[3]), e(st_s[3]), e(st_p[4]), e(st_s[4]), e(od_p[0]), e(od_s[0]), e(od_p[1]), e(od_s[1]))
```
